```python
import math
import jax, jax.numpy as jnp
from jax import lax
import numpy as np

D_MODEL = 1024
BATCH = 16
SEQ = 4096
DEPTH = 4

GRID_W = 64
CTX_LEN = 256
HEAD_DIM = 64
N_Q_HEADS = D_MODEL // (2 * HEAD_DIM)
N_KV_HEADS = max(1, N_Q_HEADS // 4)
GQA_GROUP = N_Q_HEADS // N_KV_HEADS
Q_W = N_Q_HEADS * HEAD_DIM
KV_W = N_KV_HEADS * HEAD_DIM
RNN_W = D_MODEL // 2
RNN_BLOCKS = RNN_W // HEAD_DIM
RNN_BW = RNN_W // RNN_BLOCKS
IN_W = Q_W + 2 * KV_W + 2 * RNN_W
MIX_W = Q_W + RNN_W
CONV_K = 4
CONV_LEFT = 2
RG_C = 8.0
ROPE_THETA = 10000.0
Q_BLOCK = 128
RWKV_HEADS = D_MODEL // HEAD_DIM
RWKV_LORA = 64
RWKV_GATE_LORA = 128
DECAY_SCALE = math.exp(-0.5)
GN_EPS = 64e-5
D_FF = 4 * D_MODEL
N_EVEN = (DEPTH + 1) // 2
N_ODD = DEPTH // 2
EPS = 1e-6
F32 = jnp.float32

kernel_name = "hybrid_rglru_gqa_rwkv7_dit_prefix"


def rmsnorm(x, g):
    xf = x.astype(F32)
    y = xf * lax.rsqrt(jnp.mean(xf * xf, axis=-1, keepdims=True) + EPS)
    return (y * g.astype(F32)).astype(x.dtype)


def ada_chunks(cvec, w_mod, b_mod):
    m = jnp.dot(jax.nn.silu(cvec), w_mod) + b_mod
    return jnp.split(m, 6, axis=-1)


def axial_rope(n):
    rows = n // GRID_W
    row = jnp.repeat(jnp.arange(rows, dtype=F32), GRID_W)
    col = jnp.tile(jnp.arange(GRID_W, dtype=F32), rows)
    half = HEAD_DIM // 2
    inv = ROPE_THETA ** (-jnp.arange(0, half, 2, dtype=F32) / half)
    ang = jnp.concatenate([row[:, None] * inv, col[:, None] * inv], axis=-1)
    return jnp.cos(ang), jnp.sin(ang)


def apply_rope(x, cos, sin):
    x1 = x[..., 0::2].astype(F32)
    x2 = x[..., 1::2].astype(F32)
    y = jnp.stack([x1 * cos - x2 * sin, x1 * sin + x2 * cos], axis=-1)
    return y.reshape(x.shape).astype(x.dtype)


def attend(q, k, v):
    s = jnp.einsum('bqhgd,bkhd->bhgqk', q, k).astype(F32) * HEAD_DIM ** -0.5
    p = jax.nn.softmax(s, axis=-1).astype(v.dtype)
    return jnp.einsum('bhgqk,bkhd->bqhgd', p, v)


def blocked_attend(q, k, v):
    b, n = q.shape[0], q.shape[1]
    nb = n // Q_BLOCK
    qb = jnp.moveaxis(q.reshape(b, nb, Q_BLOCK, N_KV_HEADS, GQA_GROUP, HEAD_DIM), 1, 0)
    o = lax.map(lambda qi: attend(qi, k, v), qb)
    return jnp.moveaxis(o, 0, 1).reshape(b, n, Q_W)


def dwconv(x, w, bias):
    t = x.shape[1]
    xp = jnp.pad(x, ((0, 0), (CONV_LEFT, CONV_K - 1 - CONV_LEFT), (0, 0)))
    out = bias
    for j in range(CONV_K):
        out = out + xp[:, j:j + t] * w[j]
    return out


def rglru_coeffs(x, w, bias, lam):
    xf = x.astype(F32)
    xb = xf.reshape(xf.shape[0], xf.shape[1], RNN_BLOCKS, RNN_BW)
    gates = jnp.einsum('btnd,gnde->gbtne', xb, w.astype(F32)).reshape((2,) + xf.shape)
    gates = gates + bias.astype(F32)[:, None, None, :]
    r = jax.nn.sigmoid(gates[0])
    i = jax.nn.sigmoid(gates[1])
    log_a = -RG_C * r * jax.nn.softplus(-lam.astype(F32))
    a = jnp.exp(log_a)
    u = jnp.sqrt(-jnp.expm1(2.0 * log_a)) * (i * xf)
    return a, u


def linear_scan(a, u, h0, reverse):
    if h0 is not None:
        idx = -1 if reverse else 0
        u = u.at[:, idx].add(a[:, idx] * h0)

    def comb(l, r):
        return l[0] * r[0], r[0] * l[1] + r[1]

    _, h = lax.associative_scan(comb, (a, u), reverse=reverse, axis=1)
    return h


def hybrid_mixer(hl, hc, w_in, w_out, q_norm, k_norm, conv_w, conv_b, gate_w, gate_b, lam, cos, sin, need_ctx):
    b, n, _ = hl.shape
    cl = hc.shape[1]
    cuts = [Q_W, Q_W + KV_W, Q_W + 2 * KV_W, Q_W + 2 * KV_W + RNN_W]
    ql, kl, vl, xl, gl = jnp.split(hl @ w_in, cuts, axis=-1)
    qc, kc, vc, xc, gc = jnp.split(hc @ w_in, cuts, axis=-1)
    qlh = apply_rope(rmsnorm(ql.reshape(b, n, N_KV_HEADS, GQA_GROUP, HEAD_DIM), q_norm),
                     cos[None, :, None, None, :], sin[None, :, None, None, :])
    klh = apply_rope(rmsnorm(kl.reshape(b, n, N_KV_HEADS, HEAD_DIM), k_norm),
                     cos[None, :, None, :], sin[None, :, None, :])
    kch = rmsnorm(kc.reshape(b, cl, N_KV_HEADS, HEAD_DIM), k_norm)
    vch = vc.reshape(b, cl, N_KV_HEADS, HEAD_DIM)
    k_all = jnp.concatenate([kch, klh], axis=1)
    v_all = jnp.concatenate([vch, vl.reshape(b, n, N_KV_HEADS, HEAD_DIM)], axis=1)
    att_l = blocked_attend(qlh, k_all, v_all)
    xl = dwconv(xl, conv_w, conv_b)
    xc = dwconv(xc, conv_w, conv_b)
    h_lat, h_ctx = [], []
    for d, rev in enumerate((False, True)):
        a_c, u_c = rglru_coeffs(xc, gate_w[d], gate_b[d], lam[d])
        hcd = linear_scan(a_c, u_c, None, rev)
        h0 = hcd[:, 0] if rev else hcd[:, -1]
        a_l, u_l = rglru_coeffs(xl, gate_w[d], gate_b[d], lam[d])
        h_lat.append(linear_scan(a_l, u_l, h0, rev))
        h_ctx.append(hcd)
    rec_l = (jax.nn.gelu(gl.astype(F32)) * (h_lat[0] + h_lat[1])).astype(hl.dtype)
    out_l = jnp.concatenate([att_l, rec_l], axis=-1) @ w_out
    out_c = None
    if need_ctx:
        qch = rmsnorm(qc.reshape(b, cl, N_KV_HEADS, GQA_GROUP, HEAD_DIM), q_norm)
        att_c = attend(qch, kch, vch).reshape(b, cl, Q_W)
        rec_c = (jax.nn.gelu(gc.astype(F32)) * (h_ctx[0] + h_ctx[1])).astype(hc.dtype)
        out_c = jnp.concatenate([att_c, rec_c], axis=-1) @ w_out
    return out_l, out_c


def rwkv7_project(h, mu, w_rkv, lora_down, lora_up, lora_bias, gate_down, gate_up, k_k, k_a):
    b, t, _ = h.shape
    hp = jnp.pad(h, ((0, 0), (1, 1), (0, 0)))
    xx = 0.5 * (hp[:, :-2] + hp[:, 2:]) - h
    lerp = lambda j: h + xx * mu[j]
    heads = lambda z: z.astype(F32).reshape(b, t, RWKV_HEADS, HEAD_DIM)
    r = heads(lerp(0) @ w_rkv[0])
    k = heads(lerp(2) @ w_rkv[1])
    v = heads(lerp(3) @ w_rkv[2])
    g = jax.nn.sigmoid(lerp(5) @ gate_down) @ gate_up
    kk = k * k_k.astype(F32).reshape(RWKV_HEADS, HEAD_DIM)
    kk = kk / jnp.maximum(jnp.sqrt(jnp.sum(kk * kk, axis=-1, keepdims=True)), 1e-12)
    k_a_h = k_a.astype(F32).reshape(RWKV_HEADS, HEAD_DIM)
    xw, xa = lerp(1), lerp(4)
    dirs = []
    for d in range(2):
        dec = heads(lora_bias[d, 0] + jnp.tanh(xw @ lora_down[d, 0]) @ lora_up[d, 0])
        w = jnp.exp(-DECAY_SCALE * jax.nn.sigmoid(dec))
        a = jax.nn.sigmoid(heads(lora_bias[d, 1] + (xa @ lora_down[d, 1]) @ lora_up[d, 1]))
        kd = k * (1.0 + (a - 1.0) * k_a_h)
        dirs.append((w, kk * a, kd))
    return r, kk, v, g, dirs


def wkv_scan(r, w, kk, bvec, k, v, s0, reverse):
    def step(s, inp):
        r_t, w_t, kk_t, b_t, k_t, v_t = inp
        sk = jnp.einsum('bhvk,bhk->bhv', s, kk_t)
        s = s * w_t[:, :, None, :] - sk[..., None] * b_t[:, :, None, :] + v_t[..., None] * k_t[:, :, None, :]
        return s, jnp.einsum('bhvk,bhk->bhv', s, r_t)

    xs = tuple(jnp.moveaxis(z, 1, 0) for z in (r, w, kk, bvec, k, v))
    s_fin, y = lax.scan(step, s0, xs, reverse=reverse)
    return s_fin, jnp.moveaxis(y, 0, 1)


def rwkv7_out(y, r, v, kds, g, r_k, gn_g, gn_b, w_o):
    b, t = y.shape[0], y.shape[1]
    mean = jnp.mean(y, axis=-1, keepdims=True)
    var = jnp.mean(jnp.square(y - mean), axis=-1, keepdims=True)
    yn = ((y - mean) * lax.rsqrt(var + GN_EPS)).reshape(b, t, D_MODEL)
    rk = r_k.astype(F32)
    bonus = sum(jnp.sum(r * kd * rk, axis=-1, keepdims=True) for kd in kds) * v
    o = (yn * gn_g.astype(F32) + gn_b.astype(F32) + bonus.reshape(b, t, D_MODEL)) * g
    return o.astype(g.dtype) @ w_o


def rwkv7_mixer(hl, hc, mu, w_rkv, w_o, lora_down, lora_up, lora_bias, gate_down, gate_up,
                k_k, k_a, r_k, gn_g, gn_b, need_ctx):
    proj = lambda h: rwkv7_project(h, mu, w_rkv, lora_down, lora_up, lora_bias, gate_down, gate_up, k_k, k_a)
    rl, kkl, vl, gl, dl = proj(hl)
    rc, kkc, vc, gc, dc = proj(hc)
    s0 = jnp.zeros((hl.shape[0], RWKV_HEADS, HEAD_DIM, HEAD_DIM), F32)
    ys_l, ys_c = [], []
    for d, rev in enumerate((False, True)):
        s_c, y_c = wkv_scan(rc, dc[d][0], kkc, dc[d][1], dc[d][2], vc, s0, rev)
        _, y_l = wkv_scan(rl, dl[d][0], kkl, dl[d][1], dl[d][2], vl, s_c, rev)
        ys_l.append(y_l)
        ys_c.append(y_c)
    out_l = rwkv7_out(ys_l[0] + ys_l[1], rl, vl, [dl[0][2], dl[1][2]], gl, r_k, gn_g, gn_b, w_o)
    out_c = None
    if need_ctx:
        out_c = rwkv7_out(ys_c[0] + ys_c[1], rc, vc, [dc[0][2], dc[1][2]], gc, r_k, gn_g, gn_b, w_o)
    return out_l, out_c


def sqrelu_mlp(h, w1, w2):
    return jnp.square(jax.nn.relu(h @ w1)) @ w2


def setup_inputs(seed: int = 0) -> dict:
    key = jax.random.key(seed)
    ks = iter(jax.random.split(key, 40))
    nrm = lambda shape, scale: scale * jax.random.normal(next(ks), shape, F32)
    uni = lambda shape, lo, hi: jax.random.uniform(next(ks), shape, F32, lo, hi)
    D = D_MODEL
    x = nrm((BATCH, SEQ, D), 1.0)
    c = nrm((BATCH, D), 1.0)
    ctx = nrm((BATCH, CTX_LEN, D), 1.0)
    c_ctx = nrm((D,), 1.0)
    ada_w = nrm((DEPTH, D, 6 * D), 0.5 * D ** -0.5)
    ada_b = nrm((DEPTH, 6 * D), 0.02)
    norm_g = 1.0 + nrm((DEPTH, 2, D), 0.05)
    mlp_w1 = nrm((DEPTH, D, D_FF), D ** -0.5)
    mlp_w2 = nrm((DEPTH, D_FF, D), D_FF ** -0.5)
    hy_w_in = nrm((N_EVEN, D, IN_W), D ** -0.5)
    hy_w_out = nrm((N_EVEN, MIX_W, D), MIX_W ** -0.5)
    hy_q_norm = 1.0 + nrm((N_EVEN, HEAD_DIM), 0.05)
    hy_k_norm = 1.0 + nrm((N_EVEN, HEAD_DIM), 0.05)
    hy_conv_w = nrm((N_EVEN, CONV_K, RNN_W), CONV_K ** -0.5)
    hy_conv_b = nrm((N_EVEN, RNN_W), 0.02)
    hy_gate_w = nrm((N_EVEN, 2, 2, RNN_BLOCKS, RNN_BW, RNN_BW), RNN_BW ** -0.5)
    hy_gate_b = nrm((N_EVEN, 2, 2, RNN_W), 0.1)
    s = uni((N_EVEN, 2, RNN_W), 0.9, 0.999) ** (1.0 / RG_C)
    hy_lam = jnp.log(s) - jnp.log1p(-s)
    rw_mu = uni((N_ODD, 6, D), 0.0, 1.0)
    rw_w_rkv = nrm((N_ODD, 3, D, D), D ** -0.5)
    rw_w_o = nrm((N_ODD, D, D), D ** -0.5)
    rw_lora_down = nrm((N_ODD, 2, 2, D, RWKV_LORA), D ** -0.5)
    rw_lora_up = nrm((N_ODD, 2, 2, RWKV_LORA, D), 0.1 * RWKV_LORA ** -0.5)
    w0 = uni((N_ODD, 2, D), -6.0, -0.5)
    a0 = nrm((N_ODD, 2, D), 0.1)
    rw_lora_bias = jnp.stack([w0, a0], axis=2)
    rw_gate_down = nrm((N_ODD, D, RWKV_GATE_LORA), D ** -0.5)
    rw_gate_up = nrm((N_ODD, RWKV_GATE_LORA, D), RWKV_GATE_LORA ** -0.5)
    rw_k_k = 0.85 + nrm((N_ODD, D), 0.05)
    rw_k_a = 1.0 + nrm((N_ODD, D), 0.05)
    rw_r_k = nrm((N_ODD, RWKV_HEADS, HEAD_DIM), 0.1)
    rw_gn_g = 1.0 + nrm((N_ODD, D), 0.05)
    rw_gn_b = nrm((N_ODD, D), 0.02)
    return {"x": x, "c": c, "ctx": ctx, "c_ctx": c_ctx,
            "ada_w": ada_w, "ada_b": ada_b, "norm_g": norm_g, "mlp_w1": mlp_w1, "mlp_w2": mlp_w2,
            "hy_w_in": hy_w_in, "hy_w_out": hy_w_out, "hy_q_norm": hy_q_norm, "hy_k_norm": hy_k_norm,
            "hy_conv_w": hy_conv_w, "hy_conv_b": hy_conv_b, "hy_gate_w": hy_gate_w, "hy_gate_b": hy_gate_b,
            "hy_lam": hy_lam,
            "rw_mu": rw_mu, "rw_w_rkv": rw_w_rkv, "rw_w_o": rw_w_o, "rw_lora_down": rw_lora_down,
            "rw_lora_up": rw_lora_up, "rw_lora_bias": rw_lora_bias, "rw_gate_down": rw_gate_down,
            "rw_gate_up": rw_gate_up, "rw_k_k": rw_k_k, "rw_k_a": rw_k_a, "rw_r_k": rw_r_k,
            "rw_gn_g": rw_gn_g, "rw_gn_b": rw_gn_b}


def reference(x, c, ctx, c_ctx, ada_w, ada_b, norm_g, mlp_w1, mlp_w2,
              hy_w_in, hy_w_out, hy_q_norm, hy_k_norm, hy_conv_w, hy_conv_b, hy_gate_w, hy_gate_b, hy_lam,
              rw_mu, rw_w_rkv, rw_w_o, rw_lora_down, rw_lora_up, rw_lora_bias, rw_gate_down, rw_gate_up,
              rw_k_k, rw_k_a, rw_r_k, rw_gn_g, rw_gn_b):
    cos, sin = axial_rope(x.shape[1])
    for l in range(DEPTH):
        i = l // 2
        need_ctx = l < DEPTH - 1
        ml = [m[:, None, :] for m in ada_chunks(c, ada_w[l], ada_b[l])]
        mc = ada_chunks(c_ctx, ada_w[l], ada_b[l])
        hl = rmsnorm(x, norm_g[l, 0]) * (1.0 + ml[1]) + ml[0]
        hc = rmsnorm(ctx, norm_g[l, 0]) * (1.0 + mc[1]) + mc[0]
        if l % 2 == 0:
            ol, oc = hybrid_mixer(hl, hc, hy_w_in[i], hy_w_out[i], hy_q_norm[i], hy_k_norm[i],
                                  hy_conv_w[i], hy_conv_b[i], hy_gate_w[i], hy_gate_b[i], hy_lam[i],
                                  cos, sin, need_ctx)
        else:
            ol, oc = rwkv7_mixer(hl, hc, rw_mu[i], rw_w_rkv[i], rw_w_o[i], rw_lora_down[i], rw_lora_up[i],
                                 rw_lora_bias[i], rw_gate_down[i], rw_gate_up[i], rw_k_k[i], rw_k_a[i],
                                 rw_r_k[i], rw_gn_g[i], rw_gn_b[i], need_ctx)
        x = x + ml[2] * ol
        x = x + ml[5] * sqrelu_mlp(rmsnorm(x, norm_g[l, 1]) * (1.0 + ml[4]) + ml[3], mlp_w1[l], mlp_w2[l])
        if need_ctx:
            ctx = ctx + mc[2] * oc
            ctx = ctx + mc[5] * sqrelu_mlp(rmsnorm(ctx, norm_g[l, 1]) * (1.0 + mc[4]) + mc[3], mlp_w1[l], mlp_w2[l])
    return x
```

```python
import functools
import math

import jax
import jax.numpy as jnp
from jax import lax
from jax.experimental import pallas as pl
from jax.experimental.pallas import tpu as pltpu

F32 = jnp.float32
BF16 = jnp.bfloat16

HEAD = 64
LANES = 128
SUBLANES = 8
MXU = 256
TM = 256
VMEM_LIMIT = 56 * 1024 * 1024

EPS = 1e-6
GN_EPS = 64e-5
RG_C = 8.0
ROPE_THETA = 10000.0
DECAY_SCALE = math.exp(-0.5)
CONV_K = 4
CONV_LEFT = 2
WKV_CHUNK = 64
WKV_LANES = 256
WKV_MODE = "x3"

NT_DIMS = (((1,), (1,)), ((), ()))
TN_DIMS = (((0,), (0,)), ((), ()))


def _params(sem):
    return pltpu.CompilerParams(dimension_semantics=sem, vmem_limit_bytes=VMEM_LIMIT)


def _const_spec(shape):
    nd = len(shape)
    return pl.BlockSpec(shape, lambda *_: (0,) * nd, pipeline_mode=pl.Buffered(1))


def _dot(a, b):
    return jnp.dot(a, b, preferred_element_type=F32)


def _dot_exact(a, b, dims=None):
    if dims is None:
        return jnp.dot(a, b, preferred_element_type=F32, precision=lax.Precision.HIGHEST)
    return lax.dot_general(a, b, dims, preferred_element_type=F32, precision=lax.Precision.HIGHEST)


def _split(x):
    hi = x.astype(BF16)
    lo = (x - hi.astype(F32)).astype(BF16)
    return hi, lo


def _group_reduce(x, ones_ref):
    cw = ones_ref.shape[0]
    ones = ones_ref[...]
    outs = []
    for c in range(x.shape[1] // cw):
        hi, lo = _split(x[:, c * cw:(c + 1) * cw])
        outs.append(_dot(hi, ones) + _dot(lo, ones))
    return outs[0] if len(outs) == 1 else jnp.concatenate(outs, axis=1)


def _norm_mod(x, g, shift, scale):
    ms = jnp.mean(x * x, axis=-1, keepdims=True)
    return (x * lax.rsqrt(ms + EPS) * g) * (1.0 + scale) + shift


def _ada_kernel(c_ref, w_ref, b_ref, o_ref):
    c = c_ref[...]
    s = c * jax.nn.sigmoid(c)
    o_ref[0, 0] = _dot_exact(s, w_ref[0]) + b_ref[0]


def _ada_mods(cc, ada_w, ada_b):
    depth, d, _ = ada_w.shape
    rows = cc.shape[0]
    out = pl.pallas_call(
        _ada_kernel,
        grid=(depth, 6),
        in_specs=[pl.BlockSpec((rows, d), lambda l, j: (0, 0)),
                  pl.BlockSpec((1, d, d), lambda l, j: (l, 0, j)),
                  pl.BlockSpec((1, 1, d), lambda l, j: (l * 6 + j, 0, 0))],
        out_specs=pl.BlockSpec((1, 1, rows, d), lambda l, j: (l, j, 0, 0)),
        out_shape=jax.ShapeDtypeStruct((depth, 6, rows, d), F32),
        compiler_params=_params(("arbitrary", "arbitrary")),
    )(cc, ada_w, ada_b.reshape(depth * 6, 1, d))
    return jnp.transpose(out, (0, 2, 1, 3))


def _hy_in_kernel(x_ref, mod_ref, g_ref, w_ref, cos_ref, sa_ref, sb_ref, qg_ref, kg_ref, oq_ref, ok_ref,
                  q_out, k_out, v_out, xr_out, gl_out, *, q_w, kv_w, rnn_w):
    x = x_ref[0]
    h = _norm_mod(x, g_ref[...], mod_ref[0, 0:1, :], mod_ref[0, 1:2, :]).astype(BF16)
    z = _dot(h, w_ref[...])
    c0, c1, c2, c3 = q_w, q_w + kv_w, q_w + 2 * kv_w, q_w + 2 * kv_w + rnn_w

    def norm_rope(u, gain, ones_ref):
        width = u.shape[1]
        un = u * lax.rsqrt(_group_reduce(u * u, ones_ref) + EPS) * gain
        reps = width // LANES
        tile = lambda r: jnp.concatenate([r[...]] * reps, axis=1) if reps > 1 else r[...]
        nxt = pltpu.roll(un, width - 1, 1)
        prv = pltpu.roll(un, 1, 1)
        return un * tile(cos_ref) + nxt * tile(sa_ref) + prv * tile(sb_ref)

    q_out[0] = norm_rope(z[:, :c0], qg_ref[...], oq_ref).astype(BF16)
    k_out[0] = norm_rope(z[:, c0:c1], kg_ref[...], ok_ref).astype(BF16)
    v_out[0] = z[:, c1:c2].astype(BF16)
    xr_out[0] = z[:, c2:c3]
    gl_out[0] = z[:, c3:]


def _attn_kernel(q_ref, k_ref, v_ref, o_ref, *, ct, nt, groups, n_kv):
    t = pl.program_id(1)
    nkv = jnp.where(t < ct, ct, nt)
    q = q_ref[0].astype(F32)
    lane = lax.broadcasted_iota(jnp.int32, (TM, LANES), 1)
    low = lane < HEAD
    heads = [None] * (groups * n_kv)
    for hk in range(n_kv):
        in_kv_lanes = low if hk % 2 == 0 else jnp.logical_not(low)
        q4 = []
        for j in range(groups):
            head = groups * hk + j
            qc = q[:, LANES * (head // 2):LANES * (head // 2 + 1)]
            if head % 2 != hk % 2:
                qc = pltpu.roll(qc, HEAD, 1)
            q4.append((jnp.where(in_kv_lanes, qc, 0.0) * HEAD ** -0.5).astype(BF16))
        qs = jnp.concatenate(q4, axis=0)
        kv_lo = LANES * (hk // 2)

        def body(i, carry):
            m, l, acc = carry
            r0 = pl.multiple_of(i * TM, TM)
            kc = k_ref[0, pl.ds(r0, TM), kv_lo:kv_lo + LANES]
            vc = v_ref[0, pl.ds(r0, TM), kv_lo:kv_lo + LANES]
            s = lax.dot_general(qs, kc, NT_DIMS, preferred_element_type=F32)
            m_new = jnp.maximum(m, jnp.max(s, axis=-1, keepdims=True))
            alpha = jnp.exp(m - m_new)
            p = jnp.exp(s - m_new)
            l = alpha * l + jnp.sum(p, axis=-1, keepdims=True)
            acc = alpha * acc + _dot(p.astype(BF16), vc)
            return m_new, l, acc

        rows = groups * TM
        init = (jnp.full((rows, 1), -1e30, F32), jnp.zeros((rows, 1), F32), jnp.zeros((rows, LANES), F32))
        _, l, acc = lax.fori_loop(0, nkv, body, init)
        o = acc / l
        for j in range(groups):
            heads[groups * hk + j] = (o[j * TM:(j + 1) * TM], hk % 2)
    for c in range(len(heads) // 2):
        (a, a_side), (b, b_side) = heads[2 * c], heads[2 * c + 1]
        if a_side == 1:
            a = pltpu.roll(a, HEAD, 1)
        if b_side == 0:
            b = pltpu.roll(b, HEAD, 1)
        o_ref[0, :, LANES * c:LANES * (c + 1)] = jnp.where(low, a, b).astype(BF16)


def _scan8(a, u, reverse):
    row = lax.broadcasted_iota(jnp.int32, a.shape, 0)
    for d in (1, 2, 4):
        if reverse:
            a_s, u_s, ok = pltpu.roll(a, SUBLANES - d, 0), pltpu.roll(u, SUBLANES - d, 0), row < SUBLANES - d
        else:
            a_s, u_s, ok = pltpu.roll(a, d, 0), pltpu.roll(u, d, 0), row >= d
        u = a * jnp.where(ok, u_s, 0.0) + u
        a = a * jnp.where(ok, a_s, 1.0)
    return a, u


def _rglru_kernel(xr_ref, gl_ref, cw_ref, cb_ref, gw_ref, gb_ref, lam_ref, o_ref,
                  xs_ref, a_ref, u_ref, *, cl, n):
    t_all = cl + n
    pad = SUBLANES
    xs_ref[0:pad, :] = jnp.zeros((pad, LANES), F32)
    xs_ref[pad + t_all:, :] = jnp.zeros((pad, LANES), F32)
    xs_ref[pad:pad + t_all, :] = xr_ref[0]
    lam = lam_ref[...]
    z = -lam
    softplus = jnp.maximum(z, 0.0) + jnp.log(1.0 + jnp.exp(-jnp.abs(z)))
    cw = cw_ref[...]
    cb = cb_ref[...]
    gb = gb_ref[...]

    def coeffs(i, _):
        r0 = pl.multiple_of(i * TM, TM)
        blk = xs_ref[pl.ds(r0, TM + 2 * pad), :]
        rows = r0 + lax.broadcasted_iota(jnp.int32, (TM, 1), 0)
        in_lat = rows >= cl
        pos = jnp.where(in_lat, rows - cl, rows)
        seqlen = jnp.where(in_lat, n, cl)
        xc = jnp.zeros((TM, LANES), F32) + cb
        for j in range(CONV_K):
            off = j - CONV_LEFT
            tap = blk[pad + off:pad + off + TM, :]
            ok = jnp.logical_and(pos + off >= 0, pos + off < seqlen)
            xc = xc + jnp.where(ok, tap, 0.0) * cw[j:j + 1, :]
        xcb = xc.astype(BF16)
        for d in range(2):
            r = jax.nn.sigmoid(_dot(xcb, gw_ref[0, 2 * d]) + gb[2 * d:2 * d + 1, :])
            ig = jax.nn.sigmoid(_dot(xcb, gw_ref[0, 2 * d + 1]) + gb[2 * d + 1:2 * d + 2, :])
            log_a = -RG_C * r * softplus[d:d + 1, :]
            a_ref[d, pl.ds(r0, TM), :] = jnp.exp(log_a)
            u_ref[d, pl.ds(r0, TM), :] = jnp.sqrt(1.0 - jnp.exp(2.0 * log_a)) * (ig * xc)
        return 0

    lax.fori_loop(0, t_all // TM, coeffs, 0)

    def step(d, reverse):
        def f(g, carry):
            r0 = pl.multiple_of(g * SUBLANES, SUBLANES)
            a, u = _scan8(a_ref[d, pl.ds(r0, SUBLANES), :], u_ref[d, pl.ds(r0, SUBLANES), :], reverse)
            h = a * carry + u
            u_ref[d, pl.ds(r0, SUBLANES), :] = h
            return h[0:1, :] if reverse else h[SUBLANES - 1:SUBLANES, :]
        return f

    zero = jnp.zeros((1, LANES), F32)
    g_all, g_ctx = t_all // SUBLANES, cl // SUBLANES
    lax.fori_loop(0, g_all, step(0, False), zero, unroll=4)
    f_rev = step(1, True)
    carry = lax.fori_loop(0, g_ctx, lambda i, c: f_rev(g_ctx - 1 - i, c), zero, unroll=4)
    lax.fori_loop(0, g_all - g_ctx, lambda i, c: f_rev(g_all - 1 - i, c), carry, unroll=4)

    def combine(i, _):
        r0 = pl.multiple_of(i * TM, TM)
        hsum = u_ref[0, pl.ds(r0, TM), :] + u_ref[1, pl.ds(r0, TM), :]
        o_ref[0, pl.ds(r0, TM), :] = (jax.nn.gelu(gl_ref[0, pl.ds(r0, TM), :]) * hsum).astype(BF16)
        return 0

    lax.fori_loop(0, t_all // TM, combine, 0)


def _rw_in_kernel(x_ref, xp_ref, xn_ref, mod_ref, g_ref, mu_ref, wrkv_ref, wdw_ref, wda_ref, wuw_ref, wua_ref,
                  lb_ref, gd_ref, gu_ref, kk_ref, ka_ref, rk_ref, gnb_ref, ones_ref,
                  r_out, kk_out, v_out, lw_out, bb_out, kd_out, g_out, z_out, *, ct, nt):
    t = pl.program_id(1)
    g = g_ref[...]
    shift, scale = mod_ref[0, 0:1, :], mod_ref[0, 1:2, :]
    h = _norm_mod(x_ref[0], g, shift, scale)
    first = jnp.logical_or(t == 0, t == ct)
    last = jnp.logical_or(t == ct - 1, t == nt - 1)
    hp = _norm_mod(xp_ref[0], g, shift, scale)[SUBLANES - 1:SUBLANES, :]
    hn = _norm_mod(xn_ref[0], g, shift, scale)[0:1, :]
    hp = jnp.where(first, 0.0, hp)
    hn = jnp.where(last, 0.0, hn)
    row = lax.broadcasted_iota(jnp.int32, (TM, 1), 0)
    h_prev = jnp.where(row == 0, hp, pltpu.roll(h, 1, 0))
    h_next = jnp.where(row == TM - 1, hn, pltpu.roll(h, TM - 1, 0))
    xx = 0.5 * (h_prev + h_next) - h
    lerp = lambda j: (h + xx * mu_ref[j:j + 1, :]).astype(BF16)

    r = _dot(lerp(0), wrkv_ref[0])
    k = _dot(lerp(2), wrkv_ref[1])
    v = _dot(lerp(3), wrkv_ref[2])
    gate = _dot(jax.nn.sigmoid(_dot(lerp(5), gd_ref[...])).astype(BF16), gu_ref[...])
    tw = jnp.tanh(_dot(lerp(1), wdw_ref[...])).astype(BF16)
    ta = _dot(lerp(4), wda_ref[...]).astype(BF16)

    kk = k * kk_ref[...]
    nrm = jnp.sqrt(_group_reduce(kk * kk, ones_ref))
    kk = kk / jnp.maximum(nrm, 1e-12)
    r_out[0] = r
    kk_out[0] = kk
    v_out[0] = v
    kd_sum = jnp.zeros_like(k)
    for d in range(2):
        dec = lb_ref[2 * d:2 * d + 1, :] + _dot(tw, wuw_ref[d])
        a = jax.nn.sigmoid(lb_ref[2 * d + 1:2 * d + 2, :] + _dot(ta, wua_ref[d]))
        kd = k * (1.0 + (a - 1.0) * ka_ref[...])
        lw_out[d, 0] = -DECAY_SCALE * jax.nn.sigmoid(dec)
        bb_out[d, 0] = kk * a
        kd_out[d, 0] = kd
        kd_sum = kd_sum + kd
    bonus = _group_reduce(r * kd_sum * rk_ref[...], ones_ref)
    g_out[0] = gate
    z_out[0] = (gnb_ref[...] + bonus * v) * gate


def _wkv_kernel(r_ref, kk_ref, v_ref, lw_ref, bb_ref, kd_ref, y_ref, h_ref, *, mm, mm_nt, mm_tn):
    c, lw_n = WKV_CHUNK, WKV_LANES
    d = pl.program_id(0)
    i = pl.program_id(3)

    @pl.when(i == 0)
    def _():
        h_ref[...] = jnp.zeros_like(h_ref)

    rev = d == 1
    row = lax.broadcasted_iota(jnp.int32, (c, c), 0)
    col = lax.broadcasted_iota(jnp.int32, (c, c), 1)
    ahead = jnp.where(rev, col - row, row - col)
    strict = ahead > 0
    incl = ahead >= 0
    tri = incl.astype(F32)
    eye = (row == col).astype(F32)
    lrow = lax.broadcasted_iota(jnp.int32, (lw_n, lw_n), 0)
    lcol = lax.broadcasted_iota(jnp.int32, (lw_n, lw_n), 1)
    same_head = (lrow // HEAD) == (lcol // HEAD)
    diag = lrow == lcol
    lane = lax.broadcasted_iota(jnp.int32, (1, lw_n), 1)
    n_chunks = TM // c

    def chunk(ci, _):
        cidx = jnp.where(rev, n_chunks - 1 - ci, ci)
        r0 = pl.multiple_of(cidx * c, c)
        sl = pl.ds(r0, c)
        r, kk, v = r_ref[0, sl, :], kk_ref[0, sl, :], v_ref[0, sl, :]
        lw, bb, kd = lw_ref[0, 0, sl, :], bb_ref[0, 0, sl, :], kd_ref[0, 0, sl, :]
        cum = _dot_exact(tri, lw)
        tot = jnp.sum(lw, axis=0, keepdims=True)
        e_neg = jnp.exp(-cum)
        e_end = jnp.exp(tot - cum)
        kk_hat = kk * jnp.exp(cum - lw)
        r_hat = r * jnp.exp(cum)
        kd_t, b_t = kd * e_neg, bb * e_neg
        kd_g, b_g = kd * e_end, bb * e_end

        kkp = jnp.zeros((c, lw_n), F32)
        u0 = jnp.zeros((c, lw_n), F32)
        y0 = jnp.zeros((c, lw_n), F32)
        rp = r_hat
        for hh in range(lw_n // HEAD):
            mh = jnp.logical_and(lane >= hh * HEAD, lane < (hh + 1) * HEAD)
            kk_h = jnp.where(mh, kk_hat, 0.0)
            lhs = jnp.concatenate([kk_h, jnp.where(mh, r_hat, 0.0)], axis=0)
            p1 = mm_nt(lhs, kd_t)
            p2 = mm_nt(lhs, b_t)
            a_kd = jnp.where(strict, p1[:c], 0.0)
            b_kd = jnp.where(incl, p1[c:], 0.0)
            nmat = jnp.where(strict, p2[:c], 0.0)
            b_b = jnp.where(incl, p2[c:], 0.0)
            tinv = eye - nmat
            pw = nmat
            for _ in range(int(math.log2(c)) - 1):
                pw = mm(pw, pw)
                tinv = tinv + mm(tinv, pw)
            av = jnp.where(mh, mm(a_kd, v), 0.0)
            kkp_h = mm(tinv, kk_h)
            u0_h = mm(tinv, av)
            rp = rp - mm(b_b, kkp_h)
            y0 = y0 + jnp.where(mh, mm(b_kd, v), 0.0) - mm(b_b, u0_h)
            kkp = kkp + kkp_h
            u0 = u0 + u0_h

        hs = h_ref[...]
        y_ref[0, 0, sl, :] = mm(rp, hs) + y0
        m_mat = jnp.where(diag, jnp.exp(tot), 0.0) - jnp.where(same_head, mm_tn(b_g, kkp), 0.0)
        g_mat = jnp.where(same_head,
                          mm_tn(jnp.concatenate([kd_g, b_g], axis=0), jnp.concatenate([v, -u0], axis=0)), 0.0)
        h_ref[...] = mm(m_mat, hs) + g_mat
        return 0

    lax.fori_loop(0, n_chunks, chunk, 0)


def _mm_modes(mode):
    if mode == "f32":
        return (lambda a, b: _dot_exact(a, b),
                lambda a, b: _dot_exact(a, b, NT_DIMS),
                lambda a, b: _dot_exact(a, b, TN_DIMS))
    if mode == "bf16":
        dg = lambda dims: (lambda a, b: lax.dot_general(a.astype(BF16), b.astype(BF16), dims,
                                                        preferred_element_type=F32))
        return dg((((1,), (0,)), ((), ()))), dg(NT_DIMS), dg(TN_DIMS)

    def x3(dims):
        def f(a, b):
            ah, al = _split(a)
            bh, bl = _split(b)
            g = lambda x, y: lax.dot_general(x, y, dims, preferred_element_type=F32)
            return g(ah, bh) + (g(ah, bl) + g(al, bh))
        return f
    return x3((((1,), (0,)), ((), ()))), x3(NT_DIMS), x3(TN_DIMS)


def _post_kernel(*refs, odd, ff_chunk):
    if odd:
        (x_ref, y0_ref, y1_ref, gate_ref, z_ref, gng_ref, ones_ref,
         wo_ref, mod_ref, g2_ref, w1_ref, w2_ref, o_ref) = refs
        y = y0_ref[0, 0] + y1_ref[0, 0]
        dlt = y - _group_reduce(y, ones_ref)
        var = _group_reduce(dlt * dlt, ones_ref)
        mix = (dlt * lax.rsqrt(var + GN_EPS) * gng_ref[...] * gate_ref[0] + z_ref[0]).astype(BF16)
    else:
        x_ref, att_ref, rec_ref, wo_ref, mod_ref, g2_ref, w1_ref, w2_ref, o_ref = refs
        mix = jnp.concatenate([att_ref[0], rec_ref[0]], axis=1)
    x1 = x_ref[0] + mod_ref[0, 2:3, :] * _dot(mix, wo_ref[...])
    h2 = _norm_mod(x1, g2_ref[...], mod_ref[0, 3:4, :], mod_ref[0, 4:5, :]).astype(BF16)
    acc = jnp.zeros_like(x1)
    for c in range(w1_ref.shape[1] // ff_chunk):
        a = _dot(h2, w1_ref[:, c * ff_chunk:(c + 1) * ff_chunk])
        a = jnp.square(jnp.maximum(a, 0.0)).astype(BF16)
        acc = acc + _dot(a, w2_ref[c * ff_chunk:(c + 1) * ff_chunk, :])
    o_ref[0] = x1 + mod_ref[0, 5:6, :] * acc


def _blockdiag_ones(width, value):
    idx = jnp.arange(width) // HEAD
    return jnp.where(idx[:, None] == idx[None, :], value, 0.0).astype(BF16)


def _rope_tables(cl, n, grid_w):
    rows = n // grid_w
    row = jnp.repeat(jnp.arange(rows, dtype=F32), grid_w)
    col = jnp.tile(jnp.arange(grid_w, dtype=F32), rows)
    half = HEAD // 2
    inv = ROPE_THETA ** (-jnp.arange(0, half, 2, dtype=F32) / half)
    ang = jnp.concatenate([row[:, None] * inv, col[:, None] * inv], axis=-1)
    cos = jnp.repeat(jnp.cos(ang), 2, axis=-1)
    sin = jnp.repeat(jnp.sin(ang), 2, axis=-1)
    even = (jnp.arange(HEAD) % 2 == 0)[None, :]
    sa = jnp.where(even, -sin, 0.0)
    sb = jnp.where(even, 0.0, sin)
    ctx = lambda fill: jnp.full((cl, HEAD), fill, F32)
    full = lambda lat, fill: jnp.tile(jnp.concatenate([ctx(fill), lat], axis=0), (1, LANES // HEAD))
    return full(cos, 1.0), full(sa, 0.0), full(sb, 0.0)


def _tile_specs(b_all, ct, d_model):
    mod_spec = pl.BlockSpec((1, 6, d_model), lambda b, t: (jnp.where(t < ct, b_all, b), 0, 0))
    row_spec = lambda w: pl.BlockSpec((1, TM, w), lambda b, t: (b, t, 0))
    return mod_spec, row_spec


def _hybrid_layer(xs, mods, g1, w_in, w_out, qn, kn, conv_w, conv_b, gate_w, gate_b, lam, ropes, dims):
    b_all, cl, n, d_model = dims
    t_all = cl + n
    ct, nt = cl // TM, t_all // TM
    in_w = w_in.shape[1]
    rnn_w = conv_w.shape[1]
    kv_w = (in_w - 2 * rnn_w - d_model // 2) // 2
    q_w = in_w - 2 * kv_w - 2 * rnn_w
    n_kv = kv_w // HEAD
    groups = q_w // kv_w
    mod_spec, row_spec = _tile_specs(b_all, ct, d_model)
    rope_spec = pl.BlockSpec((TM, LANES), lambda b, t: (t, 0))
    cos, sa, sb = ropes

    q, k, v, xr, gl = pl.pallas_call(
        functools.partial(_hy_in_kernel, q_w=q_w, kv_w=kv_w, rnn_w=rnn_w),
        grid=(b_all, nt),
        in_specs=[row_spec(d_model), mod_spec, _const_spec((1, d_model)), _const_spec((d_model, in_w)),
                  rope_spec, rope_spec, rope_spec, _const_spec((1, q_w)), _const_spec((1, kv_w)),
                  _const_spec((MXU, MXU)), _const_spec((kv_w, kv_w))],
        out_specs=[row_spec(q_w), row_spec(kv_w), row_spec(kv_w), row_spec(rnn_w), row_spec(rnn_w)],
        out_shape=[jax.ShapeDtypeStruct((b_all, t_all, q_w), BF16),
                   jax.ShapeDtypeStruct((b_all, t_all, kv_w), BF16),
                   jax.ShapeDtypeStruct((b_all, t_all, kv_w), BF16),
                   jax.ShapeDtypeStruct((b_all, t_all, rnn_w), F32),
                   jax.ShapeDtypeStruct((b_all, t_all, rnn_w), F32)],
        compiler_params=_params(("parallel", "parallel")),
    )(xs, mods, g1.reshape(1, d_model), w_in.astype(BF16), cos, sa, sb,
      jnp.tile(qn, q_w // HEAD).reshape(1, q_w), jnp.tile(kn, kv_w // HEAD).reshape(1, kv_w),
      _blockdiag_ones(MXU, 1.0 / HEAD), _blockdiag_ones(kv_w, 1.0 / HEAD))

    att = pl.pallas_call(
        functools.partial(_attn_kernel, ct=ct, nt=nt, groups=groups, n_kv=n_kv),
        grid=(b_all, nt),
        in_specs=[row_spec(q_w),
                  pl.BlockSpec((1, t_all, kv_w), lambda b, t: (b, 0, 0)),
                  pl.BlockSpec((1, t_all, kv_w), lambda b, t: (b, 0, 0))],
        out_specs=row_spec(q_w),
        out_shape=jax.ShapeDtypeStruct((b_all, t_all, q_w), BF16),
        compiler_params=_params(("parallel", "parallel")),
    )(q, k, v)

    n_lc = rnn_w // LANES
    per = LANES // HEAD
    gw = gate_w.reshape(4, n_lc, per, HEAD, HEAD)
    eye = jnp.eye(per, dtype=F32)
    gw = jnp.einsum('gcpde,pq->cgpdqe', gw, eye).reshape(n_lc, 4, LANES, LANES).astype(BF16)
    lane_spec = lambda rows: pl.BlockSpec((rows, LANES), lambda b, c: (0, c))
    seq_spec = pl.BlockSpec((1, t_all, LANES), lambda b, c: (b, 0, c))
    rec = pl.pallas_call(
        functools.partial(_rglru_kernel, cl=cl, n=n),
        grid=(b_all, n_lc),
        in_specs=[seq_spec, seq_spec, lane_spec(CONV_K), lane_spec(1),
                  pl.BlockSpec((1, 4, LANES, LANES), lambda b, c: (c, 0, 0, 0)),
                  lane_spec(4), lane_spec(2)],
        out_specs=seq_spec,
        out_shape=jax.ShapeDtypeStruct((b_all, t_all, rnn_w), BF16),
        scratch_shapes=[pltpu.VMEM((t_all + 2 * SUBLANES, LANES), F32),
                        pltpu.VMEM((2, t_all, LANES), F32),
                        pltpu.VMEM((2, t_all, LANES), F32)],
        compiler_params=_params(("parallel", "parallel")),
    )(xr, gl, conv_w, conv_b.reshape(1, rnn_w), gw, gate_b.reshape(4, rnn_w), lam)
    return (att, rec), w_out.astype(BF16)


def _rwkv_layer(xs, mods, g1, mu, w_rkv, lora_down, lora_up, lora_bias, gate_down, gate_up,
                k_k, k_a, r_k, gn_b, dims, wkv_mode):
    b_all, cl, n, d_model = dims
    t_all = cl + n
    ct, nt = cl // TM, t_all // TM
    n8 = t_all // SUBLANES
    per8 = TM // SUBLANES
    lora = lora_down.shape[-1]
    glora = gate_down.shape[-1]
    mod_spec, row_spec = _tile_specs(b_all, ct, d_model)
    dir_spec = pl.BlockSpec((2, 1, TM, d_model), lambda b, t: (0, b, t, 0))
    wdw = jnp.concatenate([lora_down[0, 0], lora_down[1, 0]], axis=1).astype(BF16)
    wda = jnp.concatenate([lora_down[0, 1], lora_down[1, 1]], axis=1).astype(BF16)
    zeros = jnp.zeros((lora, d_model), F32)
    pad_up = lambda j: jnp.stack([jnp.concatenate([lora_up[0, j], zeros], axis=0),
                                  jnp.concatenate([zeros, lora_up[1, j]], axis=0)]).astype(BF16)
    vec = lambda a: a.reshape(1, d_model)
    outs = pl.pallas_call(
        functools.partial(_rw_in_kernel, ct=ct, nt=nt),
        grid=(b_all, nt),
        in_specs=[row_spec(d_model),
                  pl.BlockSpec((1, SUBLANES, d_model), lambda b, t: (b, jnp.maximum(t * per8 - 1, 0), 0)),
                  pl.BlockSpec((1, SUBLANES, d_model), lambda b, t: (b, jnp.minimum((t + 1) * per8, n8 - 1), 0)),
                  mod_spec, _const_spec((1, d_model)), _const_spec((6, d_model)),
                  _const_spec((3, d_model, d_model)), _const_spec((d_model, 2 * lora)),
                  _const_spec((d_model, 2 * lora)), _const_spec((2, 2 * lora, d_model)),
                  _const_spec((2, 2 * lora, d_model)), _const_spec((4, d_model)),
                  _const_spec((d_model, glora)), _const_spec((glora, d_model)),
                  _const_spec((1, d_model)), _const_spec((1, d_model)), _const_spec((1, d_model)),
                  _const_spec((1, d_model)), _const_spec((MXU, MXU))],
        out_specs=[row_spec(d_model), row_spec(d_model), row_spec(d_model),
                   dir_spec, dir_spec, dir_spec, row_spec(d_model), row_spec(d_model)],
        out_shape=[jax.ShapeDtypeStruct((b_all, t_all, d_model), F32)] * 3
        + [jax.ShapeDtypeStruct((2, b_all, t_all, d_model), F32)] * 3
        + [jax.ShapeDtypeStruct((b_all, t_all, d_model), F32)] * 2,
        compiler_params=_params(("parallel", "parallel")),
    )(xs, xs, xs, mods, vec(g1), mu, w_rkv.astype(BF16), wdw, wda, pad_up(0), pad_up(1),
      lora_bias.reshape(4, d_model), gate_down.astype(BF16), gate_up.astype(BF16),
      vec(k_k), vec(k_a), vec(r_k), vec(gn_b), _blockdiag_ones(MXU, 1.0))
    r, kk, v, lw, bb, kd, gate, z = outs

    def tmap(dd, i):
        rev = jnp.where(i < ct, ct - 1 - i, nt - 1 - (i - ct))
        return jnp.where(dd == 0, i, rev)

    shared = pl.BlockSpec((1, TM, WKV_LANES), lambda dd, b, hh, i: (b, tmap(dd, i), hh))
    per_dir = pl.BlockSpec((1, 1, TM, WKV_LANES), lambda dd, b, hh, i: (dd, b, tmap(dd, i), hh))
    mm, mm_nt, mm_tn = _mm_modes(wkv_mode)
    y = pl.pallas_call(
        functools.partial(_wkv_kernel, mm=mm, mm_nt=mm_nt, mm_tn=mm_tn),
        grid=(2, b_all, d_model // WKV_LANES, nt),
        in_specs=[shared, shared, shared, per_dir, per_dir, per_dir],
        out_specs=per_dir,
        out_shape=jax.ShapeDtypeStruct((2, b_all, t_all, d_model), F32),
        scratch_shapes=[pltpu.VMEM((WKV_LANES, WKV_LANES), F32)],
        compiler_params=_params(("arbitrary", "arbitrary", "arbitrary", "arbitrary")),
    )(r, kk, v, lw, bb, kd)
    return y, gate, z


def _post(xs, mix_inputs, w_o, mods, g2, w1, w2, dims, odd, gn_g=None, latent_only=False):
    b_all, cl, n, d_model = dims
    t_all = cl + n
    ct, nt = cl // TM, t_all // TM
    t0 = ct if latent_only else 0
    d_ff = w1.shape[1]
    mod_spec = pl.BlockSpec((1, 6, d_model), lambda b, t: (jnp.where(t + t0 < ct, b_all, b), 0, 0))
    row_spec = lambda w: pl.BlockSpec((1, TM, w), lambda b, t: (b, t + t0, 0))
    tail_specs = [_const_spec((d_model, d_model)), mod_spec, _const_spec((1, d_model)),
                  _const_spec((d_model, d_ff)), _const_spec((d_ff, d_model))]
    tail_args = (w_o, mods, g2.reshape(1, d_model), w1.astype(BF16), w2.astype(BF16))
    if odd:
        y, gate, z = mix_inputs
        y_spec = lambda dd: pl.BlockSpec((1, 1, TM, d_model), lambda b, t: (dd, b, t + t0, 0))
        in_specs = [row_spec(d_model), y_spec(0), y_spec(1), row_spec(d_model), row_spec(d_model),
                    _const_spec((1, d_model)), _const_spec((MXU, MXU))] + tail_specs
        args = (xs, y, y, gate, z, gn_g.reshape(1, d_model), _blockdiag_ones(MXU, 1.0 / HEAD)) + tail_args
    else:
        att, rec = mix_inputs
        in_specs = [row_spec(d_model), row_spec(att.shape[-1]), row_spec(rec.shape[-1])] + tail_specs
        args = (xs, att, rec) + tail_args
    rows_out = n if latent_only else t_all
    return pl.pallas_call(
        functools.partial(_post_kernel, odd=odd, ff_chunk=min(d_ff, 4 * MXU)),
        grid=(b_all, nt - t0),
        in_specs=in_specs,
        out_specs=pl.BlockSpec((1, TM, d_model), lambda b, t: (b, t, 0)),
        out_shape=jax.ShapeDtypeStruct((b_all, rows_out, d_model), F32),
        compiler_params=_params(("parallel", "parallel")),
    )(*args)


def kernel(x, c, ctx, c_ctx, ada_w, ada_b, norm_g, mlp_w1, mlp_w2, hy_w_in, hy_w_out, hy_q_norm, hy_k_norm, hy_conv_w, hy_conv_b, hy_gate_w, hy_gate_b, hy_lam, rw_mu, rw_w_rkv, rw_w_o, rw_lora_down, rw_lora_up, rw_lora_bias, rw_gate_down, rw_gate_up, rw_k_k, rw_k_a, rw_r_k, rw_gn_g, rw_gn_b):
    b_all, n, d_model = x.shape
    cl = ctx.shape[1]
    depth = ada_w.shape[0]
    assert cl % TM == 0 and n % TM == 0 and d_model % MXU == 0
    grid_w = 64
    dims = (b_all, cl, n, d_model)
    mods_all = _ada_mods(jnp.concatenate([c, c_ctx[None, :]], axis=0), ada_w, ada_b)
    ropes = _rope_tables(cl, n, grid_w)
    xs = jnp.concatenate([ctx, x], axis=1)
    for l in range(depth):
        i = l // 2
        mods = mods_all[l]
        last = l == depth - 1
        if l % 2 == 0:
            mix, w_o = _hybrid_layer(xs, mods, norm_g[l, 0], hy_w_in[i], hy_w_out[i], hy_q_norm[i], hy_k_norm[i],
                                     hy_conv_w[i], hy_conv_b[i], hy_gate_w[i], hy_gate_b[i], hy_lam[i], ropes, dims)
            xs = _post(xs, mix, w_o, mods, norm_g[l, 1], mlp_w1[l], mlp_w2[l], dims, odd=False, latent_only=last)
        else:
            mix = _rwkv_layer(xs, mods, norm_g[l, 0], rw_mu[i], rw_w_rkv[i], rw_lora_down[i], rw_lora_up[i],
                              rw_lora_bias[i], rw_gate_down[i], rw_gate_up[i], rw_k_k[i], rw_k_a[i],
                              rw_r_k[i], rw_gn_b[i], dims, WKV_MODE)
            xs = _post(xs, mix, rw_w_o[i].astype(BF16), mods, norm_g[l, 1], mlp_w1[l], mlp_w2[l], dims,
                       odd=True, gn_g=rw_gn_g[i], latent_only=last)
    return xs if xs.shape[1] == n else xs[:, cl:]
```

```python
import functools
import math

import jax
import jax.numpy as jnp
from jax import lax
from jax.experimental import pallas as pl
from jax.experimental.pallas import tpu as pltpu

F32 = jnp.float32
BF16 = jnp.bfloat16

HEAD = 64
LANES = 128
SUBLANES = 8
MXU = 256
TM = 256
VMEM_LIMIT = 56 * 1024 * 1024

EPS = 1e-6
GN_EPS = 64e-5
RG_C = 8.0
ROPE_THETA = 10000.0
DECAY_SCALE = math.exp(-0.5)
CONV_K = 4
CONV_LEFT = 2
WKV_CHUNK = 64
WKV_LANES = 256
WKV_MODE = {"A": "x1", "T": "x1", "V": "x1", "S": "x1"}

NT_DIMS = (((1,), (1,)), ((), ()))
TN_DIMS = (((0,), (0,)), ((), ()))


def _params(sem):
    return pltpu.CompilerParams(dimension_semantics=sem, vmem_limit_bytes=VMEM_LIMIT)


def _const_spec(shape):
    nd = len(shape)
    return pl.BlockSpec(shape, lambda *_: (0,) * nd, pipeline_mode=pl.Buffered(1))


def _dot(a, b):
    return jnp.dot(a, b, preferred_element_type=F32)


def _dot_exact(a, b, dims=None):
    if dims is None:
        return jnp.dot(a, b, preferred_element_type=F32, precision=lax.Precision.HIGHEST)
    return lax.dot_general(a, b, dims, preferred_element_type=F32, precision=lax.Precision.HIGHEST)


def _split(x):
    hi = x.astype(BF16)
    lo = (x - hi.astype(F32)).astype(BF16)
    return hi, lo


def _group_reduce(x, ones_ref):
    cw = ones_ref.shape[0]
    ones = ones_ref[...]
    outs = []
    for c in range(x.shape[1] // cw):
        hi, lo = _split(x[:, c * cw:(c + 1) * cw])
        outs.append(_dot(hi, ones) + _dot(lo, ones))
    return outs[0] if len(outs) == 1 else jnp.concatenate(outs, axis=1)


def _norm_mod(x, g, shift, scale):
    ms = jnp.mean(x * x, axis=-1, keepdims=True)
    return (x * lax.rsqrt(ms + EPS) * g) * (1.0 + scale) + shift


def _ada_kernel(c_ref, w_ref, b_ref, o_ref):
    c = c_ref[...]
    s = c * jax.nn.sigmoid(c)
    o_ref[0, 0] = _dot_exact(s, w_ref[0]) + b_ref[0]


def _ada_mods(cc, ada_w, ada_b):
    depth, d, _ = ada_w.shape
    rows = cc.shape[0]
    out = pl.pallas_call(
        _ada_kernel,
        grid=(depth, 6),
        in_specs=[pl.BlockSpec((rows, d), lambda l, j: (0, 0)),
                  pl.BlockSpec((1, d, d), lambda l, j: (l, 0, j)),
                  pl.BlockSpec((1, 1, d), lambda l, j: (l * 6 + j, 0, 0))],
        out_specs=pl.BlockSpec((1, 1, rows, d), lambda l, j: (l, j, 0, 0)),
        out_shape=jax.ShapeDtypeStruct((depth, 6, rows, d), F32),
        compiler_params=_params(("arbitrary", "arbitrary")),
    )(cc, ada_w, ada_b.reshape(depth * 6, 1, d))
    return jnp.transpose(out, (0, 2, 1, 3))


def _hy_in_kernel(x_ref, mod_ref, g_ref, w_ref, cos_ref, sa_ref, sb_ref, qg_ref, kg_ref, oq_ref, ok_ref,
                  q_out, k_out, v_out, xr_out, gl_out, *, q_w, kv_w, rnn_w):
    x = x_ref[0]
    h = _norm_mod(x, g_ref[...], mod_ref[0, 0:1, :], mod_ref[0, 1:2, :]).astype(BF16)
    z = _dot(h, w_ref[...])
    c0, c1, c2, c3 = q_w, q_w + kv_w, q_w + 2 * kv_w, q_w + 2 * kv_w + rnn_w

    def norm_rope(u, gain, ones_ref):
        width = u.shape[1]
        un = u * lax.rsqrt(_group_reduce(u * u, ones_ref) + EPS) * gain
        reps = width // LANES
        tile = lambda r: jnp.concatenate([r[...]] * reps, axis=1) if reps > 1 else r[...]
        nxt = pltpu.roll(un, width - 1, 1)
        prv = pltpu.roll(un, 1, 1)
        return un * tile(cos_ref) + nxt * tile(sa_ref) + prv * tile(sb_ref)

    q_out[0] = norm_rope(z[:, :c0], qg_ref[...], oq_ref).astype(BF16)
    k_out[0] = norm_rope(z[:, c0:c1], kg_ref[...], ok_ref).astype(BF16)
    v_out[0] = z[:, c1:c2].astype(BF16)
    xr_out[0] = z[:, c2:c3]
    gl_out[0] = z[:, c3:]


def _attn_kernel(q_ref, k_ref, v_ref, o_ref, *, ct, nt, groups, n_kv):
    t = pl.program_id(1)
    nkv = jnp.where(t < ct, ct, nt)
    q = q_ref[0].astype(F32)
    lane = lax.broadcasted_iota(jnp.int32, (TM, LANES), 1)
    low = lane < HEAD
    heads = [None] * (groups * n_kv)
    for hk in range(n_kv):
        in_kv_lanes = low if hk % 2 == 0 else jnp.logical_not(low)
        q4 = []
        for j in range(groups):
            head = groups * hk + j
            qc = q[:, LANES * (head // 2):LANES * (head // 2 + 1)]
            if head % 2 != hk % 2:
                qc = pltpu.roll(qc, HEAD, 1)
            q4.append((jnp.where(in_kv_lanes, qc, 0.0) * HEAD ** -0.5).astype(BF16))
        qs = jnp.concatenate(q4, axis=0)
        kv_lo = LANES * (hk // 2)

        def body(i, carry):
            m, l, acc = carry
            r0 = pl.multiple_of(i * TM, TM)
            kc = k_ref[0, pl.ds(r0, TM), kv_lo:kv_lo + LANES]
            vc = v_ref[0, pl.ds(r0, TM), kv_lo:kv_lo + LANES]
            s = lax.dot_general(qs, kc, NT_DIMS, preferred_element_type=F32)
            m_new = jnp.maximum(m, jnp.max(s, axis=-1, keepdims=True))
            alpha = jnp.exp(m - m_new)
            p = jnp.exp(s - m_new)
            l = alpha * l + jnp.sum(p, axis=-1, keepdims=True)
            acc = alpha * acc + _dot(p.astype(BF16), vc)
            return m_new, l, acc

        rows = groups * TM
        init = (jnp.full((rows, 1), -1e30, F32), jnp.zeros((rows, 1), F32), jnp.zeros((rows, LANES), F32))
        _, l, acc = lax.fori_loop(0, nkv, body, init)
        o = acc / l
        for j in range(groups):
            heads[groups * hk + j] = (o[j * TM:(j + 1) * TM], hk % 2)
    for c in range(len(heads) // 2):
        (a, a_side), (b, b_side) = heads[2 * c], heads[2 * c + 1]
        if a_side == 1:
            a = pltpu.roll(a, HEAD, 1)
        if b_side == 0:
            b = pltpu.roll(b, HEAD, 1)
        o_ref[0, :, LANES * c:LANES * (c + 1)] = jnp.where(low, a, b).astype(BF16)


def _scan8(a, u, reverse):
    row = lax.broadcasted_iota(jnp.int32, a.shape, 0)
    for d in (1, 2, 4):
        if reverse:
            a_s, u_s, ok = pltpu.roll(a, SUBLANES - d, 0), pltpu.roll(u, SUBLANES - d, 0), row < SUBLANES - d
        else:
            a_s, u_s, ok = pltpu.roll(a, d, 0), pltpu.roll(u, d, 0), row >= d
        u = a * jnp.where(ok, u_s, 0.0) + u
        a = a * jnp.where(ok, a_s, 1.0)
    return a, u


def _rglru_kernel(xr_ref, gl_ref, cw_ref, cb_ref, gw_ref, gb_ref, lam_ref, o_ref,
                  xs_ref, a_ref, u_ref, *, cl, n):
    t_all = cl + n
    pad = SUBLANES
    xs_ref[0:pad, :] = jnp.zeros((pad, LANES), F32)
    xs_ref[pad + t_all:, :] = jnp.zeros((pad, LANES), F32)
    xs_ref[pad:pad + t_all, :] = xr_ref[0]
    lam = lam_ref[...]
    z = -lam
    softplus = jnp.maximum(z, 0.0) + jnp.log(1.0 + jnp.exp(-jnp.abs(z)))
    cw = cw_ref[...]
    cb = cb_ref[...]
    gb = gb_ref[...]

    def coeffs(i, _):
        r0 = pl.multiple_of(i * TM, TM)
        blk = xs_ref[pl.ds(r0, TM + 2 * pad), :]
        rows = r0 + lax.broadcasted_iota(jnp.int32, (TM, 1), 0)
        in_lat = rows >= cl
        pos = jnp.where(in_lat, rows - cl, rows)
        seqlen = jnp.where(in_lat, n, cl)
        xc = jnp.zeros((TM, LANES), F32) + cb
        for j in range(CONV_K):
            off = j - CONV_LEFT
            tap = blk[pad + off:pad + off + TM, :]
            ok = jnp.logical_and(pos + off >= 0, pos + off < seqlen)
            xc = xc + jnp.where(ok, tap, 0.0) * cw[j:j + 1, :]
        xcb = xc.astype(BF16)
        for d in range(2):
            r = jax.nn.sigmoid(_dot(xcb, gw_ref[0, 2 * d]) + gb[2 * d:2 * d + 1, :])
            ig = jax.nn.sigmoid(_dot(xcb, gw_ref[0, 2 * d + 1]) + gb[2 * d + 1:2 * d + 2, :])
            log_a = -RG_C * r * softplus[d:d + 1, :]
            a_ref[d, pl.ds(r0, TM), :] = jnp.exp(log_a)
            u_ref[d, pl.ds(r0, TM), :] = jnp.sqrt(1.0 - jnp.exp(2.0 * log_a)) * (ig * xc)
        return 0

    lax.fori_loop(0, t_all // TM, coeffs, 0)

    def step(d, reverse):
        def f(g, carry):
            r0 = pl.multiple_of(g * SUBLANES, SUBLANES)
            a, u = _scan8(a_ref[d, pl.ds(r0, SUBLANES), :], u_ref[d, pl.ds(r0, SUBLANES), :], reverse)
            h = a * carry + u
            u_ref[d, pl.ds(r0, SUBLANES), :] = h
            return h[0:1, :] if reverse else h[SUBLANES - 1:SUBLANES, :]
        return f

    zero = jnp.zeros((1, LANES), F32)
    g_all, g_ctx = t_all // SUBLANES, cl // SUBLANES
    lax.fori_loop(0, g_all, step(0, False), zero, unroll=4)
    f_rev = step(1, True)
    carry = lax.fori_loop(0, g_ctx, lambda i, c: f_rev(g_ctx - 1 - i, c), zero, unroll=4)
    lax.fori_loop(0, g_all - g_ctx, lambda i, c: f_rev(g_all - 1 - i, c), carry, unroll=4)

    def combine(i, _):
        r0 = pl.multiple_of(i * TM, TM)
        hsum = u_ref[0, pl.ds(r0, TM), :] + u_ref[1, pl.ds(r0, TM), :]
        o_ref[0, pl.ds(r0, TM), :] = (jax.nn.gelu(gl_ref[0, pl.ds(r0, TM), :]) * hsum).astype(BF16)
        return 0

    lax.fori_loop(0, t_all // TM, combine, 0)


def _rw_in_kernel(x_ref, xp_ref, xn_ref, mod_ref, g_ref, mu_ref, wrkv_ref, wdw_ref, wda_ref, wuw_ref, wua_ref,
                  lb_ref, gd_ref, gu_ref, kk_ref, ka_ref, rk_ref, gnb_ref, ones_ref,
                  r_out, kk_out, v_out, lw_out, bb_out, kd_out, g_out, z_out, *, ct, nt):
    t = pl.program_id(1)
    g = g_ref[...]
    shift, scale = mod_ref[0, 0:1, :], mod_ref[0, 1:2, :]
    h = _norm_mod(x_ref[0], g, shift, scale)
    first = jnp.logical_or(t == 0, t == ct)
    last = jnp.logical_or(t == ct - 1, t == nt - 1)
    hp = _norm_mod(xp_ref[0], g, shift, scale)[SUBLANES - 1:SUBLANES, :]
    hn = _norm_mod(xn_ref[0], g, shift, scale)[0:1, :]
    hp = jnp.where(first, 0.0, hp)
    hn = jnp.where(last, 0.0, hn)
    row = lax.broadcasted_iota(jnp.int32, (TM, 1), 0)
    h_prev = jnp.where(row == 0, hp, pltpu.roll(h, 1, 0))
    h_next = jnp.where(row == TM - 1, hn, pltpu.roll(h, TM - 1, 0))
    xx = 0.5 * (h_prev + h_next) - h
    lerp = lambda j: (h + xx * mu_ref[j:j + 1, :]).astype(BF16)

    r = _dot(lerp(0), wrkv_ref[0])
    k = _dot(lerp(2), wrkv_ref[1])
    v = _dot(lerp(3), wrkv_ref[2])
    gate = _dot(jax.nn.sigmoid(_dot(lerp(5), gd_ref[...])).astype(BF16), gu_ref[...])
    tw = jnp.tanh(_dot(lerp(1), wdw_ref[...])).astype(BF16)
    ta = _dot(lerp(4), wda_ref[...]).astype(BF16)

    kk = k * kk_ref[...]
    nrm = jnp.sqrt(_group_reduce(kk * kk, ones_ref))
    kk = kk / jnp.maximum(nrm, 1e-12)
    r_out[0] = r
    kk_out[0] = kk
    v_out[0] = v
    kd_sum = jnp.zeros_like(k)
    for d in range(2):
        dec = lb_ref[2 * d:2 * d + 1, :] + _dot(tw, wuw_ref[d])
        a = jax.nn.sigmoid(lb_ref[2 * d + 1:2 * d + 2, :] + _dot(ta, wua_ref[d]))
        kd = k * (1.0 + (a - 1.0) * ka_ref[...])
        lw_out[d, 0] = -DECAY_SCALE * jax.nn.sigmoid(dec)
        bb_out[d, 0] = kk * a
        kd_out[d, 0] = kd
        kd_sum = kd_sum + kd
    bonus = _group_reduce(r * kd_sum * rk_ref[...], ones_ref)
    g_out[0] = gate
    z_out[0] = (gnb_ref[...] + bonus * v) * gate


def _wkv_kernel(r_ref, kk_ref, v_ref, lw_ref, bb_ref, kd_ref, y_ref, h_ref, *, modes):
    c, lw_n = WKV_CHUNK, WKV_LANES
    n_heads = lw_n // HEAD
    s_rows = n_heads * c
    d = pl.program_id(0)
    i = pl.program_id(3)

    @pl.when(i == 0)
    def _():
        h_ref[...] = jnp.zeros_like(h_ref)

    rev = d == 1
    row = lax.broadcasted_iota(jnp.int32, (c, c), 0)
    col = lax.broadcasted_iota(jnp.int32, (c, c), 1)
    tri = (jnp.where(rev, col - row, row - col) >= 0).astype(F32)
    srow = lax.broadcasted_iota(jnp.int32, (s_rows, s_rows), 0)
    scol = lax.broadcasted_iota(jnp.int32, (s_rows, s_rows), 1)
    ahead = jnp.where(rev, scol % c - srow % c, srow % c - scol % c)
    strict = ahead > 0
    incl = ahead >= 0
    eye = (srow == scol).astype(F32)
    same_block = lambda size: (srow // size) == (scol // size)
    lrow =lax.broadcasted_iota(jnp.int32, (lw_n, lw_n), 0)
    lcol = lax.broadcasted_iota(jnp.int32, (lw_n, lw_n), 1)
    same_head = (lrow // HEAD) == (lcol // HEAD)
    diag = lrow == lcol
    head_of_lane = lax.broadcasted_iota(jnp.int32, (s_rows, lw_n), 1) // HEAD
    head_of_row = lax.broadcasted_iota(jnp.int32, (s_rows, lw_n), 0) // c
    own = head_of_lane == head_of_row
    n_chunks = TM // c

    def stack(x):
        return jnp.where(own, jnp.concatenate([x] * n_heads, axis=0), 0.0)

    def unstack(xs):
        out = xs[0:c]
        for hh in range(1, n_heads):
            out = out + xs[hh * c:(hh + 1) * c]
        return out

    def chunk(ci, _):
        cidx = jnp.where(rev, n_chunks - 1 - ci, ci)
        r0 = pl.multiple_of(cidx * c, c)
        sl = pl.ds(r0, c)
        r, kk, v = r_ref[0, sl, :], kk_ref[0, sl, :], v_ref[0, sl, :]
        lw, bb, kd = lw_ref[0, 0, sl, :], bb_ref[0, 0, sl, :], kd_ref[0, 0, sl, :]
        cum = _dot_exact(tri, lw)
        tot = jnp.sum(lw, axis=0, keepdims=True)
        e_neg = jnp.exp(-cum)
        e_end = jnp.exp(tot - cum)
        kd_g, b_g = kd * e_end, bb * e_end
        kk_s = stack(kk * jnp.exp(cum - lw))
        r_s = stack(r * jnp.exp(cum))
        v_s = stack(v)
        lhs = jnp.concatenate([kk_s, r_s], axis=0)
        p1 = _mm(lhs, stack(kd * e_neg), modes["A"], NT_DIMS)
        p2 = _mm(lhs, stack(bb * e_neg), modes["A"], NT_DIMS)
        a_kd = jnp.where(strict, p1[:s_rows], 0.0)
        b_kd = jnp.where(incl, p1[s_rows:], 0.0)
        nmat = jnp.where(strict, p2[:s_rows], 0.0)
        b_b = jnp.where(incl, p2[s_rows:], 0.0)
        tinv = eye - jnp.where(same_block(2), nmat, 0.0)
        size = 2
        while size < c:
            n_off = jnp.where(jnp.logical_and(same_block(2 * size), jnp.logical_not(same_block(size))), nmat, 0.0)
            tinv = tinv - _mm(tinv, _mm(n_off, tinv, modes["T"]), modes["T"])
            size *= 2
        av = _mm(a_kd, v_s, modes["V"])
        tw = _mm(tinv, jnp.concatenate([kk_s, av], axis=1), modes["V"])
        bw = _mm(b_b, tw, modes["V"])
        kkp, u0 = unstack(tw[:, :lw_n]), unstack(tw[:, lw_n:])
        rp = unstack(r_s - bw[:, :lw_n])
        y0 = unstack(_mm(b_kd, v_s, modes["V"]) - bw[:, lw_n:])

        hs = h_ref[...]
        y_ref[0, 0, sl, :] = _mm(rp, hs, modes["S"]) + y0
        m_mat = jnp.where(diag, jnp.exp(tot), 0.0) - jnp.where(same_head, _mm(b_g, kkp, modes["S"], TN_DIMS), 0.0)
        g_mat = jnp.where(same_head, _mm(jnp.concatenate([kd_g, b_g], axis=0),
                                         jnp.concatenate([v, -u0], axis=0), modes["S"], TN_DIMS), 0.0)
        h_ref[...] = _mm(m_mat, hs, modes["S"]) + g_mat
        return 0

    lax.fori_loop(0, n_chunks, chunk, 0)


def _mm(a, b, mode, dims=(((1,), (0,)), ((), ()))):
    dg = lambda x, y: lax.dot_general(x, y, dims, preferred_element_type=F32)
    if mode == "f32":
        return _dot_exact(a, b, dims)
    if mode == "x1":
        return dg(a.astype(BF16), b.astype(BF16))
    ah, al = _split(a)
    bh, bl = _split(b)
    return dg(ah, bh) + (dg(ah, bl) + dg(al, bh))


def _post_kernel(*refs, odd, ff_chunk):
    if odd:
        (x_ref, y0_ref, y1_ref, gate_ref, z_ref, gng_ref, ones_ref,
         wo_ref, mod_ref, g2_ref, w1_ref, w2_ref, o_ref) = refs
        y = y0_ref[0, 0] + y1_ref[0, 0]
        dlt = y - _group_reduce(y, ones_ref)
        var = _group_reduce(dlt * dlt, ones_ref)
        mix = (dlt * lax.rsqrt(var + GN_EPS) * gng_ref[...] * gate_ref[0] + z_ref[0]).astype(BF16)
    else:
        x_ref, att_ref, rec_ref, wo_ref, mod_ref, g2_ref, w1_ref, w2_ref, o_ref = refs
        mix = jnp.concatenate([att_ref[0], rec_ref[0]], axis=1)
    x1 = x_ref[0] + mod_ref[0, 2:3, :] * _dot(mix, wo_ref[...])
    h2 = _norm_mod(x1, g2_ref[...], mod_ref[0, 3:4, :], mod_ref[0, 4:5, :]).astype(BF16)
    acc = jnp.zeros_like(x1)
    for c in range(w1_ref.shape[1] // ff_chunk):
        a = _dot(h2, w1_ref[:, c * ff_chunk:(c + 1) * ff_chunk])
        a = jnp.square(jnp.maximum(a, 0.0)).astype(BF16)
        acc = acc + _dot(a, w2_ref[c * ff_chunk:(c + 1) * ff_chunk, :])
    o_ref[0] = x1 + mod_ref[0, 5:6, :] * acc


def _blockdiag_ones(width, value):
    idx = jnp.arange(width) // HEAD
    return jnp.where(idx[:, None] == idx[None, :], value, 0.0).astype(BF16)


def _rope_tables(cl, n, grid_w):
    rows = n // grid_w
    row = jnp.repeat(jnp.arange(rows, dtype=F32), grid_w)
    col = jnp.tile(jnp.arange(grid_w, dtype=F32), rows)
    half = HEAD // 2
    inv = ROPE_THETA ** (-jnp.arange(0, half, 2, dtype=F32) / half)
    ang = jnp.concatenate([row[:, None] * inv, col[:, None] * inv], axis=-1)
    cos = jnp.repeat(jnp.cos(ang), 2, axis=-1)
    sin = jnp.repeat(jnp.sin(ang), 2, axis=-1)
    even = (jnp.arange(HEAD) % 2 == 0)[None, :]
    sa = jnp.where(even, -sin, 0.0)
    sb = jnp.where(even, 0.0, sin)
    ctx = lambda fill: jnp.full((cl, HEAD), fill, F32)
    full = lambda lat, fill: jnp.tile(jnp.concatenate([ctx(fill), lat], axis=0), (1, LANES // HEAD))
    return full(cos, 1.0), full(sa, 0.0), full(sb, 0.0)


def _tile_specs(b_all, ct, d_model):
    mod_spec = pl.BlockSpec((1, 6, d_model), lambda b, t: (jnp.where(t < ct, b_all, b), 0, 0))
    row_spec = lambda w: pl.BlockSpec((1, TM, w), lambda b, t: (b, t, 0))
    return mod_spec, row_spec


def _hybrid_layer(xs, mods, g1, w_in, w_out, qn, kn, conv_w, conv_b, gate_w, gate_b, lam, ropes, dims):
    b_all, cl, n, d_model = dims
    t_all = cl + n
    ct, nt = cl // TM, t_all // TM
    in_w = w_in.shape[1]
    rnn_w = conv_w.shape[1]
    kv_w = (in_w - 2 * rnn_w - d_model // 2) // 2
    q_w = in_w - 2 * kv_w - 2 * rnn_w
    n_kv = kv_w // HEAD
    groups = q_w // kv_w
    mod_spec, row_spec = _tile_specs(b_all, ct, d_model)
    rope_spec = pl.BlockSpec((TM, LANES), lambda b, t: (t, 0))
    cos, sa, sb = ropes

    q, k, v, xr, gl = pl.pallas_call(
        functools.partial(_hy_in_kernel, q_w=q_w, kv_w=kv_w, rnn_w=rnn_w),
        grid=(b_all, nt),
        in_specs=[row_spec(d_model), mod_spec, _const_spec((1, d_model)), _const_spec((d_model, in_w)),
                  rope_spec, rope_spec, rope_spec, _const_spec((1, q_w)), _const_spec((1, kv_w)),
                  _const_spec((MXU, MXU)), _const_spec((kv_w, kv_w))],
        out_specs=[row_spec(q_w), row_spec(kv_w), row_spec(kv_w), row_spec(rnn_w), row_spec(rnn_w)],
        out_shape=[jax.ShapeDtypeStruct((b_all, t_all, q_w), BF16),
                   jax.ShapeDtypeStruct((b_all, t_all, kv_w), BF16),
                   jax.ShapeDtypeStruct((b_all, t_all, kv_w), BF16),
                   jax.ShapeDtypeStruct((b_all, t_all, rnn_w), F32),
                   jax.ShapeDtypeStruct((b_all, t_all, rnn_w), F32)],
        compiler_params=_params(("parallel", "parallel")),
    )(xs, mods, g1.reshape(1, d_model), w_in.astype(BF16), cos, sa, sb,
      jnp.tile(qn, q_w // HEAD).reshape(1, q_w), jnp.tile(kn, kv_w // HEAD).reshape(1, kv_w),
      _blockdiag_ones(MXU, 1.0 / HEAD), _blockdiag_ones(kv_w, 1.0 / HEAD))

    att = pl.pallas_call(
        functools.partial(_attn_kernel, ct=ct, nt=nt, groups=groups, n_kv=n_kv),
        grid=(b_all, nt),
        in_specs=[row_spec(q_w),
                  pl.BlockSpec((1, t_all, kv_w), lambda b, t: (b, 0, 0)),
                  pl.BlockSpec((1, t_all, kv_w), lambda b, t: (b, 0, 0))],
        out_specs=row_spec(q_w),
        out_shape=jax.ShapeDtypeStruct((b_all, t_all, q_w), BF16),
        compiler_params=_params(("parallel", "parallel")),
    )(q, k, v)

    n_lc = rnn_w // LANES
    per = LANES // HEAD
    gw = gate_w.reshape(4, n_lc, per, HEAD, HEAD)
    eye = jnp.eye(per, dtype=F32)
    gw = jnp.einsum('gcpde,pq->cgpdqe', gw, eye).reshape(n_lc, 4, LANES, LANES).astype(BF16)
    lane_spec = lambda rows: pl.BlockSpec((rows, LANES), lambda b, c: (0, c))
    seq_spec = pl.BlockSpec((1, t_all, LANES), lambda b, c: (b, 0, c))
    rec = pl.pallas_call(
        functools.partial(_rglru_kernel, cl=cl, n=n),
        grid=(b_all, n_lc),
        in_specs=[seq_spec, seq_spec, lane_spec(CONV_K), lane_spec(1),
                  pl.BlockSpec((1, 4, LANES, LANES), lambda b, c: (c, 0, 0, 0)),
                  lane_spec(4), lane_spec(2)],
        out_specs=seq_spec,
        out_shape=jax.ShapeDtypeStruct((b_all, t_all, rnn_w), BF16),
        scratch_shapes=[pltpu.VMEM((t_all + 2 * SUBLANES, LANES), F32),
                        pltpu.VMEM((2, t_all, LANES), F32),
                        pltpu.VMEM((2, t_all, LANES), F32)],
        compiler_params=_params(("parallel", "parallel")),
    )(xr, gl, conv_w, conv_b.reshape(1, rnn_w), gw, gate_b.reshape(4, rnn_w), lam)
    return (att, rec), w_out.astype(BF16)


def _rwkv_layer(xs, mods, g1, mu, w_rkv, lora_down, lora_up, lora_bias, gate_down, gate_up,
                k_k, k_a, r_k, gn_b, dims, wkv_mode):
    b_all, cl, n, d_model = dims
    t_all = cl + n
    ct, nt = cl // TM, t_all // TM
    n8 = t_all // SUBLANES
    per8 = TM // SUBLANES
    lora = lora_down.shape[-1]
    glora = gate_down.shape[-1]
    mod_spec, row_spec = _tile_specs(b_all, ct, d_model)
    dir_spec = pl.BlockSpec((2, 1, TM, d_model), lambda b, t: (0, b, t, 0))
    wdw = jnp.concatenate([lora_down[0, 0], lora_down[1, 0]], axis=1).astype(BF16)
    wda = jnp.concatenate([lora_down[0, 1], lora_down[1, 1]], axis=1).astype(BF16)
    zeros = jnp.zeros((lora, d_model), F32)
    pad_up = lambda j: jnp.stack([jnp.concatenate([lora_up[0, j], zeros], axis=0),
                                  jnp.concatenate([zeros, lora_up[1, j]], axis=0)]).astype(BF16)
    vec = lambda a: a.reshape(1, d_model)
    outs = pl.pallas_call(
        functools.partial(_rw_in_kernel, ct=ct, nt=nt),
        grid=(b_all, nt),
        in_specs=[row_spec(d_model),
                  pl.BlockSpec((1, SUBLANES, d_model), lambda b, t: (b, jnp.maximum(t * per8 - 1, 0), 0)),
                  pl.BlockSpec((1, SUBLANES, d_model), lambda b, t: (b, jnp.minimum((t + 1) * per8, n8 - 1), 0)),
                  mod_spec, _const_spec((1, d_model)), _const_spec((6, d_model)),
                  _const_spec((3, d_model, d_model)), _const_spec((d_model, 2 * lora)),
                  _const_spec((d_model, 2 * lora)), _const_spec((2, 2 * lora, d_model)),
                  _const_spec((2, 2 * lora, d_model)), _const_spec((4, d_model)),
                  _const_spec((d_model, glora)), _const_spec((glora, d_model)),
                  _const_spec((1, d_model)), _const_spec((1, d_model)), _const_spec((1, d_model)),
                  _const_spec((1, d_model)), _const_spec((MXU, MXU))],
        out_specs=[row_spec(d_model), row_spec(d_model), row_spec(d_model),
                   dir_spec, dir_spec, dir_spec, row_spec(d_model), row_spec(d_model)],
        out_shape=[jax.ShapeDtypeStruct((b_all, t_all, d_model), F32)] * 3
        + [jax.ShapeDtypeStruct((2, b_all, t_all, d_model), F32)] * 3
        + [jax.ShapeDtypeStruct((b_all, t_all, d_model), F32)] * 2,
        compiler_params=_params(("parallel", "parallel")),
    )(xs, xs, xs, mods, vec(g1), mu, w_rkv.astype(BF16), wdw, wda, pad_up(0), pad_up(1),
      lora_bias.reshape(4, d_model), gate_down.astype(BF16), gate_up.astype(BF16),
      vec(k_k), vec(k_a), vec(r_k), vec(gn_b), _blockdiag_ones(MXU, 1.0))
    r, kk, v, lw, bb, kd, gate, z = outs

    def tmap(dd, i):
        rev = jnp.where(i < ct, ct - 1 - i, nt - 1 - (i - ct))
        return jnp.where(dd == 0, i, rev)

    shared = pl.BlockSpec((1, TM, WKV_LANES), lambda dd, b, hh, i: (b, tmap(dd, i), hh))
    per_dir = pl.BlockSpec((1, 1, TM, WKV_LANES), lambda dd, b, hh, i: (dd, b, tmap(dd, i), hh))
    y = pl.pallas_call(
        functools.partial(_wkv_kernel, modes=wkv_mode),
        grid=(2, b_all, d_model // WKV_LANES, nt),
        in_specs=[shared, shared, shared, per_dir, per_dir, per_dir],
        out_specs=per_dir,
        out_shape=jax.ShapeDtypeStruct((2, b_all, t_all, d_model), F32),
        scratch_shapes=[pltpu.VMEM((WKV_LANES, WKV_LANES), F32)],
        compiler_params=_params(("arbitrary", "arbitrary", "arbitrary", "arbitrary")),
    )(r, kk, v, lw, bb, kd)
    return y, gate, z


def _post(xs, mix_inputs, w_o, mods, g2, w1, w2, dims, odd, gn_g=None, latent_only=False):
    b_all, cl, n, d_model = dims
    t_all = cl + n
    ct, nt = cl // TM, t_all // TM
    t0 = ct if latent_only else 0
    d_ff = w1.shape[1]
    mod_spec = pl.BlockSpec((1, 6, d_model), lambda b, t: (jnp.where(t + t0 < ct, b_all, b), 0, 0))
    row_spec = lambda w: pl.BlockSpec((1, TM, w), lambda b, t: (b, t + t0, 0))
    tail_specs = [_const_spec((d_model, d_model)), mod_spec, _const_spec((1, d_model)),
                  _const_spec((d_model, d_ff)), _const_spec((d_ff, d_model))]
    tail_args = (w_o, mods, g2.reshape(1, d_model), w1.astype(BF16), w2.astype(BF16))
    if odd:
        y, gate, z = mix_inputs
        y_spec = lambda dd: pl.BlockSpec((1, 1, TM, d_model), lambda b, t: (dd, b, t + t0, 0))
        in_specs = [row_spec(d_model), y_spec(0), y_spec(1), row_spec(d_model), row_spec(d_model),
                    _const_spec((1, d_model)), _const_spec((MXU, MXU))] + tail_specs
        args = (xs, y, y, gate, z, gn_g.reshape(1, d_model), _blockdiag_ones(MXU, 1.0 / HEAD)) + tail_args
    else:
        att, rec = mix_inputs
        in_specs = [row_spec(d_model), row_spec(att.shape[-1]), row_spec(rec.shape[-1])] + tail_specs
        args = (xs, att, rec) + tail_args
    rows_out = n if latent_only else t_all
    return pl.pallas_call(
        functools.partial(_post_kernel, odd=odd, ff_chunk=min(d_ff, 4 * MXU)),
        grid=(b_all, nt - t0),
        in_specs=in_specs,
        out_specs=pl.BlockSpec((1, TM, d_model), lambda b, t: (b, t, 0)),
        out_shape=jax.ShapeDtypeStruct((b_all, rows_out, d_model), F32),
        compiler_params=_params(("parallel", "parallel")),
    )(*args)


def kernel(x, c, ctx, c_ctx, ada_w, ada_b, norm_g, mlp_w1, mlp_w2, hy_w_in, hy_w_out, hy_q_norm, hy_k_norm, hy_conv_w, hy_conv_b, hy_gate_w, hy_gate_b, hy_lam, rw_mu, rw_w_rkv, rw_w_o, rw_lora_down, rw_lora_up, rw_lora_bias, rw_gate_down, rw_gate_up, rw_k_k, rw_k_a, rw_r_k, rw_gn_g, rw_gn_b):
    b_all, n, d_model = x.shape
    cl = ctx.shape[1]
    depth = ada_w.shape[0]
    assert cl % TM == 0 and n % TM == 0 and d_model % MXU == 0
    grid_w = 64
    dims = (b_all, cl, n, d_model)
    mods_all = _ada_mods(jnp.concatenate([c, c_ctx[None, :]], axis=0), ada_w, ada_b)
    ropes = _rope_tables(cl, n, grid_w)
    xs = jnp.concatenate([ctx, x], axis=1)
    for l in range(depth):
        i = l // 2
        mods = mods_all[l]
        last = l == depth - 1
        if l % 2 == 0:
            mix, w_o = _hybrid_layer(xs, mods, norm_g[l, 0], hy_w_in[i], hy_w_out[i], hy_q_norm[i], hy_k_norm[i],
                                     hy_conv_w[i], hy_conv_b[i], hy_gate_w[i], hy_gate_b[i], hy_lam[i], ropes, dims)
            xs = _post(xs, mix, w_o, mods, norm_g[l, 1], mlp_w1[l], mlp_w2[l], dims, odd=False, latent_only=last)
        else:
            mix = _rwkv_layer(xs, mods, norm_g[l, 0], rw_mu[i], rw_w_rkv[i], rw_lora_down[i], rw_lora_up[i],
                              rw_lora_bias[i], rw_gate_down[i], rw_gate_up[i], rw_k_k[i], rw_k_a[i],
                              rw_r_k[i], rw_gn_b[i], dims, WKV_MODE)
            xs = _post(xs, mix, rw_w_o[i].astype(BF16), mods, norm_g[l, 1], mlp_w1[l], mlp_w2[l], dims,
                       odd=True, gn_g=rw_gn_g[i], latent_only=last)
    return xs if xs.shape[1] == n else xs[:, cl:]
```

```python
import functools
import math

import jax
import jax.numpy as jnp
from jax import lax
from jax.experimental import pallas as pl
from jax.experimental.pallas import tpu as pltpu

F32 = jnp.float32
BF16 = jnp.bfloat16

HEAD = 64
LANES = 128
SUBLANES = 8
MXU = 256
TM = 256
VMEM_LIMIT = 56 * 1024 * 1024

EPS = 1e-6
GN_EPS = 64e-5
RG_C = 8.0
ROPE_THETA = 10000.0
DECAY_SCALE = math.exp(-0.5)
CONV_K = 4
CONV_LEFT = 2
Q_SCALE = HEAD ** -0.5 * math.log2(math.e)
V_ONES = 16
WKV_CHUNK = 64
WKV_LANES = 256

NT_DIMS = (((1,), (1,)), ((), ()))
TN_DIMS = (((0,), (0,)), ((), ()))


def _params(sem):
    return pltpu.CompilerParams(dimension_semantics=sem, vmem_limit_bytes=VMEM_LIMIT)


def _const_spec(shape):
    nd = len(shape)
    return pl.BlockSpec(shape, lambda *_: (0,) * nd, pipeline_mode=pl.Buffered(1))


def _dot(a, b):
    return jnp.dot(a, b, preferred_element_type=F32)


def _dot_exact(a, b, dims=None):
    if dims is None:
        return jnp.dot(a, b, preferred_element_type=F32, precision=lax.Precision.HIGHEST)
    return lax.dot_general(a, b, dims, preferred_element_type=F32, precision=lax.Precision.HIGHEST)


def _split(x):
    hi = x.astype(BF16)
    lo = (x - hi.astype(F32)).astype(BF16)
    return hi, lo


def _group_reduce(x, ones_ref):
    cw = ones_ref.shape[0]
    ones = ones_ref[...]
    outs = []
    for c in range(x.shape[1] // cw):
        hi, lo = _split(x[:, c * cw:(c + 1) * cw])
        outs.append(_dot(hi, ones) + _dot(lo, ones))
    return outs[0] if len(outs) == 1 else jnp.concatenate(outs, axis=1)


def _norm_mod(x, g, shift, scale):
    ms = jnp.mean(x * x, axis=-1, keepdims=True)
    return (x * lax.rsqrt(ms + EPS) * g) * (1.0 + scale) + shift


def _ada_kernel(c_ref, w_ref, b_ref, o_ref):
    c = c_ref[...]
    s = c * jax.nn.sigmoid(c)
    o_ref[0, 0] = _dot_exact(s, w_ref[0]) + b_ref[0]


def _ada_mods(cc, ada_w, ada_b):
    depth, d, _ = ada_w.shape
    rows = cc.shape[0]
    out = pl.pallas_call(
        _ada_kernel,
        grid=(depth, 6),
        in_specs=[pl.BlockSpec((rows, d), lambda l, j: (0, 0)),
                  pl.BlockSpec((1, d, d), lambda l, j: (l, 0, j)),
                  pl.BlockSpec((1, 1, d), lambda l, j: (l * 6 + j, 0, 0))],
        out_specs=pl.BlockSpec((1, 1, rows, d), lambda l, j: (l, j, 0, 0)),
        out_shape=jax.ShapeDtypeStruct((depth, 6, rows, d), F32),
        compiler_params=_params(("arbitrary", "arbitrary")),
    )(cc, ada_w, ada_b.reshape(depth * 6, 1, d))
    return jnp.transpose(out, (0, 2, 1, 3))


def _hy_in_kernel(x_ref, mod_ref, g_ref, w_ref, cos_ref, sa_ref, sb_ref, qg_ref, kg_ref, oq_ref, ok_ref,
                  q_out, k_out, vt_out, xr_out, gl_out, *, q_w, kv_w, rnn_w):
    x = x_ref[0]
    h = _norm_mod(x, g_ref[...], mod_ref[0, 0:1, :], mod_ref[0, 1:2, :]).astype(BF16)
    z = _dot(h, w_ref[...])
    c0, c1, c2, c3 = q_w, q_w + kv_w, q_w + 2 * kv_w, q_w + 2 * kv_w + rnn_w

    def norm_rope(u, gain, ones_ref):
        width = u.shape[1]
        un = u * lax.rsqrt(_group_reduce(u * u, ones_ref) + EPS) * gain
        reps = width // LANES
        tile = lambda r: jnp.concatenate([r[...]] * reps, axis=1) if reps > 1 else r[...]
        nxt = pltpu.roll(un, width - 1, 1)
        prv = pltpu.roll(un, 1, 1)
        return un * tile(cos_ref) + nxt * tile(sa_ref) + prv * tile(sb_ref)

    q_out[0] = (norm_rope(z[:, :c0], qg_ref[...], oq_ref) * Q_SCALE).astype(BF16)
    k_out[0] = norm_rope(z[:, c0:c1], kg_ref[...], ok_ref).astype(BF16)
    vt = z[:, c1:c2].T
    ones = jnp.ones((V_ONES, TM), F32)
    for hk in range(kv_w // HEAD):
        vt_out[0, hk] = jnp.concatenate([vt[hk * HEAD:(hk + 1) * HEAD], ones], axis=0).astype(BF16)
    xr_out[0] = z[:, c2:c3]
    gl_out[0] = z[:, c3:]


def _attn_kernel(q_ref, k_ref, vt_ref, o_ref, ot_ref, *, ct, nt, groups, n_kv):
    t = pl.program_id(1)
    nkv = jnp.where(t < ct, ct, nt)
    q = q_ref[0].astype(F32)
    lane = lax.broadcasted_iota(jnp.int32, (TM, LANES), 1)
    low = lane < HEAD
    n_sub = TM // LANES
    blocks = []
    for head in range(groups * n_kv):
        hk = head // groups
        qc = q[:, LANES * (head // 2):LANES * (head // 2 + 1)]
        if head % 2 != hk % 2:
            qc = pltpu.roll(qc, HEAD, 1)
        qm = jnp.where(low if hk % 2 == 0 else jnp.logical_not(low), qc, 0.0).astype(BF16)
        blocks += [(hk, qm[s * LANES:(s + 1) * LANES]) for s in range(n_sub)]

    def body(i, carry):
        r0 = pl.multiple_of(i * TM, TM)
        kcs = [k_ref[0, pl.ds(r0, TM), LANES * (hk // 2):LANES * (hk // 2 + 1)] for hk in range(n_kv)]
        vts = [vt_ref[0, hk, :, pl.ds(r0, TM)] for hk in range(n_kv)]
        sts = [lax.dot_general(kcs[hk], qm, NT_DIMS, preferred_element_type=F32) for hk, qm in blocks]
        ms = [jnp.maximum(m, jnp.max(st, axis=0, keepdims=True)) for st, (m, _) in zip(sts, carry)]
        pvs = [_dot(vts[hk], jnp.exp2(st - m_new).astype(BF16)) for (hk, _), st, m_new in zip(blocks, sts, ms)]
        return tuple((m_new, jnp.exp2(m - m_new) * acc + pv) for m_new, pv, (m, acc) in zip(ms, pvs, carry))

    init = tuple((jnp.full((1, LANES), -1e30, F32), jnp.zeros((vt_ref.shape[2], LANES), F32)) for _ in blocks)
    res = lax.fori_loop(0, nkv, body, init)
    for idx, (_, acc) in enumerate(res):
        head, s = idx // n_sub, idx % n_sub
        ot_ref[HEAD * head:HEAD * (head + 1), s * LANES:(s + 1) * LANES] = acc[:HEAD] / acc[HEAD:HEAD + 1]
    o_ref[0] = ot_ref[...].T.astype(BF16)


def _scan8(a, u, reverse):
    row = lax.broadcasted_iota(jnp.int32, a.shape, 0)
    for d in (1, 2, 4):
        if reverse:
            a_s, u_s, ok = pltpu.roll(a, SUBLANES - d, 0), pltpu.roll(u, SUBLANES - d, 0), row < SUBLANES - d
        else:
            a_s, u_s, ok = pltpu.roll(a, d, 0), pltpu.roll(u, d, 0), row >= d
        u = a * jnp.where(ok, u_s, 0.0) + u
        a = a * jnp.where(ok, a_s, 1.0)
    return a, u


def _rglru_kernel(xr_ref, gl_ref, cw_ref, cb_ref, gw_ref, gb_ref, lam_ref, o_ref,
                  xs_ref, a_ref, u_ref, *, cl, n):
    t_all = cl + n
    pad = SUBLANES
    xs_ref[0:pad, :] = jnp.zeros((pad, LANES), F32)
    xs_ref[pad + t_all:, :] = jnp.zeros((pad, LANES), F32)
    xs_ref[pad:pad + t_all, :] = xr_ref[0]
    lam = lam_ref[...]
    z = -lam
    softplus = jnp.maximum(z, 0.0) + jnp.log(1.0 + jnp.exp(-jnp.abs(z)))
    cw = cw_ref[...]
    cb = cb_ref[...]
    gb = gb_ref[...]

    def coeffs(i, _):
        r0 = pl.multiple_of(i * TM, TM)
        blk = xs_ref[pl.ds(r0, TM + 2 * pad), :]
        rows = r0 + lax.broadcasted_iota(jnp.int32, (TM, 1), 0)
        in_lat = rows >= cl
        pos = jnp.where(in_lat, rows - cl, rows)
        seqlen = jnp.where(in_lat, n, cl)
        xc = jnp.zeros((TM, LANES), F32) + cb
        for j in range(CONV_K):
            off = j - CONV_LEFT
            tap = blk[pad + off:pad + off + TM, :]
            ok = jnp.logical_and(pos + off >= 0, pos + off < seqlen)
            xc = xc + jnp.where(ok, tap, 0.0) * cw[j:j + 1, :]
        xcb = xc.astype(BF16)
        for d in range(2):
            r = jax.nn.sigmoid(_dot(xcb, gw_ref[0, 2 * d]) + gb[2 * d:2 * d + 1, :])
            ig = jax.nn.sigmoid(_dot(xcb, gw_ref[0, 2 * d + 1]) + gb[2 * d + 1:2 * d + 2, :])
            log_a = -RG_C * r * softplus[d:d + 1, :]
            a_ref[d, pl.ds(r0, TM), :] = jnp.exp(log_a)
            u_ref[d, pl.ds(r0, TM), :] = jnp.sqrt(1.0 - jnp.exp(2.0 * log_a)) * (ig * xc)
        return 0

    lax.fori_loop(0, t_all // TM, coeffs, 0)

    def step(d, reverse):
        def f(g, carry):
            r0 = pl.multiple_of(g * SUBLANES, SUBLANES)
            a, u = _scan8(a_ref[d, pl.ds(r0, SUBLANES), :], u_ref[d, pl.ds(r0, SUBLANES), :], reverse)
            h = a * carry + u
            u_ref[d, pl.ds(r0, SUBLANES), :] = h
            return h[0:1, :] if reverse else h[SUBLANES - 1:SUBLANES, :]
        return f

    zero = jnp.zeros((1, LANES), F32)
    g_all, g_ctx = t_all // SUBLANES, cl // SUBLANES
    lax.fori_loop(0, g_all, step(0, False), zero, unroll=4)
    f_rev = step(1, True)
    carry = lax.fori_loop(0, g_ctx, lambda i, c: f_rev(g_ctx - 1 - i, c), zero, unroll=4)
    lax.fori_loop(0, g_all - g_ctx, lambda i, c: f_rev(g_all - 1 - i, c), carry, unroll=4)

    def combine(i, _):
        r0 = pl.multiple_of(i * TM, TM)
        hsum = u_ref[0, pl.ds(r0, TM), :] + u_ref[1, pl.ds(r0, TM), :]
        o_ref[0, pl.ds(r0, TM), :] = (jax.nn.gelu(gl_ref[0, pl.ds(r0, TM), :]) * hsum).astype(BF16)
        return 0

    lax.fori_loop(0, t_all // TM, combine, 0)


def _rw_in_kernel(x_ref, xp_ref, xn_ref, mod_ref, g_ref, mu_ref, wrkv_ref, wdw_ref, wda_ref, wuw_ref, wua_ref,
                  lb_ref, gd_ref, gu_ref, kk_ref, ka_ref, rk_ref, gnb_ref, ones_ref,
                  r_out, kk_out, v_out, lw_out, bb_out, kd_out, g_out, z_out, *, ct, nt):
    t = pl.program_id(1)
    g = g_ref[...]
    shift, scale = mod_ref[0, 0:1, :], mod_ref[0, 1:2, :]
    h = _norm_mod(x_ref[0], g, shift, scale)
    first = jnp.logical_or(t == 0, t == ct)
    last = jnp.logical_or(t == ct - 1, t == nt - 1)
    hp = _norm_mod(xp_ref[0], g, shift, scale)[SUBLANES - 1:SUBLANES, :]
    hn = _norm_mod(xn_ref[0], g, shift, scale)[0:1, :]
    hp = jnp.where(first, 0.0, hp)
    hn = jnp.where(last, 0.0, hn)
    row = lax.broadcasted_iota(jnp.int32, (TM, 1), 0)
    h_prev = jnp.where(row == 0, hp, pltpu.roll(h, 1, 0))
    h_next = jnp.where(row == TM - 1, hn, pltpu.roll(h, TM - 1, 0))
    xx = 0.5 * (h_prev + h_next) - h
    lerp = lambda j: (h + xx * mu_ref[j:j + 1, :]).astype(BF16)

    r = _dot(lerp(0), wrkv_ref[0])
    k = _dot(lerp(2), wrkv_ref[1])
    v = _dot(lerp(3), wrkv_ref[2])
    gate = _dot(jax.nn.sigmoid(_dot(lerp(5), gd_ref[...])).astype(BF16), gu_ref[...])
    tw = jnp.tanh(_dot(lerp(1), wdw_ref[...])).astype(BF16)
    ta = _dot(lerp(4), wda_ref[...]).astype(BF16)

    kk = k * kk_ref[...]
    nrm = jnp.sqrt(_group_reduce(kk * kk, ones_ref))
    kk = kk / jnp.maximum(nrm, 1e-12)
    r_out[0] = r
    kk_out[0] = kk
    v_out[0] = v
    kd_sum = jnp.zeros_like(k)
    for d in range(2):
        dec = lb_ref[2 * d:2 * d + 1, :] + _dot(tw, wuw_ref[d])
        a = jax.nn.sigmoid(lb_ref[2 * d + 1:2 * d + 2, :] + _dot(ta, wua_ref[d]))
        kd = k * (1.0 + (a - 1.0) * ka_ref[...])
        lw_out[d, 0] = -DECAY_SCALE * jax.nn.sigmoid(dec)
        bb_out[d, 0] = kk * a
        kd_out[d, 0] = kd
        kd_sum = kd_sum + kd
    bonus = _group_reduce(r * kd_sum * rk_ref[...], ones_ref)
    g_out[0] = gate
    z_out[0] = (gnb_ref[...] + bonus * v) * gate


def _wkv_kernel(r_ref, kk_ref, v_ref, lw_ref, bb_ref, kd_ref, y_ref, h_ref):
    c, lw_n = WKV_CHUNK, WKV_LANES
    n_heads = lw_n // HEAD
    s_rows = n_heads * c
    d = pl.program_id(0)
    i = pl.program_id(3)

    @pl.when(i == 0)
    def _():
        h_ref[...] = jnp.zeros_like(h_ref)

    rev = d == 1
    row = lax.broadcasted_iota(jnp.int32, (c, c), 0)
    col = lax.broadcasted_iota(jnp.int32, (c, c), 1)
    tri = (jnp.where(rev, col - row, row - col) >= 0).astype(F32)
    srow = lax.broadcasted_iota(jnp.int32, (s_rows, s_rows), 0)
    scol = lax.broadcasted_iota(jnp.int32, (s_rows, s_rows), 1)
    ahead = jnp.where(rev, scol % c - srow % c, srow % c - scol % c)
    strict = ahead > 0
    incl = ahead >= 0
    eye = (srow == scol).astype(F32)
    same_block = lambda size: (srow // size) == (scol // size)
    lrow =lax.broadcasted_iota(jnp.int32, (lw_n, lw_n), 0)
    lcol = lax.broadcasted_iota(jnp.int32, (lw_n, lw_n), 1)
    same_head = (lrow // HEAD) == (lcol // HEAD)
    diag = lrow == lcol
    head_of_lane = lax.broadcasted_iota(jnp.int32, (s_rows, lw_n), 1) // HEAD
    head_of_row = lax.broadcasted_iota(jnp.int32, (s_rows, lw_n), 0) // c
    own = head_of_lane == head_of_row
    n_chunks = TM // c

    def stack(x):
        return jnp.where(own, jnp.concatenate([x] * n_heads, axis=0), 0.0)

    def unstack(xs):
        out = xs[0:c]
        for hh in range(1, n_heads):
            out = out + xs[hh * c:(hh + 1) * c]
        return out

    def local(ci):
        cidx = jnp.where(rev, n_chunks - 1 - ci, ci)
        sl = pl.ds(pl.multiple_of(cidx * c, c), c)
        r, kk, v = r_ref[0, sl, :], kk_ref[0, sl, :], v_ref[0, sl, :]
        lw, bb, kd = lw_ref[0, 0, sl, :], bb_ref[0, 0, sl, :], kd_ref[0, 0, sl, :]
        cum = _dot_exact(tri, lw)
        tot = jnp.sum(lw, axis=0, keepdims=True)
        e_neg = jnp.exp(-cum)
        e_end = jnp.exp(tot - cum)
        kd_g, b_g = kd * e_end, bb * e_end
        kk_s = stack(kk * jnp.exp(cum - lw))
        r_s = stack(r * jnp.exp(cum))
        v_s = stack(v).astype(BF16)
        lhs = jnp.concatenate([kk_s, r_s], axis=0)
        p1 = _bdot(lhs, stack(kd * e_neg), NT_DIMS)
        p2 = _bdot(lhs, stack(bb * e_neg), NT_DIMS)
        yield
        a_kd = jnp.where(strict, p1[:s_rows], 0.0)
        b_kd = jnp.where(incl, p1[s_rows:], 0.0)
        nmat = jnp.where(strict, p2[:s_rows], 0.0)
        b_b = jnp.where(incl, p2[s_rows:], 0.0).astype(BF16)
        tinv = eye - jnp.where(same_block(2), nmat, 0.0)
        size = 2
        while size < c:
            n_off = jnp.where(jnp.logical_and(same_block(2 * size), jnp.logical_not(same_block(size))), nmat, 0.0)
            tb = tinv.astype(BF16)
            nt = _bdot(n_off, tb)
            yield
            tinv = tinv - _bdot(tb, nt)
            yield
            size *= 2
        av = _bdot(a_kd, v_s)
        yield
        tw = _bdot(tinv, jnp.concatenate([kk_s, av], axis=1))
        yield
        bw = _bdot(b_b, tw)
        yield
        kkp, u0 = unstack(tw[:, :lw_n]), unstack(tw[:, lw_n:])
        rp = unstack(r_s - bw[:, :lw_n])
        y0 = unstack(_bdot(b_kd, v_s) - bw[:, lw_n:])
        m_mat = jnp.where(diag, jnp.exp(tot), 0.0) - jnp.where(same_head, _bdot(b_g, kkp, TN_DIMS), 0.0)
        g_mat = jnp.where(same_head, _bdot(jnp.concatenate([kd_g, b_g], axis=0),
                                           jnp.concatenate([v, -u0], axis=0), TN_DIMS), 0.0)
        return sl, rp, y0, m_mat, g_mat

    chains = [local(ci) for ci in range(n_chunks)]
    parts = [None] * n_chunks
    while any(p is None for p in parts):
        for ci, chain in enumerate(chains):
            if parts[ci] is None:
                try:
                    next(chain)
                except StopIteration as done:
                    parts[ci] = done.value
    hs = h_ref[...]
    for sl, rp, y0, m_mat, g_mat in parts:
        y_ref[0, 0, sl, :] = _bdot(rp, hs) + y0
        hs = _bdot(m_mat, hs) + g_mat
    h_ref[...] = hs


def _bdot(a, b, dims=(((1,), (0,)), ((), ()))):
    return lax.dot_general(a.astype(BF16), b.astype(BF16), dims, preferred_element_type=F32)


def _post_kernel(*refs, odd, ff_chunk):
    if odd:
        (x_ref, y0_ref, y1_ref, gate_ref, z_ref, gng_ref, ones_ref,
         wo_ref, mod_ref, g2_ref, w1_ref, w2_ref, o_ref) = refs
        y = y0_ref[0, 0] + y1_ref[0, 0]
        dlt = y - _group_reduce(y, ones_ref)
        var = _group_reduce(dlt * dlt, ones_ref)
        mix = (dlt * lax.rsqrt(var + GN_EPS) * gng_ref[...] * gate_ref[0] + z_ref[0]).astype(BF16)
    else:
        x_ref, att_ref, rec_ref, wo_ref, mod_ref, g2_ref, w1_ref, w2_ref, o_ref = refs
        mix = jnp.concatenate([att_ref[0], rec_ref[0]], axis=1)
    x1 = x_ref[0] + mod_ref[0, 2:3, :] * _dot(mix, wo_ref[...])
    h2 = _norm_mod(x1, g2_ref[...], mod_ref[0, 3:4, :], mod_ref[0, 4:5, :]).astype(BF16)
    acc = jnp.zeros_like(x1)
    for c in range(w1_ref.shape[1] // ff_chunk):
        a = _dot(h2, w1_ref[:, c * ff_chunk:(c + 1) * ff_chunk])
        a = jnp.square(jnp.maximum(a, 0.0)).astype(BF16)
        acc = acc + _dot(a, w2_ref[c * ff_chunk:(c + 1) * ff_chunk, :])
    o_ref[0] = x1 + mod_ref[0, 5:6, :] * acc


def _blockdiag_ones(width, value):
    idx = jnp.arange(width) // HEAD
    return jnp.where(idx[:, None] == idx[None, :], value, 0.0).astype(BF16)


def _rope_tables(cl, n, grid_w):
    rows = n // grid_w
    row = jnp.repeat(jnp.arange(rows, dtype=F32), grid_w)
    col = jnp.tile(jnp.arange(grid_w, dtype=F32), rows)
    half = HEAD // 2
    inv = ROPE_THETA ** (-jnp.arange(0, half, 2, dtype=F32) / half)
    ang = jnp.concatenate([row[:, None] * inv, col[:, None] * inv], axis=-1)
    cos = jnp.repeat(jnp.cos(ang), 2, axis=-1)
    sin = jnp.repeat(jnp.sin(ang), 2, axis=-1)
    even = (jnp.arange(HEAD) % 2 == 0)[None, :]
    sa = jnp.where(even, -sin, 0.0)
    sb = jnp.where(even, 0.0, sin)
    ctx = lambda fill: jnp.full((cl, HEAD), fill, F32)
    full = lambda lat, fill: jnp.tile(jnp.concatenate([ctx(fill), lat], axis=0), (1, LANES // HEAD))
    return full(cos, 1.0), full(sa, 0.0), full(sb, 0.0)


def _tile_specs(b_all, ct, d_model):
    mod_spec = pl.BlockSpec((1, 6, d_model), lambda b, t: (jnp.where(t < ct, b_all, b), 0, 0))
    row_spec = lambda w: pl.BlockSpec((1, TM, w), lambda b, t: (b, t, 0))
    return mod_spec, row_spec


def _hybrid_layer(xs, mods, g1, w_in, w_out, qn, kn, conv_w, conv_b, gate_w, gate_b, lam, ropes, dims):
    b_all, cl, n, d_model = dims
    t_all = cl + n
    ct, nt = cl // TM, t_all // TM
    in_w = w_in.shape[1]
    rnn_w = conv_w.shape[1]
    kv_w = (in_w - 2 * rnn_w - d_model // 2) // 2
    q_w = in_w - 2 * kv_w - 2 * rnn_w
    n_kv = kv_w // HEAD
    groups = q_w // kv_w
    mod_spec, row_spec = _tile_specs(b_all, ct, d_model)
    rope_spec = pl.BlockSpec((TM, LANES), lambda b, t: (t, 0))
    cos, sa, sb = ropes

    q, k, vt, xr, gl = pl.pallas_call(
        functools.partial(_hy_in_kernel, q_w=q_w, kv_w=kv_w, rnn_w=rnn_w),
        grid=(b_all, nt),
        in_specs=[row_spec(d_model), mod_spec, _const_spec((1, d_model)), _const_spec((d_model, in_w)),
                  rope_spec, rope_spec, rope_spec, _const_spec((1, q_w)), _const_spec((1, kv_w)),
                  _const_spec((MXU, MXU)), _const_spec((kv_w, kv_w))],
        out_specs=[row_spec(q_w), row_spec(kv_w),
                   pl.BlockSpec((1, n_kv, HEAD + V_ONES, TM), lambda b, t: (b, 0, 0, t)),
                   row_spec(rnn_w), row_spec(rnn_w)],
        out_shape=[jax.ShapeDtypeStruct((b_all, t_all, q_w), BF16),
                   jax.ShapeDtypeStruct((b_all, t_all, kv_w), BF16),
                   jax.ShapeDtypeStruct((b_all, n_kv, HEAD + V_ONES, t_all), BF16),
                   jax.ShapeDtypeStruct((b_all, t_all, rnn_w), F32),
                   jax.ShapeDtypeStruct((b_all, t_all, rnn_w), F32)],
        compiler_params=_params(("parallel", "parallel")),
    )(xs, mods, g1.reshape(1, d_model), w_in.astype(BF16), cos, sa, sb,
      jnp.tile(qn, q_w // HEAD).reshape(1, q_w), jnp.tile(kn, kv_w // HEAD).reshape(1, kv_w),
      _blockdiag_ones(MXU, 1.0 / HEAD), _blockdiag_ones(kv_w, 1.0 / HEAD))

    att = pl.pallas_call(
        functools.partial(_attn_kernel, ct=ct, nt=nt, groups=groups, n_kv=n_kv),
        grid=(b_all, nt),
        in_specs=[row_spec(q_w),
                  pl.BlockSpec((1, t_all, kv_w), lambda b, t: (b, 0, 0)),
                  pl.BlockSpec((1, n_kv, HEAD + V_ONES, t_all), lambda b, t: (b, 0, 0, 0))],
        out_specs=row_spec(q_w),
        out_shape=jax.ShapeDtypeStruct((b_all, t_all, q_w), BF16),
        scratch_shapes=[pltpu.VMEM((q_w, TM), F32)],
        compiler_params=_params(("parallel", "parallel")),
    )(q, k, vt)

    n_lc = rnn_w // LANES
    per = LANES // HEAD
    gw = gate_w.reshape(4, n_lc, per, HEAD, HEAD)
    eye = jnp.eye(per, dtype=F32)
    gw = jnp.einsum('gcpde,pq->cgpdqe', gw, eye).reshape(n_lc, 4, LANES, LANES).astype(BF16)
    lane_spec = lambda rows: pl.BlockSpec((rows, LANES), lambda b, c: (0, c))
    seq_spec = pl.BlockSpec((1, t_all, LANES), lambda b, c: (b, 0, c))
    rec = pl.pallas_call(
        functools.partial(_rglru_kernel, cl=cl, n=n),
        grid=(b_all, n_lc),
        in_specs=[seq_spec, seq_spec, lane_spec(CONV_K), lane_spec(1),
                  pl.BlockSpec((1, 4, LANES, LANES), lambda b, c: (c, 0, 0, 0)),
                  lane_spec(4), lane_spec(2)],
        out_specs=seq_spec,
        out_shape=jax.ShapeDtypeStruct((b_all, t_all, rnn_w), BF16),
        scratch_shapes=[pltpu.VMEM((t_all + 2 * SUBLANES, LANES), F32),
                        pltpu.VMEM((2, t_all, LANES), F32),
                        pltpu.VMEM((2, t_all, LANES), F32)],
        compiler_params=_params(("parallel", "parallel")),
    )(xr, gl, conv_w, conv_b.reshape(1, rnn_w), gw, gate_b.reshape(4, rnn_w), lam)
    return (att, rec), w_out.astype(BF16)


def _rwkv_layer(xs, mods, g1, mu, w_rkv, lora_down, lora_up, lora_bias, gate_down, gate_up,
                k_k, k_a, r_k, gn_b, dims):
    b_all, cl, n, d_model = dims
    t_all = cl + n
    ct, nt = cl // TM, t_all // TM
    n8 = t_all // SUBLANES
    per8 = TM // SUBLANES
    lora = lora_down.shape[-1]
    glora = gate_down.shape[-1]
    mod_spec, row_spec = _tile_specs(b_all, ct, d_model)
    dir_spec = pl.BlockSpec((2, 1, TM, d_model), lambda b, t: (0, b, t, 0))
    wdw = jnp.concatenate([lora_down[0, 0], lora_down[1, 0]], axis=1).astype(BF16)
    wda = jnp.concatenate([lora_down[0, 1], lora_down[1, 1]], axis=1).astype(BF16)
    zeros = jnp.zeros((lora, d_model), F32)
    pad_up = lambda j: jnp.stack([jnp.concatenate([lora_up[0, j], zeros], axis=0),
                                  jnp.concatenate([zeros, lora_up[1, j]], axis=0)]).astype(BF16)
    vec = lambda a: a.reshape(1, d_model)
    outs = pl.pallas_call(
        functools.partial(_rw_in_kernel, ct=ct, nt=nt),
        grid=(b_all, nt),
        in_specs=[row_spec(d_model),
                  pl.BlockSpec((1, SUBLANES, d_model), lambda b, t: (b, jnp.maximum(t * per8 - 1, 0), 0)),
                  pl.BlockSpec((1, SUBLANES, d_model), lambda b, t: (b, jnp.minimum((t + 1) * per8, n8 - 1), 0)),
                  mod_spec, _const_spec((1, d_model)), _const_spec((6, d_model)),
                  _const_spec((3, d_model, d_model)), _const_spec((d_model, 2 * lora)),
                  _const_spec((d_model, 2 * lora)), _const_spec((2, 2 * lora, d_model)),
                  _const_spec((2, 2 * lora, d_model)), _const_spec((4, d_model)),
                  _const_spec((d_model, glora)), _const_spec((glora, d_model)),
                  _const_spec((1, d_model)), _const_spec((1, d_model)), _const_spec((1, d_model)),
                  _const_spec((1, d_model)), _const_spec((MXU, MXU))],
        out_specs=[row_spec(d_model), row_spec(d_model), row_spec(d_model),
                   dir_spec, dir_spec, dir_spec, row_spec(d_model), row_spec(d_model)],
        out_shape=[jax.ShapeDtypeStruct((b_all, t_all, d_model), F32)] * 3
        + [jax.ShapeDtypeStruct((2, b_all, t_all, d_model), F32)] * 3
        + [jax.ShapeDtypeStruct((b_all, t_all, d_model), F32)] * 2,
        compiler_params=_params(("parallel", "parallel")),
    )(xs, xs, xs, mods, vec(g1), mu, w_rkv.astype(BF16), wdw, wda, pad_up(0), pad_up(1),
      lora_bias.reshape(4, d_model), gate_down.astype(BF16), gate_up.astype(BF16),
      vec(k_k), vec(k_a), vec(r_k), vec(gn_b), _blockdiag_ones(MXU, 1.0))
    r, kk, v, lw, bb, kd, gate, z = outs

    def tmap(dd, i):
        rev = jnp.where(i < ct, ct - 1 - i, nt - 1 - (i - ct))
        return jnp.where(dd == 0, i, rev)

    shared = pl.BlockSpec((1, TM, WKV_LANES), lambda dd, b, hh, i: (b, tmap(dd, i), hh))
    per_dir = pl.BlockSpec((1, 1, TM, WKV_LANES), lambda dd, b, hh, i: (dd, b, tmap(dd, i), hh))
    y = pl.pallas_call(
        _wkv_kernel,
        grid=(2, b_all, d_model // WKV_LANES, nt),
        in_specs=[shared, shared, shared, per_dir, per_dir, per_dir],
        out_specs=per_dir,
        out_shape=jax.ShapeDtypeStruct((2, b_all, t_all, d_model), F32),
        scratch_shapes=[pltpu.VMEM((WKV_LANES, WKV_LANES), F32)],
        compiler_params=_params(("arbitrary", "arbitrary", "arbitrary", "arbitrary")),
    )(r, kk, v, lw, bb, kd)
    return y, gate, z


def _post(xs, mix_inputs, w_o, mods, g2, w1, w2, dims, odd, gn_g=None, latent_only=False):
    b_all, cl, n, d_model = dims
    t_all = cl + n
    ct, nt = cl // TM, t_all // TM
    t0 = ct if latent_only else 0
    d_ff = w1.shape[1]
    mod_spec = pl.BlockSpec((1, 6, d_model), lambda b, t: (jnp.where(t + t0 < ct, b_all, b), 0, 0))
    row_spec = lambda w: pl.BlockSpec((1, TM, w), lambda b, t: (b, t + t0, 0))
    tail_specs = [_const_spec((d_model, d_model)), mod_spec, _const_spec((1, d_model)),
                  _const_spec((d_model, d_ff)), _const_spec((d_ff, d_model))]
    tail_args = (w_o, mods, g2.reshape(1, d_model), w1.astype(BF16), w2.astype(BF16))
    if odd:
        y, gate, z = mix_inputs
        y_spec = lambda dd: pl.BlockSpec((1, 1, TM, d_model), lambda b, t: (dd, b, t + t0, 0))
        in_specs = [row_spec(d_model), y_spec(0), y_spec(1), row_spec(d_model), row_spec(d_model),
                    _const_spec((1, d_model)), _const_spec((MXU, MXU))] + tail_specs
        args = (xs, y, y, gate, z, gn_g.reshape(1, d_model), _blockdiag_ones(MXU, 1.0 / HEAD)) + tail_args
    else:
        att, rec = mix_inputs
        in_specs = [row_spec(d_model), row_spec(att.shape[-1]), row_spec(rec.shape[-1])] + tail_specs
        args = (xs, att, rec) + tail_args
    rows_out = n if latent_only else t_all
    return pl.pallas_call(
        functools.partial(_post_kernel, odd=odd, ff_chunk=min(d_ff, 4 * MXU)),
        grid=(b_all, nt - t0),
        in_specs=in_specs,
        out_specs=pl.BlockSpec((1, TM, d_model), lambda b, t: (b, t, 0)),
        out_shape=jax.ShapeDtypeStruct((b_all, rows_out, d_model), F32),
        compiler_params=_params(("parallel", "parallel")),
    )(*args)


def kernel(x, c, ctx, c_ctx, ada_w, ada_b, norm_g, mlp_w1, mlp_w2, hy_w_in, hy_w_out, hy_q_norm, hy_k_norm, hy_conv_w, hy_conv_b, hy_gate_w, hy_gate_b, hy_lam, rw_mu, rw_w_rkv, rw_w_o, rw_lora_down, rw_lora_up, rw_lora_bias, rw_gate_down, rw_gate_up, rw_k_k, rw_k_a, rw_r_k, rw_gn_g, rw_gn_b):
    b_all, n, d_model = x.shape
    cl = ctx.shape[1]
    depth = ada_w.shape[0]
    assert cl % TM == 0 and n % TM == 0 and d_model % MXU == 0
    grid_w = 64
    dims = (b_all, cl, n, d_model)
    mods_all = _ada_mods(jnp.concatenate([c, c_ctx[None, :]], axis=0), ada_w, ada_b)
    ropes = _rope_tables(cl, n, grid_w)
    xs = jnp.concatenate([ctx, x], axis=1)
    for l in range(depth):
        i = l // 2
        mods = mods_all[l]
        last = l == depth - 1
        if l % 2 == 0:
            mix, w_o = _hybrid_layer(xs, mods, norm_g[l, 0], hy_w_in[i], hy_w_out[i], hy_q_norm[i], hy_k_norm[i],
                                     hy_conv_w[i], hy_conv_b[i], hy_gate_w[i], hy_gate_b[i], hy_lam[i], ropes, dims)
            xs = _post(xs, mix, w_o, mods, norm_g[l, 1], mlp_w1[l], mlp_w2[l], dims, odd=False, latent_only=last)
        else:
            mix = _rwkv_layer(xs, mods, norm_g[l, 0], rw_mu[i], rw_w_rkv[i], rw_lora_down[i], rw_lora_up[i],
                              rw_lora_bias[i], rw_gate_down[i], rw_gate_up[i], rw_k_k[i], rw_k_a[i],
                              rw_r_k[i], rw_gn_b[i], dims)
            xs = _post(xs, mix, rw_w_o[i].astype(BF16), mods, norm_g[l, 1], mlp_w1[l], mlp_w2[l], dims,
                       odd=True, gn_g=rw_gn_g[i], latent_only=last)
    return xs if xs.shape[1] == n else xs[:, cl:]
```

```python
import functools
import math

import jax
import jax.numpy as jnp
from jax import lax
from jax.experimental import pallas as pl
from jax.experimental.pallas import tpu as pltpu

F32 = jnp.float32
BF16 = jnp.bfloat16

HEAD = 64
LANES = 128
SUBLANES = 8
MXU = 256
TM = 256
VMEM_LIMIT = 56 * 1024 * 1024

EPS = 1e-6
GN_EPS = 64e-5
RG_C = 8.0
ROPE_THETA = 10000.0
DECAY_SCALE = math.exp(-0.5)
CONV_K = 4
CONV_LEFT = 2
Q_SCALE = HEAD ** -0.5 * math.log2(math.e)
V_ONES = 16
RG_LANES = 256
WKV_CHUNK = 64
WKV_LANES = 256
WKV_GROUPS = 2

NT_DIMS = (((1,), (1,)), ((), ()))
TN_DIMS = (((0,), (0,)), ((), ()))


def _params(sem):
    return pltpu.CompilerParams(dimension_semantics=sem, vmem_limit_bytes=VMEM_LIMIT)


def _const_spec(shape):
    nd = len(shape)
    return pl.BlockSpec(shape, lambda *_: (0,) * nd, pipeline_mode=pl.Buffered(1))


def _dot(a, b):
    return jnp.dot(a, b, preferred_element_type=F32)


def _dot_exact(a, b, dims=None):
    if dims is None:
        return jnp.dot(a, b, preferred_element_type=F32, precision=lax.Precision.HIGHEST)
    return lax.dot_general(a, b, dims, preferred_element_type=F32, precision=lax.Precision.HIGHEST)


def _split(x):
    hi = x.astype(BF16)
    lo = (x - hi.astype(F32)).astype(BF16)
    return hi, lo


def _group_reduce(x, ones_ref):
    cw = ones_ref.shape[0]
    ones = ones_ref[...]
    outs = []
    for c in range(x.shape[1] // cw):
        hi, lo = _split(x[:, c * cw:(c + 1) * cw])
        outs.append(_dot(hi, ones) + _dot(lo, ones))
    return outs[0] if len(outs) == 1 else jnp.concatenate(outs, axis=1)


def _sigmoid(x):
    return 0.5 * jnp.tanh(0.5 * x) + 0.5


def _norm_mod(x, g, shift, scale):
    ms = jnp.mean(x * x, axis=-1, keepdims=True)
    return (x * lax.rsqrt(ms + EPS) * g) * (1.0 + scale) + shift


def _ada_kernel(c_ref, w_ref, b_ref, o_ref):
    c = c_ref[...]
    s = c * _sigmoid(c)
    o_ref[0, 0] = _dot_exact(s, w_ref[0]) + b_ref[0]


def _ada_mods(cc, ada_w, ada_b):
    depth, d, _ = ada_w.shape
    rows = cc.shape[0]
    out = pl.pallas_call(
        _ada_kernel,
        grid=(depth, 6),
        in_specs=[pl.BlockSpec((rows, d), lambda l, j: (0, 0)),
                  pl.BlockSpec((1, d, d), lambda l, j: (l, 0, j)),
                  pl.BlockSpec((1, 1, d), lambda l, j: (l * 6 + j, 0, 0))],
        out_specs=pl.BlockSpec((1, 1, rows, d), lambda l, j: (l, j, 0, 0)),
        out_shape=jax.ShapeDtypeStruct((depth, 6, rows, d), F32),
        compiler_params=_params(("arbitrary", "arbitrary")),
    )(cc, ada_w, ada_b.reshape(depth * 6, 1, d))
    return jnp.transpose(out, (0, 2, 1, 3))


def _hy_in_kernel(x_ref, mod_ref, g_ref, w_ref, cos_ref, sa_ref, sb_ref, qg_ref, kg_ref, oq_ref, ok_ref,
                  q_out, k_out, vt_out, xr_out, gl_out, *, q_w, kv_w, rnn_w):
    x = x_ref[0]
    h = _norm_mod(x, g_ref[...], mod_ref[0, 0:1, :], mod_ref[0, 1:2, :]).astype(BF16)
    z = _dot(h, w_ref[...])
    c0, c1, c2, c3 = q_w, q_w + kv_w, q_w + 2 * kv_w, q_w + 2 * kv_w + rnn_w

    def norm_rope(u, gain, ones_ref):
        width = u.shape[1]
        un = u * lax.rsqrt(_group_reduce(u * u, ones_ref) + EPS) * gain
        reps = width // LANES
        tile = lambda r: jnp.concatenate([r[...]] * reps, axis=1) if reps > 1 else r[...]
        nxt = pltpu.roll(un, width - 1, 1)
        prv = pltpu.roll(un, 1, 1)
        return un * tile(cos_ref) + nxt * tile(sa_ref) + prv * tile(sb_ref)

    q_out[0] = (norm_rope(z[:, :c0], qg_ref[...], oq_ref) * Q_SCALE).astype(BF16)
    k_out[0] = norm_rope(z[:, c0:c1], kg_ref[...], ok_ref).astype(BF16)
    vt = z[:, c1:c2].T
    ones = jnp.ones((V_ONES, TM), F32)
    for hk in range(kv_w // HEAD):
        vt_out[0, hk] = jnp.concatenate([vt[hk * HEAD:(hk + 1) * HEAD], ones], axis=0).astype(BF16)
    xr_out[0] = z[:, c2:c3]
    gl_out[0] = z[:, c3:]


def _attn_kernel(q_ref, k_ref, vt_ref, o_ref, ot_ref, *, ct, nt, groups, n_kv):
    t = pl.program_id(1)
    nkv = jnp.where(t < ct, ct, nt)
    q = q_ref[0].astype(F32)
    lane = lax.broadcasted_iota(jnp.int32, (TM, LANES), 1)
    low = lane < HEAD
    n_sub = TM // LANES
    blocks = []
    for head in range(groups * n_kv):
        hk = head // groups
        qc = q[:, LANES * (head // 2):LANES * (head // 2 + 1)]
        if head % 2 != hk % 2:
            qc = pltpu.roll(qc, HEAD, 1)
        qm = jnp.where(low if hk % 2 == 0 else jnp.logical_not(low), qc, 0.0).astype(BF16)
        blocks += [(hk, qm[s * LANES:(s + 1) * LANES]) for s in range(n_sub)]

    def body(i, carry):
        r0 = pl.multiple_of(i * TM, TM)
        kcs = [k_ref[0, pl.ds(r0, TM), LANES * (hk // 2):LANES * (hk // 2 + 1)] for hk in range(n_kv)]
        vts = [vt_ref[0, hk, :, pl.ds(r0, TM)] for hk in range(n_kv)]
        sts = [lax.dot_general(kcs[hk], qm, NT_DIMS, preferred_element_type=F32) for hk, qm in blocks]
        ms = [jnp.maximum(m, jnp.max(st, axis=0, keepdims=True)) for st, (m, _) in zip(sts, carry)]
        pvs = [_dot(vts[hk], jnp.exp2(st - m_new).astype(BF16)) for (hk, _), st, m_new in zip(blocks, sts, ms)]
        return tuple((m_new, jnp.exp2(m - m_new) * acc + pv) for m_new, pv, (m, acc) in zip(ms, pvs, carry))

    init = tuple((jnp.full((1, LANES), -1e30, F32), jnp.zeros((vt_ref.shape[2], LANES), F32)) for _ in blocks)
    res = lax.fori_loop(0, nkv, body, init)
    for idx, (_, acc) in enumerate(res):
        head, s = idx // n_sub, idx % n_sub
        ot_ref[HEAD * head:HEAD * (head + 1), s * LANES:(s + 1) * LANES] = acc[:HEAD] / acc[HEAD:HEAD + 1]
    o_ref[0] = ot_ref[...].T.astype(BF16)


def _scan8(a, u, reverse):
    row = lax.broadcasted_iota(jnp.int32, a.shape, 0)
    for d in (1, 2, 4):
        if reverse:
            a_s, u_s, ok = pltpu.roll(a, SUBLANES - d, 0), pltpu.roll(u, SUBLANES - d, 0), row < SUBLANES - d
        else:
            a_s, u_s, ok = pltpu.roll(a, d, 0), pltpu.roll(u, d, 0), row >= d
        u = a * jnp.where(ok, u_s, 0.0) + u
        a = a * jnp.where(ok, a_s, 1.0)
    return a, u


def _rglru_kernel(xr_ref, gl_ref, cw_ref, cb_ref, gw_ref, gb_ref, lam_ref, o_ref,
                  xs_ref, a_ref, u_ref, *, cl, n):
    t_all = cl + n
    pad = SUBLANES
    width = xs_ref.shape[1]
    xs_ref[0:pad, :] = jnp.zeros((pad, width), F32)
    xs_ref[pad + t_all:, :] = jnp.zeros((pad, width), F32)
    xs_ref[pad:pad + t_all, :] = xr_ref[0]
    lam = lam_ref[...]
    z = -lam
    softplus = jnp.maximum(z, 0.0) + jnp.log(1.0 + jnp.exp(-jnp.abs(z)))
    cw = cw_ref[...]
    cb = cb_ref[...]
    gb = gb_ref[...]

    def coeffs(i, _):
        r0 = pl.multiple_of(i * TM, TM)
        blk = xs_ref[pl.ds(r0, TM + 2 * pad), :]
        rows = r0 + lax.broadcasted_iota(jnp.int32, (TM, 1), 0)
        in_lat = rows >= cl
        pos = jnp.where(in_lat, rows - cl, rows)
        seqlen = jnp.where(in_lat, n, cl)
        xc = jnp.zeros((TM, width), F32) + cb
        for j in range(CONV_K):
            off = j - CONV_LEFT
            tap = blk[pad + off:pad + off + TM, :]
            ok = jnp.logical_and(pos + off >= 0, pos + off < seqlen)
            xc = xc + jnp.where(ok, tap, 0.0) * cw[j:j + 1, :]
        xcb = xc.astype(BF16)
        for d in range(2):
            r = _sigmoid(_dot(xcb, gw_ref[0, 2 * d]) + gb[2 * d:2 * d + 1, :])
            ig = _sigmoid(_dot(xcb, gw_ref[0, 2 * d + 1]) + gb[2 * d + 1:2 * d + 2, :])
            a = jnp.exp(-RG_C * r * softplus[d:d + 1, :])
            a_ref[d, pl.ds(r0, TM), :] = a
            u_ref[d, pl.ds(r0, TM), :] = jnp.sqrt(1.0 - a * a) * (ig * xc)
        return 0

    lax.fori_loop(0, t_all // TM, coeffs, 0)

    g_all, g_ctx = t_all // SUBLANES, cl // SUBLANES

    def step(i, carry):
        h_fwd, h_rev = carry
        g_rev = jnp.where(i < g_ctx, g_ctx - 1 - i, g_all - 1 - (i - g_ctx))
        sf = pl.ds(pl.multiple_of(i * SUBLANES, SUBLANES), SUBLANES)
        sr = pl.ds(pl.multiple_of(g_rev * SUBLANES, SUBLANES), SUBLANES)
        af, uf = _scan8(a_ref[0, sf, :], u_ref[0, sf, :], False)
        ar, ur = _scan8(a_ref[1, sr, :], u_ref[1, sr, :], True)
        hf = af * h_fwd + uf
        hr = ar * h_rev + ur
        u_ref[0, sf, :] = hf
        u_ref[1, sr, :] = hr
        return hf[SUBLANES - 1:SUBLANES, :], hr[0:1, :]

    zero = jnp.zeros((1, width), F32)
    lax.fori_loop(0, g_all, step, (zero, zero), unroll=4)

    def combine(i, _):
        r0 = pl.multiple_of(i * TM, TM)
        hsum = u_ref[0, pl.ds(r0, TM), :] + u_ref[1, pl.ds(r0, TM), :]
        o_ref[0, pl.ds(r0, TM), :] = (jax.nn.gelu(gl_ref[0, pl.ds(r0, TM), :]) * hsum).astype(BF16)
        return 0

    lax.fori_loop(0, t_all // TM, combine, 0)


def _rw_in_kernel(x_ref, xp_ref, xn_ref, mod_ref, g_ref, mu_ref, wrkv_ref, wdw_ref, wda_ref, wuw_ref, wua_ref,
                  lb_ref, gd_ref, gu_ref, kk_ref, ka_ref, rk_ref, gnb_ref, ones_ref,
                  r_out, kk_out, v_out, lw_out, bb_out, kd_out, g_out, z_out, *, ct, nt):
    t = pl.program_id(1)
    g = g_ref[...]
    shift, scale = mod_ref[0, 0:1, :], mod_ref[0, 1:2, :]
    h = _norm_mod(x_ref[0], g, shift, scale)
    first = jnp.logical_or(t == 0, t == ct)
    last = jnp.logical_or(t == ct - 1, t == nt - 1)
    hp = _norm_mod(xp_ref[0], g, shift, scale)[SUBLANES - 1:SUBLANES, :]
    hn = _norm_mod(xn_ref[0], g, shift, scale)[0:1, :]
    hp = jnp.where(first, 0.0, hp)
    hn = jnp.where(last, 0.0, hn)
    row = lax.broadcasted_iota(jnp.int32, (TM, 1), 0)
    h_prev = jnp.where(row == 0, hp, pltpu.roll(h, 1, 0))
    h_next = jnp.where(row == TM - 1, hn, pltpu.roll(h, TM - 1, 0))
    xx = 0.5 * (h_prev + h_next) - h
    lerp = lambda j: (h + xx * mu_ref[j:j + 1, :]).astype(BF16)

    r = _dot(lerp(0), wrkv_ref[0])
    k = _dot(lerp(2), wrkv_ref[1])
    v = _dot(lerp(3), wrkv_ref[2])
    gate = _dot(_sigmoid(_dot(lerp(5), gd_ref[...])).astype(BF16), gu_ref[...])
    tw = jnp.tanh(_dot(lerp(1), wdw_ref[...])).astype(BF16)
    ta = _dot(lerp(4), wda_ref[...]).astype(BF16)

    kk = k * kk_ref[...]
    nrm = jnp.sqrt(_group_reduce(kk * kk, ones_ref))
    kk = kk / jnp.maximum(nrm, 1e-12)
    r_out[0] = r
    kk_out[0] = kk
    v_out[0] = v
    kd_sum = jnp.zeros_like(k)
    for d in range(2):
        dec = lb_ref[2 * d:2 * d + 1, :] + _dot(tw, wuw_ref[d])
        a = _sigmoid(lb_ref[2 * d + 1:2 * d + 2, :] + _dot(ta, wua_ref[d]))
        kd = k * (1.0 + (a - 1.0) * ka_ref[...])
        lw_out[d, 0] = -DECAY_SCALE * _sigmoid(dec)
        bb_out[d, 0] = kk * a
        kd_out[d, 0] = kd
        kd_sum = kd_sum + kd
    bonus = _group_reduce(r * kd_sum * rk_ref[...], ones_ref)
    g_out[0] = gate
    z_out[0] = (gnb_ref[...] + bonus * v) * gate


def _wkv_kernel(r_ref, kk_ref, v_ref, lw_ref, bb_ref, kd_ref, y_ref, h_ref):
    c, lw_n = WKV_CHUNK, WKV_LANES
    n_heads = lw_n // HEAD
    s_rows = n_heads * c
    d = pl.program_id(0)
    i = pl.program_id(3)

    @pl.when(i == 0)
    def _():
        h_ref[...] = jnp.zeros_like(h_ref)

    rev = d == 1
    row = lax.broadcasted_iota(jnp.int32, (c, c), 0)
    col = lax.broadcasted_iota(jnp.int32, (c, c), 1)
    tri = (jnp.where(rev, col - row, row - col) >= 0).astype(BF16)
    srow = lax.broadcasted_iota(jnp.int32, (s_rows, s_rows), 0)
    scol = lax.broadcasted_iota(jnp.int32, (s_rows, s_rows), 1)
    ahead = jnp.where(rev, scol % c - srow % c, srow % c - scol % c)
    strict = ahead > 0
    incl = ahead >= 0
    eye = (srow == scol).astype(F32)
    same_block = lambda size: (srow // size) == (scol // size)
    lrow =lax.broadcasted_iota(jnp.int32, (lw_n, lw_n), 0)
    lcol = lax.broadcasted_iota(jnp.int32, (lw_n, lw_n), 1)
    same_head = (lrow // HEAD) == (lcol // HEAD)
    diag = lrow == lcol
    head_of_lane = lax.broadcasted_iota(jnp.int32, (s_rows, lw_n), 1) // HEAD
    head_of_row = lax.broadcasted_iota(jnp.int32, (s_rows, lw_n), 0) // c
    own = head_of_lane == head_of_row
    n_chunks = TM // c

    def stack(x):
        return jnp.where(own, jnp.concatenate([x] * n_heads, axis=0), 0.0)

    def unstack(xs):
        out = xs[0:c]
        for hh in range(1, n_heads):
            out = out + xs[hh * c:(hh + 1) * c]
        return out

    def local(grp, ci):
        cidx = jnp.where(rev, n_chunks - 1 - ci, ci)
        sl = pl.ds(pl.multiple_of(cidx * c, c), c)
        ls = slice(grp * lw_n, (grp + 1) * lw_n)
        r, kk, v = r_ref[0, sl, ls], kk_ref[0, sl, ls], v_ref[0, sl, ls]
        lw, bb, kd = lw_ref[0, 0, sl, ls], bb_ref[0, 0, sl, ls], kd_ref[0, 0, sl, ls]
        lw_hi, lw_rest = _split(lw)[0], lw - _split(lw)[0].astype(F32)
        lw_mid, lw_lo = _split(lw_rest)
        cum = _dot(tri, lw_hi) + (_dot(tri, lw_mid) + _dot(tri, lw_lo))
        yield
        tot = jnp.sum(lw, axis=0, keepdims=True)
        e_neg = jnp.exp(-cum)
        e_end = jnp.exp(tot - cum)
        kd_g, b_g = kd * e_end, bb * e_end
        kk_s = stack(kk * jnp.exp(cum - lw))
        r_s = stack(r * jnp.exp(cum))
        v_s = stack(v).astype(BF16)
        lhs = jnp.concatenate([kk_s, r_s], axis=0)
        yield
        p1 = _bdot(lhs, stack(kd * e_neg), NT_DIMS)
        yield
        p2 = _bdot(lhs, stack(bb * e_neg), NT_DIMS)
        yield
        a_kd = jnp.where(strict, p1[:s_rows], 0.0)
        b_kd = jnp.where(incl, p1[s_rows:], 0.0)
        nmat = jnp.where(strict, p2[:s_rows], 0.0)
        b_b = jnp.where(incl, p2[s_rows:], 0.0).astype(BF16)
        tinv = eye - jnp.where(same_block(2), nmat, 0.0)
        size = 2
        while size < c:
            n_off = jnp.where(jnp.logical_and(same_block(2 * size), jnp.logical_not(same_block(size))), nmat, 0.0)
            tb = tinv.astype(BF16)
            nt = _bdot(n_off, tb)
            yield
            tinv = tinv - _bdot(tb, nt)
            yield
            size *= 2
        av = _bdot(a_kd, v_s)
        yield
        tw = _bdot(tinv, jnp.concatenate([kk_s, av], axis=1))
        yield
        bw = _bdot(b_b, tw)
        yield
        kkp, u0 = unstack(tw[:, :lw_n]), unstack(tw[:, lw_n:])
        rp = unstack(r_s - bw[:, :lw_n])
        y0 = unstack(_bdot(b_kd, v_s) - bw[:, lw_n:])
        m_mat = jnp.where(diag, jnp.exp(tot), 0.0) - jnp.where(same_head, _bdot(b_g, kkp, TN_DIMS), 0.0)
        g_mat = jnp.where(same_head, _bdot(jnp.concatenate([kd_g, b_g], axis=0),
                                           jnp.concatenate([v, -u0], axis=0), TN_DIMS), 0.0)
        return grp, sl, ls, rp, y0, m_mat, g_mat

    chains = [local(grp, ci) for ci in range(n_chunks) for grp in range(h_ref.shape[0])]
    parts = [None] * len(chains)
    while any(p is None for p in parts):
        for idx, chain in enumerate(chains):
            if parts[idx] is None:
                try:
                    next(chain)
                except StopIteration as done:
                    parts[idx] = done.value
    hs = [h_ref[grp] for grp in range(h_ref.shape[0])]
    for grp, sl, ls, rp, y0, m_mat, g_mat in parts:
        y_ref[0, 0, sl, ls] = _bdot(rp, hs[grp]) + y0
        hs[grp] = _bdot(m_mat, hs[grp]) + g_mat
    for grp, h_new in enumerate(hs):
        h_ref[grp] = h_new


def _bdot(a, b, dims=(((1,), (0,)), ((), ()))):
    return lax.dot_general(a.astype(BF16), b.astype(BF16), dims, preferred_element_type=F32)


def _post_kernel(*refs, odd, ff_chunk):
    if odd:
        (x_ref, y0_ref, y1_ref, gate_ref, z_ref, gng_ref, ones_ref,
         wo_ref, mod_ref, g2_ref, w1_ref, w2_ref, o_ref) = refs
        y = y0_ref[0, 0] + y1_ref[0, 0]
        dlt = y - _group_reduce(y, ones_ref)
        var = _group_reduce(dlt * dlt, ones_ref)
        mix = (dlt * lax.rsqrt(var + GN_EPS) * gng_ref[...] * gate_ref[0] + z_ref[0]).astype(BF16)
    else:
        x_ref, att_ref, rec_ref, wo_ref, mod_ref, g2_ref, w1_ref, w2_ref, o_ref = refs
        mix = jnp.concatenate([att_ref[0], rec_ref[0]], axis=1)
    x1 = x_ref[0] + mod_ref[0, 2:3, :] * _dot(mix, wo_ref[...])
    h2 = _norm_mod(x1, g2_ref[...], mod_ref[0, 3:4, :], mod_ref[0, 4:5, :]).astype(BF16)
    acc = jnp.zeros_like(x1)
    for c in range(w1_ref.shape[1] // ff_chunk):
        a = _dot(h2, w1_ref[:, c * ff_chunk:(c + 1) * ff_chunk])
        a = jnp.square(jnp.maximum(a, 0.0)).astype(BF16)
        acc = acc + _dot(a, w2_ref[c * ff_chunk:(c + 1) * ff_chunk, :])
    o_ref[0] = x1 + mod_ref[0, 5:6, :] * acc


def _blockdiag_ones(width, value):
    idx = jnp.arange(width) // HEAD
    return jnp.where(idx[:, None] == idx[None, :], value, 0.0).astype(BF16)


def _rope_tables(cl, n, grid_w):
    rows = n // grid_w
    row = jnp.repeat(jnp.arange(rows, dtype=F32), grid_w)
    col = jnp.tile(jnp.arange(grid_w, dtype=F32), rows)
    half = HEAD // 2
    inv = ROPE_THETA ** (-jnp.arange(0, half, 2, dtype=F32) / half)
    ang = jnp.concatenate([row[:, None] * inv, col[:, None] * inv], axis=-1)
    cos = jnp.repeat(jnp.cos(ang), 2, axis=-1)
    sin = jnp.repeat(jnp.sin(ang), 2, axis=-1)
    even = (jnp.arange(HEAD) % 2 == 0)[None, :]
    sa = jnp.where(even, -sin, 0.0)
    sb = jnp.where(even, 0.0, sin)
    ctx = lambda fill: jnp.full((cl, HEAD), fill, F32)
    full = lambda lat, fill: jnp.tile(jnp.concatenate([ctx(fill), lat], axis=0), (1, LANES // HEAD))
    return full(cos, 1.0), full(sa, 0.0), full(sb, 0.0)


def _tile_specs(b_all, ct, d_model):
    mod_spec = pl.BlockSpec((1, 6, d_model), lambda b, t: (jnp.where(t < ct, b_all, b), 0, 0))
    row_spec = lambda w: pl.BlockSpec((1, TM, w), lambda b, t: (b, t, 0))
    return mod_spec, row_spec


def _hybrid_layer(xs, mods, g1, w_in, w_out, qn, kn, conv_w, conv_b, gate_w, gate_b, lam, ropes, dims):
    b_all, cl, n, d_model = dims
    t_all = cl + n
    ct, nt = cl // TM, t_all // TM
    in_w = w_in.shape[1]
    rnn_w = conv_w.shape[1]
    kv_w = (in_w - 2 * rnn_w - d_model // 2) // 2
    q_w = in_w - 2 * kv_w - 2 * rnn_w
    n_kv = kv_w // HEAD
    groups = q_w // kv_w
    mod_spec, row_spec = _tile_specs(b_all, ct, d_model)
    rope_spec = pl.BlockSpec((TM, LANES), lambda b, t: (t, 0))
    cos, sa, sb = ropes

    q, k, vt, xr, gl = pl.pallas_call(
        functools.partial(_hy_in_kernel, q_w=q_w, kv_w=kv_w, rnn_w=rnn_w),
        grid=(b_all, nt),
        in_specs=[row_spec(d_model), mod_spec, _const_spec((1, d_model)), _const_spec((d_model, in_w)),
                  rope_spec, rope_spec, rope_spec, _const_spec((1, q_w)), _const_spec((1, kv_w)),
                  _const_spec((MXU, MXU)), _const_spec((kv_w, kv_w))],
        out_specs=[row_spec(q_w), row_spec(kv_w),
                   pl.BlockSpec((1, n_kv, HEAD + V_ONES, TM), lambda b, t: (b, 0, 0, t)),
                   row_spec(rnn_w), row_spec(rnn_w)],
        out_shape=[jax.ShapeDtypeStruct((b_all, t_all, q_w), BF16),
                   jax.ShapeDtypeStruct((b_all, t_all, kv_w), BF16),
                   jax.ShapeDtypeStruct((b_all, n_kv, HEAD + V_ONES, t_all), BF16),
                   jax.ShapeDtypeStruct((b_all, t_all, rnn_w), F32),
                   jax.ShapeDtypeStruct((b_all, t_all, rnn_w), F32)],
        compiler_params=_params(("parallel", "parallel")),
    )(xs, mods, g1.reshape(1, d_model), w_in.astype(BF16), cos, sa, sb,
      jnp.tile(qn, q_w // HEAD).reshape(1, q_w), jnp.tile(kn, kv_w // HEAD).reshape(1, kv_w),
      _blockdiag_ones(MXU, 1.0 / HEAD), _blockdiag_ones(kv_w, 1.0 / HEAD))

    att = pl.pallas_call(
        functools.partial(_attn_kernel, ct=ct, nt=nt, groups=groups, n_kv=n_kv),
        grid=(b_all, nt),
        in_specs=[row_spec(q_w),
                  pl.BlockSpec((1, t_all, kv_w), lambda b, t: (b, 0, 0)),
                  pl.BlockSpec((1, n_kv, HEAD + V_ONES, t_all), lambda b, t: (b, 0, 0, 0))],
        out_specs=row_spec(q_w),
        out_shape=jax.ShapeDtypeStruct((b_all, t_all, q_w), BF16),
        scratch_shapes=[pltpu.VMEM((q_w, TM), F32)],
        compiler_params=_params(("parallel", "parallel")),
    )(q, k, vt)

    n_lc = rnn_w // RG_LANES
    per = RG_LANES // HEAD
    gw = gate_w.reshape(4, n_lc, per, HEAD, HEAD)
    eye = jnp.eye(per, dtype=F32)
    gw = jnp.einsum('gcpde,pq->cgpdqe', gw, eye).reshape(n_lc, 4, RG_LANES, RG_LANES).astype(BF16)
    lane_spec = lambda rows: pl.BlockSpec((rows, RG_LANES), lambda b, c: (0, c))
    seq_spec = pl.BlockSpec((1, t_all, RG_LANES), lambda b, c: (b, 0, c))
    rec = pl.pallas_call(
        functools.partial(_rglru_kernel, cl=cl, n=n),
        grid=(b_all, n_lc),
        in_specs=[seq_spec, seq_spec, lane_spec(CONV_K), lane_spec(1),
                  pl.BlockSpec((1, 4, RG_LANES, RG_LANES), lambda b, c: (c, 0, 0, 0)),
                  lane_spec(4), lane_spec(2)],
        out_specs=seq_spec,
        out_shape=jax.ShapeDtypeStruct((b_all, t_all, rnn_w), BF16),
        scratch_shapes=[pltpu.VMEM((t_all + 2 * SUBLANES, RG_LANES), F32),
                        pltpu.VMEM((2, t_all, RG_LANES), F32),
                        pltpu.VMEM((2, t_all, RG_LANES), F32)],
        compiler_params=_params(("parallel", "parallel")),
    )(xr, gl, conv_w, conv_b.reshape(1, rnn_w), gw, gate_b.reshape(4, rnn_w), lam)
    return (att, rec), w_out.astype(BF16)


def _rwkv_layer(xs, mods, g1, mu, w_rkv, lora_down, lora_up, lora_bias, gate_down, gate_up,
                k_k, k_a, r_k, gn_b, dims):
    b_all, cl, n, d_model = dims
    t_all = cl + n
    ct, nt = cl // TM, t_all // TM
    n8 = t_all // SUBLANES
    per8 = TM // SUBLANES
    lora = lora_down.shape[-1]
    glora = gate_down.shape[-1]
    mod_spec, row_spec = _tile_specs(b_all, ct, d_model)
    dir_spec = pl.BlockSpec((2, 1, TM, d_model), lambda b, t: (0, b, t, 0))
    wdw = jnp.concatenate([lora_down[0, 0], lora_down[1, 0]], axis=1).astype(BF16)
    wda = jnp.concatenate([lora_down[0, 1], lora_down[1, 1]], axis=1).astype(BF16)
    zeros = jnp.zeros((lora, d_model), F32)
    pad_up = lambda j: jnp.stack([jnp.concatenate([lora_up[0, j], zeros], axis=0),
                                  jnp.concatenate([zeros, lora_up[1, j]], axis=0)]).astype(BF16)
    vec = lambda a: a.reshape(1, d_model)
    outs = pl.pallas_call(
        functools.partial(_rw_in_kernel, ct=ct, nt=nt),
        grid=(b_all, nt),
        in_specs=[row_spec(d_model),
                  pl.BlockSpec((1, SUBLANES, d_model), lambda b, t: (b, jnp.maximum(t * per8 - 1, 0), 0)),
                  pl.BlockSpec((1, SUBLANES, d_model), lambda b, t: (b, jnp.minimum((t + 1) * per8, n8 - 1), 0)),
                  mod_spec, _const_spec((1, d_model)), _const_spec((6, d_model)),
                  _const_spec((3, d_model, d_model)), _const_spec((d_model, 2 * lora)),
                  _const_spec((d_model, 2 * lora)), _const_spec((2, 2 * lora, d_model)),
                  _const_spec((2, 2 * lora, d_model)), _const_spec((4, d_model)),
                  _const_spec((d_model, glora)), _const_spec((glora, d_model)),
                  _const_spec((1, d_model)), _const_spec((1, d_model)), _const_spec((1, d_model)),
                  _const_spec((1, d_model)), _const_spec((MXU, MXU))],
        out_specs=[row_spec(d_model), row_spec(d_model), row_spec(d_model),
                   dir_spec, dir_spec, dir_spec, row_spec(d_model), row_spec(d_model)],
        out_shape=[jax.ShapeDtypeStruct((b_all, t_all, d_model), F32)] * 3
        + [jax.ShapeDtypeStruct((2, b_all, t_all, d_model), F32)] * 3
        + [jax.ShapeDtypeStruct((b_all, t_all, d_model), F32)] * 2,
        compiler_params=_params(("parallel", "parallel")),
    )(xs, xs, xs, mods, vec(g1), mu, w_rkv.astype(BF16), wdw, wda, pad_up(0), pad_up(1),
      lora_bias.reshape(4, d_model), gate_down.astype(BF16), gate_up.astype(BF16),
      vec(k_k), vec(k_a), vec(r_k), vec(gn_b), _blockdiag_ones(MXU, 1.0))
    r, kk, v, lw, bb, kd, gate, z = outs

    def tmap(dd, i):
        rev = jnp.where(i < ct, ct - 1 - i, nt - 1 - (i - ct))
        return jnp.where(dd == 0, i, rev)

    blk_lanes = WKV_LANES * WKV_GROUPS
    shared = pl.BlockSpec((1, TM, blk_lanes), lambda dd, b, hh, i: (b, tmap(dd, i), hh))
    per_dir = pl.BlockSpec((1, 1, TM, blk_lanes), lambda dd, b, hh, i: (dd, b, tmap(dd, i), hh))
    y = pl.pallas_call(
        _wkv_kernel,
        grid=(2, b_all, d_model // blk_lanes, nt),
        in_specs=[shared, shared, shared, per_dir, per_dir, per_dir],
        out_specs=per_dir,
        out_shape=jax.ShapeDtypeStruct((2, b_all, t_all, d_model), F32),
        scratch_shapes=[pltpu.VMEM((WKV_GROUPS, WKV_LANES, WKV_LANES), F32)],
        compiler_params=_params(("arbitrary", "arbitrary", "arbitrary", "arbitrary")),
    )(r, kk, v, lw, bb, kd)
    return y, gate, z


def _post(xs, mix_inputs, w_o, mods, g2, w1, w2, dims, odd, gn_g=None, latent_only=False):
    b_all, cl, n, d_model = dims
    t_all = cl + n
    ct, nt = cl // TM, t_all // TM
    t0 = ct if latent_only else 0
    d_ff = w1.shape[1]
    mod_spec = pl.BlockSpec((1, 6, d_model), lambda b, t: (jnp.where(t + t0 < ct, b_all, b), 0, 0))
    row_spec = lambda w: pl.BlockSpec((1, TM, w), lambda b, t: (b, t + t0, 0))
    tail_specs = [_const_spec((d_model, d_model)), mod_spec, _const_spec((1, d_model)),
                  _const_spec((d_model, d_ff)), _const_spec((d_ff, d_model))]
    tail_args = (w_o, mods, g2.reshape(1, d_model), w1.astype(BF16), w2.astype(BF16))
    if odd:
        y, gate, z = mix_inputs
        y_spec = lambda dd: pl.BlockSpec((1, 1, TM, d_model), lambda b, t: (dd, b, t + t0, 0))
        in_specs = [row_spec(d_model), y_spec(0), y_spec(1), row_spec(d_model), row_spec(d_model),
                    _const_spec((1, d_model)), _const_spec((MXU, MXU))] + tail_specs
        args = (xs, y, y, gate, z, gn_g.reshape(1, d_model), _blockdiag_ones(MXU, 1.0 / HEAD)) + tail_args
    else:
        att, rec = mix_inputs
        in_specs = [row_spec(d_model), row_spec(att.shape[-1]), row_spec(rec.shape[-1])] + tail_specs
        args = (xs, att, rec) + tail_args
    rows_out = n if latent_only else t_all
    return pl.pallas_call(
        functools.partial(_post_kernel, odd=odd, ff_chunk=min(d_ff, 4 * MXU)),
        grid=(b_all, nt - t0),
        in_specs=in_specs,
        out_specs=pl.BlockSpec((1, TM, d_model), lambda b, t: (b, t, 0)),
        out_shape=jax.ShapeDtypeStruct((b_all, rows_out, d_model), F32),
        compiler_params=_params(("parallel", "parallel")),
    )(*args)


def kernel(x, c, ctx, c_ctx, ada_w, ada_b, norm_g, mlp_w1, mlp_w2, hy_w_in, hy_w_out, hy_q_norm, hy_k_norm, hy_conv_w, hy_conv_b, hy_gate_w, hy_gate_b, hy_lam, rw_mu, rw_w_rkv, rw_w_o, rw_lora_down, rw_lora_up, rw_lora_bias, rw_gate_down, rw_gate_up, rw_k_k, rw_k_a, rw_r_k, rw_gn_g, rw_gn_b):
    b_all, n, d_model = x.shape
    cl = ctx.shape[1]
    depth = ada_w.shape[0]
    assert cl % TM == 0 and n % TM == 0 and d_model % MXU == 0
    grid_w = 64
    dims = (b_all, cl, n, d_model)
    mods_all = _ada_mods(jnp.concatenate([c, c_ctx[None, :]], axis=0), ada_w, ada_b)
    ropes = _rope_tables(cl, n, grid_w)
    xs = jnp.concatenate([ctx, x], axis=1)
    for l in range(depth):
        i = l // 2
        mods = mods_all[l]
        last = l == depth - 1
        if l % 2 == 0:
            mix, w_o = _hybrid_layer(xs, mods, norm_g[l, 0], hy_w_in[i], hy_w_out[i], hy_q_norm[i], hy_k_norm[i],
                                     hy_conv_w[i], hy_conv_b[i], hy_gate_w[i], hy_gate_b[i], hy_lam[i], ropes, dims)
            xs = _post(xs, mix, w_o, mods, norm_g[l, 1], mlp_w1[l], mlp_w2[l], dims, odd=False, latent_only=last)
        else:
            mix = _rwkv_layer(xs, mods, norm_g[l, 0], rw_mu[i], rw_w_rkv[i], rw_lora_down[i], rw_lora_up[i],
                              rw_lora_bias[i], rw_gate_down[i], rw_gate_up[i], rw_k_k[i], rw_k_a[i],
                              rw_r_k[i], rw_gn_b[i], dims)
            xs = _post(xs, mix, rw_w_o[i].astype(BF16), mods, norm_g[l, 1], mlp_w1[l], mlp_w2[l], dims,
                       odd=True, gn_g=rw_gn_g[i], latent_only=last)
    return xs if xs.shape[1] == n else xs[:, cl:]
```

```python
import functools
import math

import jax
import jax.numpy as jnp
from jax import lax
from jax.experimental import pallas as pl
from jax.experimental.pallas import tpu as pltpu

F32 = jnp.float32
BF16 = jnp.bfloat16

HEAD = 64
LANES = 128
SUBLANES = 8
MXU = 256
TM = 256
VMEM_LIMIT = 56 * 1024 * 1024

EPS = 1e-6
GN_EPS = 64e-5
RG_C = 8.0
ROPE_THETA = 10000.0
DECAY_SCALE = math.exp(-0.5)
CONV_K = 4
CONV_LEFT = 2
Q_SCALE = HEAD ** -0.5 * math.log2(math.e)
V_ONES = 16
RG_LANES = 256
WKV_CHUNK = 64
WKV_LANES = 256
WKV_GROUPS = 2

NT_DIMS = (((1,), (1,)), ((), ()))
TN_DIMS = (((0,), (0,)), ((), ()))


def _params(sem):
    return pltpu.CompilerParams(dimension_semantics=sem, vmem_limit_bytes=VMEM_LIMIT)


def _const_spec(shape):
    nd = len(shape)
    return pl.BlockSpec(shape, lambda *_: (0,) * nd, pipeline_mode=pl.Buffered(1))


def _dot(a, b):
    return jnp.dot(a, b, preferred_element_type=F32)


def _dot_exact(a, b, dims=None):
    if dims is None:
        return jnp.dot(a, b, preferred_element_type=F32, precision=lax.Precision.HIGHEST)
    return lax.dot_general(a, b, dims, preferred_element_type=F32, precision=lax.Precision.HIGHEST)


def _split(x):
    hi = x.astype(BF16)
    lo = (x - hi.astype(F32)).astype(BF16)
    return hi, lo


def _group_reduce(x, ones_ref):
    cw = ones_ref.shape[0]
    ones = ones_ref[...]
    outs = []
    for c in range(x.shape[1] // cw):
        hi, lo = _split(x[:, c * cw:(c + 1) * cw])
        outs.append(_dot(hi, ones) + _dot(lo, ones))
    return outs[0] if len(outs) == 1 else jnp.concatenate(outs, axis=1)


def _sigmoid(x):
    return 0.5 * jnp.tanh(0.5 * x) + 0.5


def _norm_mod(x, g, shift, scale):
    ms = jnp.mean(x * x, axis=-1, keepdims=True)
    return (x * lax.rsqrt(ms + EPS) * g) * (1.0 + scale) + shift


def _ada_kernel(c_ref, w_ref, b_ref, o_ref):
    c = c_ref[...]
    s = c * _sigmoid(c)
    o_ref[0, 0] = _dot_exact(s, w_ref[0]) + b_ref[0]


def _ada_mods(cc, ada_w, ada_b):
    depth, d, _ = ada_w.shape
    rows = cc.shape[0]
    out = pl.pallas_call(
        _ada_kernel,
        grid=(depth, 6),
        in_specs=[pl.BlockSpec((rows, d), lambda l, j: (0, 0)),
                  pl.BlockSpec((1, d, d), lambda l, j: (l, 0, j)),
                  pl.BlockSpec((1, 1, d), lambda l, j: (l * 6 + j, 0, 0))],
        out_specs=pl.BlockSpec((1, 1, rows, d), lambda l, j: (l, j, 0, 0)),
        out_shape=jax.ShapeDtypeStruct((depth, 6, rows, d), F32),
        compiler_params=_params(("arbitrary", "arbitrary")),
    )(cc, ada_w, ada_b.reshape(depth * 6, 1, d))
    return jnp.transpose(out, (0, 2, 1, 3))


def _hy_in_kernel(x_ref, mod_ref, g_ref, w_ref, cos_ref, sa_ref, sb_ref, qg_ref, kg_ref, oq_ref, ok_ref,
                  q_out, k_out, vt_out, xr_out, gl_out, *, q_w, kv_w, rnn_w):
    x = x_ref[0]
    h = _norm_mod(x, g_ref[...], mod_ref[0, 0:1, :], mod_ref[0, 1:2, :]).astype(BF16)
    z = _dot(h, w_ref[...])
    c0, c1, c2, c3 = q_w, q_w + kv_w, q_w + 2 * kv_w, q_w + 2 * kv_w + rnn_w

    def norm_rope(u, gain, ones_ref):
        width = u.shape[1]
        un = u * lax.rsqrt(_group_reduce(u * u, ones_ref) + EPS) * gain
        reps = width // LANES
        tile = lambda r: jnp.concatenate([r[...]] * reps, axis=1) if reps > 1 else r[...]
        nxt = pltpu.roll(un, width - 1, 1)
        prv = pltpu.roll(un, 1, 1)
        return un * tile(cos_ref) + nxt * tile(sa_ref) + prv * tile(sb_ref)

    q_out[0] = (norm_rope(z[:, :c0], qg_ref[...], oq_ref) * Q_SCALE).astype(BF16)
    k_out[0] = norm_rope(z[:, c0:c1], kg_ref[...], ok_ref).astype(BF16)
    vt = z[:, c1:c2].T
    ones = jnp.ones((V_ONES, TM), F32)
    for hk in range(kv_w // HEAD):
        vt_out[0, hk] = jnp.concatenate([vt[hk * HEAD:(hk + 1) * HEAD], ones], axis=0).astype(BF16)
    xr_out[0] = z[:, c2:c3]
    gl_out[0] = z[:, c3:]


def _attn_kernel(q_ref, k_ref, vt_ref, o_ref, ot_ref, sa_ref, sb_ref, *, ct, nt, groups, n_kv):
    t = pl.program_id(1)
    nkv = jnp.where(t < ct, ct, nt)
    q = q_ref[0].astype(F32)
    lane = lax.broadcasted_iota(jnp.int32, (TM, LANES), 1)
    low = lane < HEAD
    blocks = []
    for head in range(groups * n_kv):
        hk = head // groups
        qc = q[:, LANES * (head // 2):LANES * (head // 2 + 1)]
        if head % 2 != hk % 2:
            qc = pltpu.roll(qc, HEAD, 1)
        blocks.append((hk, jnp.where(low if hk % 2 == 0 else jnp.logical_not(low), qc, 0.0).astype(BF16)))

    def scores(i, dst_ref):
        r0 = pl.multiple_of(i * TM, TM)
        kcs = [k_ref[0, pl.ds(r0, TM), LANES * (hk // 2):LANES * (hk // 2 + 1)] for hk in range(n_kv)]
        for head, (hk, qm) in enumerate(blocks):
            dst_ref[head] = lax.dot_general(kcs[hk], qm, NT_DIMS, preferred_element_type=F32)

    def absorb(i, src_ref, stats):
        vts = [vt_ref[0, hk, :, pl.ds(pl.multiple_of(i * TM, TM), TM)] for hk in range(n_kv)]
        sts = [src_ref[head] for head in range(len(blocks))]
        ms = [jnp.maximum(m, jnp.max(st, axis=0, keepdims=True)) for st, (m, _) in zip(sts, stats)]
        pvs = [_dot(vts[hk], jnp.exp2(st - m_new).astype(BF16)) for (hk, _), st, m_new in zip(blocks, sts, ms)]
        return tuple((m_new, jnp.exp2(m - m_new) * acc + pv) for m_new, pv, (m, acc) in zip(ms, pvs, stats))

    def pair(j, stats):
        scores(2 * j + 1, sb_ref)
        stats = absorb(2 * j, sa_ref, stats)
        scores(2 * j + 2, sa_ref)
        return absorb(2 * j + 1, sb_ref, stats)

    scores(0, sa_ref)
    init = tuple((jnp.full((1, TM), -1e30, F32), jnp.zeros((vt_ref.shape[2], TM), F32)) for _ in blocks)
    res = absorb(nkv - 1, sa_ref, lax.fori_loop(0, (nkv - 1) // 2, pair, init))
    for head, (_, acc) in enumerate(res):
        ot_ref[HEAD * head:HEAD * (head + 1), :] = acc[:HEAD] / acc[HEAD:HEAD + 1]
    o_ref[0] = ot_ref[...].T.astype(BF16)


def _scan8(a, u, reverse):
    row = lax.broadcasted_iota(jnp.int32, a.shape, 0)
    for d in (1, 2, 4):
        if reverse:
            a_s, u_s, ok = pltpu.roll(a, SUBLANES - d, 0), pltpu.roll(u, SUBLANES - d, 0), row < SUBLANES - d
        else:
            a_s, u_s, ok = pltpu.roll(a, d, 0), pltpu.roll(u, d, 0), row >= d
        u = a * jnp.where(ok, u_s, 0.0) + u
        a = a * jnp.where(ok, a_s, 1.0)
    return a, u


def _rglru_kernel(xr_ref, gl_ref, cw_ref, cb_ref, gw_ref, gb_ref, lam_ref, o_ref,
                  xs_ref, a_ref, u_ref, *, cl, n):
    t_all = cl + n
    pad = SUBLANES
    width = xs_ref.shape[1]
    xs_ref[0:pad, :] = jnp.zeros((pad, width), F32)
    xs_ref[pad + t_all:, :] = jnp.zeros((pad, width), F32)
    xs_ref[pad:pad + t_all, :] = xr_ref[0]
    lam = lam_ref[...]
    z = -lam
    softplus = jnp.maximum(z, 0.0) + jnp.log(1.0 + jnp.exp(-jnp.abs(z)))
    cw = cw_ref[...]
    cb = cb_ref[...]
    gb = gb_ref[...]

    def coeffs(i, _):
        r0 = pl.multiple_of(i * TM, TM)
        blk = xs_ref[pl.ds(r0, TM + 2 * pad), :]
        rows = r0 + lax.broadcasted_iota(jnp.int32, (TM, 1), 0)
        in_lat = rows >= cl
        pos = jnp.where(in_lat, rows - cl, rows)
        seqlen = jnp.where(in_lat, n, cl)
        xc = jnp.zeros((TM, width), F32) + cb
        for j in range(CONV_K):
            off = j - CONV_LEFT
            tap = blk[pad + off:pad + off + TM, :]
            ok = jnp.logical_and(pos + off >= 0, pos + off < seqlen)
            xc = xc + jnp.where(ok, tap, 0.0) * cw[j:j + 1, :]
        xcb = xc.astype(BF16)
        for d in range(2):
            r = _sigmoid(_dot(xcb, gw_ref[0, 2 * d]) + gb[2 * d:2 * d + 1, :])
            ig = _sigmoid(_dot(xcb, gw_ref[0, 2 * d + 1]) + gb[2 * d + 1:2 * d + 2, :])
            a = jnp.exp(-RG_C * r * softplus[d:d + 1, :])
            a_ref[d, pl.ds(r0, TM), :] = a
            u_ref[d, pl.ds(r0, TM), :] = jnp.sqrt(1.0 - a * a) * (ig * xc)
        return 0

    lax.fori_loop(0, t_all // TM, coeffs, 0)

    g_all, g_ctx = t_all // SUBLANES, cl // SUBLANES

    def step(i, carry):
        h_fwd, h_rev = carry
        g_rev = jnp.where(i < g_ctx, g_ctx - 1 - i, g_all - 1 - (i - g_ctx))
        sf = pl.ds(pl.multiple_of(i * SUBLANES, SUBLANES), SUBLANES)
        sr = pl.ds(pl.multiple_of(g_rev * SUBLANES, SUBLANES), SUBLANES)
        af, uf = _scan8(a_ref[0, sf, :], u_ref[0, sf, :], False)
        ar, ur = _scan8(a_ref[1, sr, :], u_ref[1, sr, :], True)
        hf = af * h_fwd + uf
        hr = ar * h_rev + ur
        u_ref[0, sf, :] = hf
        u_ref[1, sr, :] = hr
        return hf[SUBLANES - 1:SUBLANES, :], hr[0:1, :]

    zero = jnp.zeros((1, width), F32)
    lax.fori_loop(0, g_all, step, (zero, zero), unroll=4)

    def combine(i, _):
        r0 = pl.multiple_of(i * TM, TM)
        hsum = u_ref[0, pl.ds(r0, TM), :] + u_ref[1, pl.ds(r0, TM), :]
        o_ref[0, pl.ds(r0, TM), :] = (jax.nn.gelu(gl_ref[0, pl.ds(r0, TM), :]) * hsum).astype(BF16)
        return 0

    lax.fori_loop(0, t_all // TM, combine, 0)


def _rw_in_kernel(x_ref, xp_ref, xn_ref, mod_ref, g_ref, mu_ref, wrkv_ref, wdw_ref, wda_ref, wuw_ref, wua_ref,
                  lb_ref, gd_ref, gu_ref, kk_ref, ka_ref, rk_ref, gnb_ref, ones_ref,
                  r_out, kk_out, v_out, lw_out, bb_out, kd_out, g_out, z_out, *, ct, nt):
    t = pl.program_id(1)
    g = g_ref[...]
    shift, scale = mod_ref[0, 0:1, :], mod_ref[0, 1:2, :]
    h = _norm_mod(x_ref[0], g, shift, scale)
    first = jnp.logical_or(t == 0, t == ct)
    last = jnp.logical_or(t == ct - 1, t == nt - 1)
    hp = _norm_mod(xp_ref[0], g, shift, scale)[SUBLANES - 1:SUBLANES, :]
    hn = _norm_mod(xn_ref[0], g, shift, scale)[0:1, :]
    hp = jnp.where(first, 0.0, hp)
    hn = jnp.where(last, 0.0, hn)
    row = lax.broadcasted_iota(jnp.int32, (TM, 1), 0)
    h_prev = jnp.where(row == 0, hp, pltpu.roll(h, 1, 0))
    h_next = jnp.where(row == TM - 1, hn, pltpu.roll(h, TM - 1, 0))
    xx = 0.5 * (h_prev + h_next) - h
    lerp = lambda j: (h + xx * mu_ref[j:j + 1, :]).astype(BF16)

    r = _dot(lerp(0), wrkv_ref[0])
    k = _dot(lerp(2), wrkv_ref[1])
    v = _dot(lerp(3), wrkv_ref[2])
    gate = _dot(_sigmoid(_dot(lerp(5), gd_ref[...])).astype(BF16), gu_ref[...])
    tw = jnp.tanh(_dot(lerp(1), wdw_ref[...])).astype(BF16)
    ta = _dot(lerp(4), wda_ref[...]).astype(BF16)

    kk = k * kk_ref[...]
    nrm = jnp.sqrt(_group_reduce(kk * kk, ones_ref))
    kk = kk / jnp.maximum(nrm, 1e-12)
    r_out[0] = r
    kk_out[0] = kk
    v_out[0] = v
    kd_sum = jnp.zeros_like(k)
    for d in range(2):
        dec = lb_ref[2 * d:2 * d + 1, :] + _dot(tw, wuw_ref[d])
        a = _sigmoid(lb_ref[2 * d + 1:2 * d + 2, :] + _dot(ta, wua_ref[d]))
        kd = k * (1.0 + (a - 1.0) * ka_ref[...])
        lw_out[d, 0] = -DECAY_SCALE * _sigmoid(dec)
        bb_out[d, 0] = kk * a
        kd_out[d, 0] = kd
        kd_sum = kd_sum + kd
    bonus = _group_reduce(r * kd_sum * rk_ref[...], ones_ref)
    g_out[0] = gate
    z_out[0] = (gnb_ref[...] + bonus * v) * gate


def _wkv_kernel(r_ref, kk_ref, v_ref, lw_ref, bb_ref, kd_ref, y_ref, h_ref):
    c, lw_n = WKV_CHUNK, WKV_LANES
    n_heads = lw_n // HEAD
    s_rows = n_heads * c
    d = pl.program_id(0)
    i = pl.program_id(3)

    @pl.when(i == 0)
    def _():
        h_ref[...] = jnp.zeros_like(h_ref)

    rev = d == 1
    row = lax.broadcasted_iota(jnp.int32, (c, c), 0)
    col = lax.broadcasted_iota(jnp.int32, (c, c), 1)
    tri = (jnp.where(rev, col - row, row - col) >= 0).astype(BF16)
    srow = lax.broadcasted_iota(jnp.int32, (s_rows, s_rows), 0)
    scol = lax.broadcasted_iota(jnp.int32, (s_rows, s_rows), 1)
    ahead = jnp.where(rev, scol % c - srow % c, srow % c - scol % c)
    strict = ahead > 0
    incl = ahead >= 0
    eye = (srow == scol).astype(F32)
    same_block = lambda size: (srow // size) == (scol // size)
    lrow =lax.broadcasted_iota(jnp.int32, (lw_n, lw_n), 0)
    lcol = lax.broadcasted_iota(jnp.int32, (lw_n, lw_n), 1)
    same_head = (lrow // HEAD) == (lcol // HEAD)
    diag = lrow == lcol
    head_of_lane = lax.broadcasted_iota(jnp.int32, (s_rows, lw_n), 1) // HEAD
    head_of_row = lax.broadcasted_iota(jnp.int32, (s_rows, lw_n), 0) // c
    own = head_of_lane == head_of_row
    n_chunks = TM // c

    def stack(x):
        return jnp.where(own, jnp.concatenate([x] * n_heads, axis=0), 0.0)

    def unstack(xs):
        out = xs[0:c]
        for hh in range(1, n_heads):
            out = out + xs[hh * c:(hh + 1) * c]
        return out

    def local(grp, ci):
        cidx = jnp.where(rev, n_chunks - 1 - ci, ci)
        sl = pl.ds(pl.multiple_of(cidx * c, c), c)
        ls = slice(grp * lw_n, (grp + 1) * lw_n)
        r, kk, v = r_ref[0, sl, ls], kk_ref[0, sl, ls], v_ref[0, sl, ls]
        lw, bb, kd = lw_ref[0, 0, sl, ls], bb_ref[0, 0, sl, ls], kd_ref[0, 0, sl, ls]
        lw_hi, lw_rest = _split(lw)[0], lw - _split(lw)[0].astype(F32)
        lw_mid, lw_lo = _split(lw_rest)
        cum = _dot(tri, lw_hi) + (_dot(tri, lw_mid) + _dot(tri, lw_lo))
        yield
        tot = jnp.sum(lw, axis=0, keepdims=True)
        e_neg = jnp.exp(-cum)
        e_end = jnp.exp(tot - cum)
        kd_g, b_g = kd * e_end, bb * e_end
        kk_s = stack(kk * jnp.exp(cum - lw))
        r_s = stack(r * jnp.exp(cum))
        v_s = stack(v).astype(BF16)
        lhs = jnp.concatenate([kk_s, r_s], axis=0)
        yield
        p1 = _bdot(lhs, stack(kd * e_neg), NT_DIMS)
        yield
        p2 = _bdot(lhs, stack(bb * e_neg), NT_DIMS)
        yield
        a_kd = jnp.where(strict, p1[:s_rows], 0.0)
        b_kd = jnp.where(incl, p1[s_rows:], 0.0)
        nmat = jnp.where(strict, p2[:s_rows], 0.0)
        b_b = jnp.where(incl, p2[s_rows:], 0.0).astype(BF16)
        tinv = eye - jnp.where(same_block(2), nmat, 0.0)
        size = 2
        while size < c:
            n_off = jnp.where(jnp.logical_and(same_block(2 * size), jnp.logical_not(same_block(size))), nmat, 0.0)
            tb = tinv.astype(BF16)
            nt = _bdot(n_off, tb)
            yield
            tinv = tinv - _bdot(tb, nt)
            yield
            size *= 2
        av = _bdot(a_kd, v_s)
        yield
        tw = _bdot(tinv, jnp.concatenate([kk_s, av], axis=1))
        yield
        bw = _bdot(b_b, tw)
        yield
        kkp, u0 = unstack(tw[:, :lw_n]), unstack(tw[:, lw_n:])
        rp = unstack(r_s - bw[:, :lw_n])
        y0 = unstack(_bdot(b_kd, v_s) - bw[:, lw_n:])
        m_mat = jnp.where(diag, jnp.exp(tot), 0.0) - jnp.where(same_head, _bdot(b_g, kkp, TN_DIMS), 0.0)
        g_mat = jnp.where(same_head, _bdot(jnp.concatenate([kd_g, b_g], axis=0),
                                           jnp.concatenate([v, -u0], axis=0), TN_DIMS), 0.0)
        return grp, sl, ls, rp, y0, m_mat, g_mat

    chains = [local(grp, ci) for ci in range(n_chunks) for grp in range(h_ref.shape[0])]
    parts = [None] * len(chains)
    while any(p is None for p in parts):
        for idx, chain in enumerate(chains):
            if parts[idx] is None:
                try:
                    next(chain)
                except StopIteration as done:
                    parts[idx] = done.value
    hs = [h_ref[grp] for grp in range(h_ref.shape[0])]
    for grp, sl, ls, rp, y0, m_mat, g_mat in parts:
        y_ref[0, 0, sl, ls] = _bdot(rp, hs[grp]) + y0
        hs[grp] = _bdot(m_mat, hs[grp]) + g_mat
    for grp, h_new in enumerate(hs):
        h_ref[grp] = h_new


def _bdot(a, b, dims=(((1,), (0,)), ((), ()))):
    return lax.dot_general(a.astype(BF16), b.astype(BF16), dims, preferred_element_type=F32)


def _post_kernel(*refs, odd, ff_chunk):
    if odd:
        (x_ref, y0_ref, y1_ref, gate_ref, z_ref, gng_ref, ones_ref,
         wo_ref, mod_ref, g2_ref, w1_ref, w2_ref, o_ref) = refs
        y = y0_ref[0, 0] + y1_ref[0, 0]
        dlt = y - _group_reduce(y, ones_ref)
        var = _group_reduce(dlt * dlt, ones_ref)
        mix = (dlt * lax.rsqrt(var + GN_EPS) * gng_ref[...] * gate_ref[0] + z_ref[0]).astype(BF16)
    else:
        x_ref, att_ref, rec_ref, wo_ref, mod_ref, g2_ref, w1_ref, w2_ref, o_ref = refs
        mix = jnp.concatenate([att_ref[0], rec_ref[0]], axis=1)
    x1 = x_ref[0] + mod_ref[0, 2:3, :] * _dot(mix, wo_ref[...])
    h2 = _norm_mod(x1, g2_ref[...], mod_ref[0, 3:4, :], mod_ref[0, 4:5, :]).astype(BF16)
    acc = jnp.zeros_like(x1)
    for c in range(w1_ref.shape[1] // ff_chunk):
        a = _dot(h2, w1_ref[:, c * ff_chunk:(c + 1) * ff_chunk])
        a = jnp.square(jnp.maximum(a, 0.0)).astype(BF16)
        acc = acc + _dot(a, w2_ref[c * ff_chunk:(c + 1) * ff_chunk, :])
    o_ref[0] = x1 + mod_ref[0, 5:6, :] * acc


def _blockdiag_ones(width, value):
    idx = jnp.arange(width) // HEAD
    return jnp.where(idx[:, None] == idx[None, :], value, 0.0).astype(BF16)


def _rope_tables(cl, n, grid_w):
    rows = n // grid_w
    row = jnp.repeat(jnp.arange(rows, dtype=F32), grid_w)
    col = jnp.tile(jnp.arange(grid_w, dtype=F32), rows)
    half = HEAD // 2
    inv = ROPE_THETA ** (-jnp.arange(0, half, 2, dtype=F32) / half)
    ang = jnp.concatenate([row[:, None] * inv, col[:, None] * inv], axis=-1)
    cos = jnp.repeat(jnp.cos(ang), 2, axis=-1)
    sin = jnp.repeat(jnp.sin(ang), 2, axis=-1)
    even = (jnp.arange(HEAD) % 2 == 0)[None, :]
    sa = jnp.where(even, -sin, 0.0)
    sb = jnp.where(even, 0.0, sin)
    ctx = lambda fill: jnp.full((cl, HEAD), fill, F32)
    full = lambda lat, fill: jnp.tile(jnp.concatenate([ctx(fill), lat], axis=0), (1, LANES // HEAD))
    return full(cos, 1.0), full(sa, 0.0), full(sb, 0.0)


def _tile_specs(b_all, ct, d_model):
    mod_spec = pl.BlockSpec((1, 6, d_model), lambda b, t: (jnp.where(t < ct, b_all, b), 0, 0))
    row_spec = lambda w: pl.BlockSpec((1, TM, w), lambda b, t: (b, t, 0))
    return mod_spec, row_spec


def _hybrid_layer(xs, mods, g1, w_in, w_out, qn, kn, conv_w, conv_b, gate_w, gate_b, lam, ropes, dims):
    b_all, cl, n, d_model = dims
    t_all = cl + n
    ct, nt = cl // TM, t_all // TM
    in_w = w_in.shape[1]
    rnn_w = conv_w.shape[1]
    kv_w = (in_w - 2 * rnn_w - d_model // 2) // 2
    q_w = in_w - 2 * kv_w - 2 * rnn_w
    n_kv = kv_w // HEAD
    assert ct % 2 == 1 and nt % 2 == 1, "the attention kernel walks key tiles in pairs plus one"
    groups = q_w // kv_w
    mod_spec, row_spec = _tile_specs(b_all, ct, d_model)
    rope_spec = pl.BlockSpec((TM, LANES), lambda b, t: (t, 0))
    cos, sa, sb = ropes

    q, k, vt, xr, gl = pl.pallas_call(
        functools.partial(_hy_in_kernel, q_w=q_w, kv_w=kv_w, rnn_w=rnn_w),
        grid=(b_all, nt),
        in_specs=[row_spec(d_model), mod_spec, _const_spec((1, d_model)), _const_spec((d_model, in_w)),
                  rope_spec, rope_spec, rope_spec, _const_spec((1, q_w)), _const_spec((1, kv_w)),
                  _const_spec((MXU, MXU)), _const_spec((kv_w, kv_w))],
        out_specs=[row_spec(q_w), row_spec(kv_w),
                   pl.BlockSpec((1, n_kv, HEAD + V_ONES, TM), lambda b, t: (b, 0, 0, t)),
                   row_spec(rnn_w), row_spec(rnn_w)],
        out_shape=[jax.ShapeDtypeStruct((b_all, t_all, q_w), BF16),
                   jax.ShapeDtypeStruct((b_all, t_all, kv_w), BF16),
                   jax.ShapeDtypeStruct((b_all, n_kv, HEAD + V_ONES, t_all), BF16),
                   jax.ShapeDtypeStruct((b_all, t_all, rnn_w), F32),
                   jax.ShapeDtypeStruct((b_all, t_all, rnn_w), F32)],
        compiler_params=_params(("parallel", "parallel")),
    )(xs, mods, g1.reshape(1, d_model), w_in.astype(BF16), cos, sa, sb,
      jnp.tile(qn, q_w // HEAD).reshape(1, q_w), jnp.tile(kn, kv_w // HEAD).reshape(1, kv_w),
      _blockdiag_ones(MXU, 1.0 / HEAD), _blockdiag_ones(kv_w, 1.0 / HEAD))

    att = pl.pallas_call(
        functools.partial(_attn_kernel, ct=ct, nt=nt, groups=groups, n_kv=n_kv),
        grid=(b_all, nt),
        in_specs=[row_spec(q_w),
                  pl.BlockSpec((1, t_all, kv_w), lambda b, t: (b, 0, 0)),
                  pl.BlockSpec((1, n_kv, HEAD + V_ONES, t_all), lambda b, t: (b, 0, 0, 0))],
        out_specs=row_spec(q_w),
        out_shape=jax.ShapeDtypeStruct((b_all, t_all, q_w), BF16),
        scratch_shapes=[pltpu.VMEM((q_w, TM), F32)] + [pltpu.VMEM((q_w // HEAD, TM, TM), F32)] * 2,
        compiler_params=_params(("parallel", "parallel")),
    )(q, k, vt)

    n_lc = rnn_w // RG_LANES
    per = RG_LANES // HEAD
    gw = gate_w.reshape(4, n_lc, per, HEAD, HEAD)
    eye = jnp.eye(per, dtype=F32)
    gw = jnp.einsum('gcpde,pq->cgpdqe', gw, eye).reshape(n_lc, 4, RG_LANES, RG_LANES).astype(BF16)
    lane_spec = lambda rows: pl.BlockSpec((rows, RG_LANES), lambda b, c: (0, c))
    seq_spec = pl.BlockSpec((1, t_all, RG_LANES), lambda b, c: (b, 0, c))
    rec = pl.pallas_call(
        functools.partial(_rglru_kernel, cl=cl, n=n),
        grid=(b_all, n_lc),
        in_specs=[seq_spec, seq_spec, lane_spec(CONV_K), lane_spec(1),
                  pl.BlockSpec((1, 4, RG_LANES, RG_LANES), lambda b, c: (c, 0, 0, 0)),
                  lane_spec(4), lane_spec(2)],
        out_specs=seq_spec,
        out_shape=jax.ShapeDtypeStruct((b_all, t_all, rnn_w), BF16),
        scratch_shapes=[pltpu.VMEM((t_all + 2 * SUBLANES, RG_LANES), F32),
                        pltpu.VMEM((2, t_all, RG_LANES), F32),
                        pltpu.VMEM((2, t_all, RG_LANES), F32)],
        compiler_params=_params(("parallel", "parallel")),
    )(xr, gl, conv_w, conv_b.reshape(1, rnn_w), gw, gate_b.reshape(4, rnn_w), lam)
    return (att, rec), w_out.astype(BF16)


def _rwkv_layer(xs, mods, g1, mu, w_rkv, lora_down, lora_up, lora_bias, gate_down, gate_up,
                k_k, k_a, r_k, gn_b, dims):
    b_all, cl, n, d_model = dims
    t_all = cl + n
    ct, nt = cl // TM, t_all // TM
    n8 = t_all // SUBLANES
    per8 = TM // SUBLANES
    lora = lora_down.shape[-1]
    glora = gate_down.shape[-1]
    mod_spec, row_spec = _tile_specs(b_all, ct, d_model)
    dir_spec = pl.BlockSpec((2, 1, TM, d_model), lambda b, t: (0, b, t, 0))
    wdw = jnp.concatenate([lora_down[0, 0], lora_down[1, 0]], axis=1).astype(BF16)
    wda = jnp.concatenate([lora_down[0, 1], lora_down[1, 1]], axis=1).astype(BF16)
    zeros = jnp.zeros((lora, d_model), F32)
    pad_up = lambda j: jnp.stack([jnp.concatenate([lora_up[0, j], zeros], axis=0),
                                  jnp.concatenate([zeros, lora_up[1, j]], axis=0)]).astype(BF16)
    vec = lambda a: a.reshape(1, d_model)
    outs = pl.pallas_call(
        functools.partial(_rw_in_kernel, ct=ct, nt=nt),
        grid=(b_all, nt),
        in_specs=[row_spec(d_model),
                  pl.BlockSpec((1, SUBLANES, d_model), lambda b, t: (b, jnp.maximum(t * per8 - 1, 0), 0)),
                  pl.BlockSpec((1, SUBLANES, d_model), lambda b, t: (b, jnp.minimum((t + 1) * per8, n8 - 1), 0)),
                  mod_spec, _const_spec((1, d_model)), _const_spec((6, d_model)),
                  _const_spec((3, d_model, d_model)), _const_spec((d_model, 2 * lora)),
                  _const_spec((d_model, 2 * lora)), _const_spec((2, 2 * lora, d_model)),
                  _const_spec((2, 2 * lora, d_model)), _const_spec((4, d_model)),
                  _const_spec((d_model, glora)), _const_spec((glora, d_model)),
                  _const_spec((1, d_model)), _const_spec((1, d_model)), _const_spec((1, d_model)),
                  _const_spec((1, d_model)), _const_spec((MXU, MXU))],
        out_specs=[row_spec(d_model), row_spec(d_model), row_spec(d_model),
                   dir_spec, dir_spec, dir_spec, row_spec(d_model), row_spec(d_model)],
        out_shape=[jax.ShapeDtypeStruct((b_all, t_all, d_model), F32)] * 3
        + [jax.ShapeDtypeStruct((2, b_all, t_all, d_model), F32)] * 3
        + [jax.ShapeDtypeStruct((b_all, t_all, d_model), F32)] * 2,
        compiler_params=_params(("parallel", "parallel")),
    )(xs, xs, xs, mods, vec(g1), mu, w_rkv.astype(BF16), wdw, wda, pad_up(0), pad_up(1),
      lora_bias.reshape(4, d_model), gate_down.astype(BF16), gate_up.astype(BF16),
      vec(k_k), vec(k_a), vec(r_k), vec(gn_b), _blockdiag_ones(MXU, 1.0))
    r, kk, v, lw, bb, kd, gate, z = outs

    def tmap(dd, i):
        rev = jnp.where(i < ct, ct - 1 - i, nt - 1 - (i - ct))
        return jnp.where(dd == 0, i, rev)

    blk_lanes = WKV_LANES * WKV_GROUPS
    shared = pl.BlockSpec((1, TM, blk_lanes), lambda dd, b, hh, i: (b, tmap(dd, i), hh))
    per_dir = pl.BlockSpec((1, 1, TM, blk_lanes), lambda dd, b, hh, i: (dd, b, tmap(dd, i), hh))
    y = pl.pallas_call(
        _wkv_kernel,
        grid=(2, b_all, d_model // blk_lanes, nt),
        in_specs=[shared, shared, shared, per_dir, per_dir, per_dir],
        out_specs=per_dir,
        out_shape=jax.ShapeDtypeStruct((2, b_all, t_all, d_model), F32),
        scratch_shapes=[pltpu.VMEM((WKV_GROUPS, WKV_LANES, WKV_LANES), F32)],
        compiler_params=_params(("arbitrary", "arbitrary", "arbitrary", "arbitrary")),
    )(r, kk, v, lw, bb, kd)
    return y, gate, z


def _post(xs, mix_inputs, w_o, mods, g2, w1, w2, dims, odd, gn_g=None, latent_only=False):
    b_all, cl, n, d_model = dims
    t_all = cl + n
    ct, nt = cl // TM, t_all // TM
    t0 = ct if latent_only else 0
    d_ff = w1.shape[1]
    mod_spec = pl.BlockSpec((1, 6, d_model), lambda b, t: (jnp.where(t + t0 < ct, b_all, b), 0, 0))
    row_spec = lambda w: pl.BlockSpec((1, TM, w), lambda b, t: (b, t + t0, 0))
    tail_specs = [_const_spec((d_model, d_model)), mod_spec, _const_spec((1, d_model)),
                  _const_spec((d_model, d_ff)), _const_spec((d_ff, d_model))]
    tail_args = (w_o, mods, g2.reshape(1, d_model), w1.astype(BF16), w2.astype(BF16))
    if odd:
        y, gate, z = mix_inputs
        y_spec = lambda dd: pl.BlockSpec((1, 1, TM, d_model), lambda b, t: (dd, b, t + t0, 0))
        in_specs = [row_spec(d_model), y_spec(0), y_spec(1), row_spec(d_model), row_spec(d_model),
                    _const_spec((1, d_model)), _const_spec((MXU, MXU))] + tail_specs
        args = (xs, y, y, gate, z, gn_g.reshape(1, d_model), _blockdiag_ones(MXU, 1.0 / HEAD)) + tail_args
    else:
        att, rec = mix_inputs
        in_specs = [row_spec(d_model), row_spec(att.shape[-1]), row_spec(rec.shape[-1])] + tail_specs
        args = (xs, att, rec) + tail_args
    rows_out = n if latent_only else t_all
    return pl.pallas_call(
        functools.partial(_post_kernel, odd=odd, ff_chunk=min(d_ff, 4 * MXU)),
        grid=(b_all, nt - t0),
        in_specs=in_specs,
        out_specs=pl.BlockSpec((1, TM, d_model), lambda b, t: (b, t, 0)),
        out_shape=jax.ShapeDtypeStruct((b_all, rows_out, d_model), F32),
        compiler_params=_params(("parallel", "parallel")),
    )(*args)


def kernel(x, c, ctx, c_ctx, ada_w, ada_b, norm_g, mlp_w1, mlp_w2, hy_w_in, hy_w_out, hy_q_norm, hy_k_norm, hy_conv_w, hy_conv_b, hy_gate_w, hy_gate_b, hy_lam, rw_mu, rw_w_rkv, rw_w_o, rw_lora_down, rw_lora_up, rw_lora_bias, rw_gate_down, rw_gate_up, rw_k_k, rw_k_a, rw_r_k, rw_gn_g, rw_gn_b):
    b_all, n, d_model = x.shape
    cl = ctx.shape[1]
    depth = ada_w.shape[0]
    assert cl % TM == 0 and n % TM == 0 and d_model % MXU == 0
    grid_w = 64
    dims = (b_all, cl, n, d_model)
    mods_all = _ada_mods(jnp.concatenate([c, c_ctx[None, :]], axis=0), ada_w, ada_b)
    ropes = _rope_tables(cl, n, grid_w)
    xs = jnp.concatenate([ctx, x], axis=1)
    for l in range(depth):
        i = l // 2
        mods = mods_all[l]
        last = l == depth - 1
        if l % 2 == 0:
            mix, w_o = _hybrid_layer(xs, mods, norm_g[l, 0], hy_w_in[i], hy_w_out[i], hy_q_norm[i], hy_k_norm[i],
                                     hy_conv_w[i], hy_conv_b[i], hy_gate_w[i], hy_gate_b[i], hy_lam[i], ropes, dims)
            xs = _post(xs, mix, w_o, mods, norm_g[l, 1], mlp_w1[l], mlp_w2[l], dims, odd=False, latent_only=last)
        else:
            mix = _rwkv_layer(xs, mods, norm_g[l, 0], rw_mu[i], rw_w_rkv[i], rw_lora_down[i], rw_lora_up[i],
                              rw_lora_bias[i], rw_gate_down[i], rw_gate_up[i], rw_k_k[i], rw_k_a[i],
                              rw_r_k[i], rw_gn_b[i], dims)
            xs = _post(xs, mix, rw_w_o[i].astype(BF16), mods, norm_g[l, 1], mlp_w1[l], mlp_w2[l], dims,
                       odd=True, gn_g=rw_gn_g[i], latent_only=last)
    return xs if xs.shape[1] == n else xs[:, cl:]
```

```python
import functools
import math

import jax
import jax.numpy as jnp
from jax import lax
from jax.experimental import pallas as pl
from jax.experimental.pallas import tpu as pltpu

F32 = jnp.float32
BF16 = jnp.bfloat16

HEAD = 64
LANES = 128
SUBLANES = 8
MXU = 256
TM = 256
VMEM_LIMIT = 56 * 1024 * 1024

EPS = 1e-6
GN_EPS = 64e-5
RG_C = 8.0
ROPE_THETA = 10000.0
DECAY_SCALE = math.exp(-0.5)
CONV_K = 4
CONV_LEFT = 2
Q_SCALE = HEAD ** -0.5 * math.log2(math.e)
V_ONES = 16
RG_LANES = 256
WKV_CHUNK = 64
WKV_LANES = 256
WKV_GROUPS = 4

NT_DIMS = (((1,), (1,)), ((), ()))
TN_DIMS = (((0,), (0,)), ((), ()))


def _params(sem):
    return pltpu.CompilerParams(dimension_semantics=sem, vmem_limit_bytes=VMEM_LIMIT)


def _const_spec(shape):
    nd = len(shape)
    return pl.BlockSpec(shape, lambda *_: (0,) * nd, pipeline_mode=pl.Buffered(1))


def _dot(a, b):
    return jnp.dot(a, b, preferred_element_type=F32)


def _dot_exact(a, b, dims=None):
    if dims is None:
        return jnp.dot(a, b, preferred_element_type=F32, precision=lax.Precision.HIGHEST)
    return lax.dot_general(a, b, dims, preferred_element_type=F32, precision=lax.Precision.HIGHEST)


def _split(x):
    hi = x.astype(BF16)
    lo = (x - hi.astype(F32)).astype(BF16)
    return hi, lo


def _group_reduce(x, ones_ref):
    cw = ones_ref.shape[0]
    ones = ones_ref[...]
    outs = []
    for c in range(x.shape[1] // cw):
        hi, lo = _split(x[:, c * cw:(c + 1) * cw])
        outs.append(_dot(hi, ones) + _dot(lo, ones))
    return outs[0] if len(outs) == 1 else jnp.concatenate(outs, axis=1)


def _sigmoid(x):
    return 0.5 * jnp.tanh(0.5 * x) + 0.5


def _norm_mod(x, g, shift, scale):
    ms = jnp.mean(x * x, axis=-1, keepdims=True)
    return (x * lax.rsqrt(ms + EPS) * g) * (1.0 + scale) + shift


def _ada_kernel(c_ref, w_ref, b_ref, o_ref):
    c = c_ref[...]
    s = c * _sigmoid(c)
    o_ref[0, 0] = _dot_exact(s, w_ref[0]) + b_ref[0]


def _ada_mods(cc, ada_w, ada_b):
    depth, d, _ = ada_w.shape
    rows = cc.shape[0]
    out = pl.pallas_call(
        _ada_kernel,
        grid=(depth, 6),
        in_specs=[pl.BlockSpec((rows, d), lambda l, j: (0, 0)),
                  pl.BlockSpec((1, d, d), lambda l, j: (l, 0, j)),
                  pl.BlockSpec((1, 1, d), lambda l, j: (l * 6 + j, 0, 0))],
        out_specs=pl.BlockSpec((1, 1, rows, d), lambda l, j: (l, j, 0, 0)),
        out_shape=jax.ShapeDtypeStruct((depth, 6, rows, d), F32),
        compiler_params=_params(("arbitrary", "arbitrary")),
    )(cc, ada_w, ada_b.reshape(depth * 6, 1, d))
    return jnp.transpose(out, (0, 2, 1, 3))


def _hy_in_kernel(x_ref, mod_ref, g_ref, w_ref, cos_ref, sa_ref, sb_ref, qg_ref, kg_ref, oq_ref, ok_ref,
                  q_out, k_out, vt_out, xr_out, gl_out, *, q_w, kv_w, rnn_w):
    x = x_ref[0]
    h = _norm_mod(x, g_ref[...], mod_ref[0, 0:1, :], mod_ref[0, 1:2, :]).astype(BF16)
    z = _dot(h, w_ref[...])
    c0, c1, c2, c3 = q_w, q_w + kv_w, q_w + 2 * kv_w, q_w + 2 * kv_w + rnn_w

    def norm_rope(u, gain, ones_ref):
        width = u.shape[1]
        un = u * lax.rsqrt(_group_reduce(u * u, ones_ref) + EPS) * gain
        reps = width // LANES
        tile = lambda r: jnp.concatenate([r[...]] * reps, axis=1) if reps > 1 else r[...]
        nxt = pltpu.roll(un, width - 1, 1)
        prv = pltpu.roll(un, 1, 1)
        return un * tile(cos_ref) + nxt * tile(sa_ref) + prv * tile(sb_ref)

    q_out[0] = (norm_rope(z[:, :c0], qg_ref[...], oq_ref) * Q_SCALE).astype(BF16)
    k_out[0] = norm_rope(z[:, c0:c1], kg_ref[...], ok_ref).astype(BF16)
    vt = z[:, c1:c2].T
    ones = jnp.ones((V_ONES, TM), F32)
    for hk in range(kv_w // HEAD):
        vt_out[0, hk] = jnp.concatenate([vt[hk * HEAD:(hk + 1) * HEAD], ones], axis=0).astype(BF16)
    xr_out[0] = z[:, c2:c3]
    gl_out[0] = z[:, c3:]


def _attn_kernel(q_ref, k_ref, vt_ref, o_ref, ot_ref, sa_ref, sb_ref, *, ct, nt, groups, n_kv):
    t = pl.program_id(1)
    nkv = jnp.where(t < ct, ct, nt)
    q = q_ref[0].astype(F32)
    lane = lax.broadcasted_iota(jnp.int32, (TM, LANES), 1)
    low = lane < HEAD
    blocks = []
    for head in range(groups * n_kv):
        hk = head // groups
        qc = q[:, LANES * (head // 2):LANES * (head // 2 + 1)]
        if head % 2 != hk % 2:
            qc = pltpu.roll(qc, HEAD, 1)
        blocks.append((hk, jnp.where(low if hk % 2 == 0 else jnp.logical_not(low), qc, 0.0).astype(BF16)))

    def scores(i, dst_ref):
        r0 = pl.multiple_of(i * TM, TM)
        kcs = [k_ref[0, pl.ds(r0, TM), LANES * (hk // 2):LANES * (hk // 2 + 1)] for hk in range(n_kv)]
        for head, (hk, qm) in enumerate(blocks):
            dst_ref[head] = lax.dot_general(kcs[hk], qm, NT_DIMS, preferred_element_type=F32)

    def absorb(i, src_ref, stats):
        vts = [vt_ref[0, hk, :, pl.ds(pl.multiple_of(i * TM, TM), TM)] for hk in range(n_kv)]
        sts = [src_ref[head] for head in range(len(blocks))]
        ms = [jnp.maximum(m, jnp.max(st, axis=0, keepdims=True)) for st, (m, _) in zip(sts, stats)]
        pvs = [_dot(vts[hk], jnp.exp2(st - m_new).astype(BF16)) for (hk, _), st, m_new in zip(blocks, sts, ms)]
        return tuple((m_new, jnp.exp2(m - m_new) * acc + pv) for m_new, pv, (m, acc) in zip(ms, pvs, stats))

    def pair(j, stats):
        scores(2 * j + 1, sb_ref)
        stats = absorb(2 * j, sa_ref, stats)
        scores(2 * j + 2, sa_ref)
        return absorb(2 * j + 1, sb_ref, stats)

    scores(0, sa_ref)
    init = tuple((jnp.full((1, TM), -1e30, F32), jnp.zeros((vt_ref.shape[2], TM), F32)) for _ in blocks)
    res = absorb(nkv - 1, sa_ref, lax.fori_loop(0, (nkv - 1) // 2, pair, init))
    for head, (_, acc) in enumerate(res):
        ot_ref[HEAD * head:HEAD * (head + 1), :] = acc[:HEAD] / acc[HEAD:HEAD + 1]
    o_ref[0] = ot_ref[...].T.astype(BF16)


def _scan8(a, u, reverse):
    row = lax.broadcasted_iota(jnp.int32, a.shape, 0)
    for d in (1, 2, 4):
        if reverse:
            a_s, u_s, ok = pltpu.roll(a, SUBLANES - d, 0), pltpu.roll(u, SUBLANES - d, 0), row < SUBLANES - d
        else:
            a_s, u_s, ok = pltpu.roll(a, d, 0), pltpu.roll(u, d, 0), row >= d
        u = a * jnp.where(ok, u_s, 0.0) + u
        a = a * jnp.where(ok, a_s, 1.0)
    return a, u


def _rglru_kernel(xr_ref, gl_ref, cw_ref, cb_ref, gw_ref, gb_ref, lam_ref, o_ref,
                  xs_ref, a_ref, u_ref, *, cl, n):
    t_all = cl + n
    pad = SUBLANES
    width = xs_ref.shape[1]
    xs_ref[0:pad, :] = jnp.zeros((pad, width), F32)
    xs_ref[pad + t_all:, :] = jnp.zeros((pad, width), F32)
    xs_ref[pad:pad + t_all, :] = xr_ref[0]
    lam = lam_ref[...]
    z = -lam
    softplus = jnp.maximum(z, 0.0) + jnp.log(1.0 + jnp.exp(-jnp.abs(z)))
    cw = cw_ref[...]
    cb = cb_ref[...]
    gb = gb_ref[...]

    def coeffs(i, _):
        r0 = pl.multiple_of(i * TM, TM)
        blk = xs_ref[pl.ds(r0, TM + 2 * pad), :]
        rows = r0 + lax.broadcasted_iota(jnp.int32, (TM, 1), 0)
        in_lat = rows >= cl
        pos = jnp.where(in_lat, rows - cl, rows)
        seqlen = jnp.where(in_lat, n, cl)
        xc = jnp.zeros((TM, width), F32) + cb
        for j in range(CONV_K):
            off = j - CONV_LEFT
            tap = blk[pad + off:pad + off + TM, :]
            ok = jnp.logical_and(pos + off >= 0, pos + off < seqlen)
            xc = xc + jnp.where(ok, tap, 0.0) * cw[j:j + 1, :]
        xcb = xc.astype(BF16)
        for d in range(2):
            r = _sigmoid(_dot(xcb, gw_ref[0, 2 * d]) + gb[2 * d:2 * d + 1, :])
            ig = _sigmoid(_dot(xcb, gw_ref[0, 2 * d + 1]) + gb[2 * d + 1:2 * d + 2, :])
            a = jnp.exp(-RG_C * r * softplus[d:d + 1, :])
            a_ref[d, pl.ds(r0, TM), :] = a
            u_ref[d, pl.ds(r0, TM), :] = jnp.sqrt(1.0 - a * a) * (ig * xc)
        return 0

    lax.fori_loop(0, t_all // TM, coeffs, 0)

    g_all, g_ctx = t_all // SUBLANES, cl // SUBLANES

    def step(i, carry):
        h_fwd, h_rev = carry
        g_rev = jnp.where(i < g_ctx, g_ctx - 1 - i, g_all - 1 - (i - g_ctx))
        sf = pl.ds(pl.multiple_of(i * SUBLANES, SUBLANES), SUBLANES)
        sr = pl.ds(pl.multiple_of(g_rev * SUBLANES, SUBLANES), SUBLANES)
        af, uf = _scan8(a_ref[0, sf, :], u_ref[0, sf, :], False)
        ar, ur = _scan8(a_ref[1, sr, :], u_ref[1, sr, :], True)
        hf = af * h_fwd + uf
        hr = ar * h_rev + ur
        u_ref[0, sf, :] = hf
        u_ref[1, sr, :] = hr
        return hf[SUBLANES - 1:SUBLANES, :], hr[0:1, :]

    zero = jnp.zeros((1, width), F32)
    lax.fori_loop(0, g_all, step, (zero, zero), unroll=4)

    def combine(i, _):
        r0 = pl.multiple_of(i * TM, TM)
        hsum = u_ref[0, pl.ds(r0, TM), :] + u_ref[1, pl.ds(r0, TM), :]
        o_ref[0, pl.ds(r0, TM), :] = (jax.nn.gelu(gl_ref[0, pl.ds(r0, TM), :]) * hsum).astype(BF16)
        return 0

    lax.fori_loop(0, t_all // TM, combine, 0)


def _rw_in_kernel(x_ref, xp_ref, xn_ref, mod_ref, g_ref, mu_ref, wrkv_ref, wdw_ref, wda_ref, wuw_ref, wua_ref,
                  lb_ref, gd_ref, gu_ref, kk_ref, ka_ref, rk_ref, gnb_ref, ones_ref,
                  r_out, kk_out, v_out, lw_out, bb_out, kd_out, g_out, z_out, *, ct, nt):
    t = pl.program_id(1)
    g = g_ref[...]
    shift, scale = mod_ref[0, 0:1, :], mod_ref[0, 1:2, :]
    h = _norm_mod(x_ref[0], g, shift, scale)
    first = jnp.logical_or(t == 0, t == ct)
    last = jnp.logical_or(t == ct - 1, t == nt - 1)
    hp = _norm_mod(xp_ref[0], g, shift, scale)[SUBLANES - 1:SUBLANES, :]
    hn = _norm_mod(xn_ref[0], g, shift, scale)[0:1, :]
    hp = jnp.where(first, 0.0, hp)
    hn = jnp.where(last, 0.0, hn)
    row = lax.broadcasted_iota(jnp.int32, (TM, 1), 0)
    h_prev = jnp.where(row == 0, hp, pltpu.roll(h, 1, 0))
    h_next = jnp.where(row == TM - 1, hn, pltpu.roll(h, TM - 1, 0))
    xx = 0.5 * (h_prev + h_next) - h
    lerp = lambda j: (h + xx * mu_ref[j:j + 1, :]).astype(BF16)

    r = _dot(lerp(0), wrkv_ref[0])
    k = _dot(lerp(2), wrkv_ref[1])
    v = _dot(lerp(3), wrkv_ref[2])
    gate = _dot(_sigmoid(_dot(lerp(5), gd_ref[...])).astype(BF16), gu_ref[...])
    tw = jnp.tanh(_dot(lerp(1), wdw_ref[...])).astype(BF16)
    ta = _dot(lerp(4), wda_ref[...]).astype(BF16)

    kk = k * kk_ref[...]
    nrm = jnp.sqrt(_group_reduce(kk * kk, ones_ref))
    kk = kk / jnp.maximum(nrm, 1e-12)
    r_out[0] = r
    kk_out[0] = kk
    v_out[0] = v
    kd_sum = jnp.zeros_like(k)
    for d in range(2):
        dec = lb_ref[2 * d:2 * d + 1, :] + _dot(tw, wuw_ref[d])
        a = _sigmoid(lb_ref[2 * d + 1:2 * d + 2, :] + _dot(ta, wua_ref[d]))
        kd = k * (1.0 + (a - 1.0) * ka_ref[...])
        lw_out[d, 0] = -DECAY_SCALE * _sigmoid(dec)
        bb_out[d, 0] = kk * a
        kd_out[d, 0] = kd
        kd_sum = kd_sum + kd
    bonus = _group_reduce(r * kd_sum * rk_ref[...], ones_ref)
    g_out[0] = gate
    z_out[0] = (gnb_ref[...] + bonus * v) * gate


def _wkv_kernel(r_ref, kk_ref, v_ref, lw_ref, bb_ref, kd_ref, y_ref, h_ref):
    c, lw_n = WKV_CHUNK, WKV_LANES
    n_heads = lw_n // HEAD
    s_rows = n_heads * c
    d = pl.program_id(0)
    i = pl.program_id(3)

    @pl.when(i == 0)
    def _():
        h_ref[...] = jnp.zeros_like(h_ref)

    rev = d == 1
    row = lax.broadcasted_iota(jnp.int32, (c, c), 0)
    col = lax.broadcasted_iota(jnp.int32, (c, c), 1)
    tri = (jnp.where(rev, col - row, row - col) >= 0).astype(BF16)
    srow = lax.broadcasted_iota(jnp.int32, (s_rows, s_rows), 0)
    scol = lax.broadcasted_iota(jnp.int32, (s_rows, s_rows), 1)
    ahead = jnp.where(rev, scol % c - srow % c, srow % c - scol % c)
    strict = ahead > 0
    incl = ahead >= 0
    eye = (srow == scol).astype(F32)
    same_block = lambda size: (srow // size) == (scol // size)
    lrow =lax.broadcasted_iota(jnp.int32, (lw_n, lw_n), 0)
    lcol = lax.broadcasted_iota(jnp.int32, (lw_n, lw_n), 1)
    same_head = (lrow // HEAD) == (lcol // HEAD)
    diag = lrow == lcol
    head_of_lane = lax.broadcasted_iota(jnp.int32, (s_rows, lw_n), 1) // HEAD
    head_of_row = lax.broadcasted_iota(jnp.int32, (s_rows, lw_n), 0) // c
    own = head_of_lane == head_of_row
    n_chunks = TM // c

    def stack(x):
        return jnp.where(own, jnp.concatenate([x] * n_heads, axis=0), 0.0)

    def unstack(xs):
        out = xs[0:c]
        for hh in range(1, n_heads):
            out = out + xs[hh * c:(hh + 1) * c]
        return out

    def local(grp, ci):
        cidx = jnp.where(rev, n_chunks - 1 - ci, ci)
        sl = pl.ds(pl.multiple_of(cidx * c, c), c)
        ls = slice(grp * lw_n, (grp + 1) * lw_n)
        r, kk, v = r_ref[0, sl, ls], kk_ref[0, sl, ls], v_ref[0, sl, ls]
        lw, bb, kd = lw_ref[0, 0, sl, ls], bb_ref[0, 0, sl, ls], kd_ref[0, 0, sl, ls]
        lw_hi, lw_rest = _split(lw)[0], lw - _split(lw)[0].astype(F32)
        lw_mid, lw_lo = _split(lw_rest)
        cum = _dot(tri, lw_hi) + (_dot(tri, lw_mid) + _dot(tri, lw_lo))
        yield
        tot = jnp.sum(lw, axis=0, keepdims=True)
        e_neg = jnp.exp(-cum)
        e_end = jnp.exp(tot - cum)
        kd_g, b_g = kd * e_end, bb * e_end
        kk_s = stack(kk * jnp.exp(cum - lw))
        r_s = stack(r * jnp.exp(cum))
        v_s = stack(v).astype(BF16)
        lhs = jnp.concatenate([kk_s, r_s], axis=0)
        yield
        p1 = _bdot(lhs, stack(kd * e_neg), NT_DIMS)
        yield
        p2 = _bdot(lhs, stack(bb * e_neg), NT_DIMS)
        yield
        a_kd = jnp.where(strict, p1[:s_rows], 0.0)
        b_kd = jnp.where(incl, p1[s_rows:], 0.0)
        nmat = jnp.where(strict, p2[:s_rows], 0.0)
        b_b = jnp.where(incl, p2[s_rows:], 0.0).astype(BF16)
        tinv = eye - jnp.where(same_block(2), nmat, 0.0)
        size = 2
        while size < c:
            n_off = jnp.where(jnp.logical_and(same_block(2 * size), jnp.logical_not(same_block(size))), nmat, 0.0)
            tb = tinv.astype(BF16)
            nt = _bdot(n_off, tb)
            yield
            tinv = tinv - _bdot(tb, nt)
            yield
            size *= 2
        av = _bdot(a_kd, v_s)
        yield
        tw = _bdot(tinv, jnp.concatenate([kk_s, av], axis=1))
        yield
        bw = _bdot(b_b, tw)
        yield
        kkp, u0 = unstack(tw[:, :lw_n]), unstack(tw[:, lw_n:])
        rp = unstack(r_s - bw[:, :lw_n])
        y0 = unstack(_bdot(b_kd, v_s) - bw[:, lw_n:])
        m_mat = jnp.where(diag, jnp.exp(tot), 0.0) - jnp.where(same_head, _bdot(b_g, kkp, TN_DIMS), 0.0)
        g_mat = jnp.where(same_head, _bdot(jnp.concatenate([kd_g, b_g], axis=0),
                                           jnp.concatenate([v, -u0], axis=0), TN_DIMS), 0.0)
        return grp, sl, ls, rp, y0, m_mat, g_mat

    chains = [local(grp, ci) for ci in range(n_chunks) for grp in range(h_ref.shape[0])]
    parts = [None] * len(chains)
    while any(p is None for p in parts):
        for idx, chain in enumerate(chains):
            if parts[idx] is None:
                try:
                    next(chain)
                except StopIteration as done:
                    parts[idx] = done.value
    hs = [h_ref[grp] for grp in range(h_ref.shape[0])]
    for grp, sl, ls, rp, y0, m_mat, g_mat in parts:
        y_ref[0, 0, sl, ls] = _bdot(rp, hs[grp]) + y0
        hs[grp] = _bdot(m_mat, hs[grp]) + g_mat
    for grp, h_new in enumerate(hs):
        h_ref[grp] = h_new


def _bdot(a, b, dims=(((1,), (0,)), ((), ()))):
    return lax.dot_general(a.astype(BF16), b.astype(BF16), dims, preferred_element_type=F32)


def _post_kernel(*refs, odd, ff_chunk):
    if odd:
        (x_ref, y0_ref, y1_ref, gate_ref, z_ref, gng_ref, ones_ref,
         wo_ref, mod_ref, g2_ref, w1_ref, w2_ref, o_ref) = refs
        y = y0_ref[0, 0] + y1_ref[0, 0]
        dlt = y - _group_reduce(y, ones_ref)
        var = _group_reduce(dlt * dlt, ones_ref)
        mix = (dlt * lax.rsqrt(var + GN_EPS) * gng_ref[...] * gate_ref[0] + z_ref[0]).astype(BF16)
    else:
        x_ref, att_ref, rec_ref, wo_ref, mod_ref, g2_ref, w1_ref, w2_ref, o_ref = refs
        mix = jnp.concatenate([att_ref[0], rec_ref[0]], axis=1)
    x1 = x_ref[0] + mod_ref[0, 2:3, :] * _dot(mix, wo_ref[...])
    h2 = _norm_mod(x1, g2_ref[...], mod_ref[0, 3:4, :], mod_ref[0, 4:5, :]).astype(BF16)
    acc = jnp.zeros_like(x1)
    for c in range(w1_ref.shape[1] // ff_chunk):
        a = _dot(h2, w1_ref[:, c * ff_chunk:(c + 1) * ff_chunk])
        a = jnp.square(jnp.maximum(a, 0.0)).astype(BF16)
        acc = acc + _dot(a, w2_ref[c * ff_chunk:(c + 1) * ff_chunk, :])
    o_ref[0] = x1 + mod_ref[0, 5:6, :] * acc


def _blockdiag_ones(width, value):
    idx = jnp.arange(width) // HEAD
    return jnp.where(idx[:, None] == idx[None, :], value, 0.0).astype(BF16)


def _rope_tables(cl, n, grid_w):
    rows = n // grid_w
    row = jnp.repeat(jnp.arange(rows, dtype=F32), grid_w)
    col = jnp.tile(jnp.arange(grid_w, dtype=F32), rows)
    half = HEAD // 2
    inv = ROPE_THETA ** (-jnp.arange(0, half, 2, dtype=F32) / half)
    ang = jnp.concatenate([row[:, None] * inv, col[:, None] * inv], axis=-1)
    cos = jnp.repeat(jnp.cos(ang), 2, axis=-1)
    sin = jnp.repeat(jnp.sin(ang), 2, axis=-1)
    even = (jnp.arange(HEAD) % 2 == 0)[None, :]
    sa = jnp.where(even, -sin, 0.0)
    sb = jnp.where(even, 0.0, sin)
    ctx = lambda fill: jnp.full((cl, HEAD), fill, F32)
    full = lambda lat, fill: jnp.tile(jnp.concatenate([ctx(fill), lat], axis=0), (1, LANES // HEAD))
    return full(cos, 1.0), full(sa, 0.0), full(sb, 0.0)


def _tile_specs(b_all, ct, d_model):
    mod_spec = pl.BlockSpec((1, 6, d_model), lambda b, t: (jnp.where(t < ct, b_all, b), 0, 0))
    row_spec = lambda w: pl.BlockSpec((1, TM, w), lambda b, t: (b, t, 0))
    return mod_spec, row_spec


def _hybrid_layer(xs, mods, g1, w_in, w_out, qn, kn, conv_w, conv_b, gate_w, gate_b, lam, ropes, dims):
    b_all, cl, n, d_model = dims
    t_all = cl + n
    ct, nt = cl // TM, t_all // TM
    in_w = w_in.shape[1]
    rnn_w = conv_w.shape[1]
    kv_w = (in_w - 2 * rnn_w - d_model // 2) // 2
    q_w = in_w - 2 * kv_w - 2 * rnn_w
    n_kv = kv_w // HEAD
    assert ct % 2 == 1 and nt % 2 == 1, "the attention kernel walks key tiles in pairs plus one"
    groups = q_w // kv_w
    mod_spec, row_spec = _tile_specs(b_all, ct, d_model)
    rope_spec = pl.BlockSpec((TM, LANES), lambda b, t: (t, 0))
    cos, sa, sb = ropes

    q, k, vt, xr, gl = pl.pallas_call(
        functools.partial(_hy_in_kernel, q_w=q_w, kv_w=kv_w, rnn_w=rnn_w),
        grid=(b_all, nt),
        in_specs=[row_spec(d_model), mod_spec, _const_spec((1, d_model)), _const_spec((d_model, in_w)),
                  rope_spec, rope_spec, rope_spec, _const_spec((1, q_w)), _const_spec((1, kv_w)),
                  _const_spec((MXU, MXU)), _const_spec((kv_w, kv_w))],
        out_specs=[row_spec(q_w), row_spec(kv_w),
                   pl.BlockSpec((1, n_kv, HEAD + V_ONES, TM), lambda b, t: (b, 0, 0, t)),
                   row_spec(rnn_w), row_spec(rnn_w)],
        out_shape=[jax.ShapeDtypeStruct((b_all, t_all, q_w), BF16),
                   jax.ShapeDtypeStruct((b_all, t_all, kv_w), BF16),
                   jax.ShapeDtypeStruct((b_all, n_kv, HEAD + V_ONES, t_all), BF16),
                   jax.ShapeDtypeStruct((b_all, t_all, rnn_w), F32),
                   jax.ShapeDtypeStruct((b_all, t_all, rnn_w), F32)],
        compiler_params=_params(("parallel", "parallel")),
    )(xs, mods, g1.reshape(1, d_model), w_in.astype(BF16), cos, sa, sb,
      jnp.tile(qn, q_w // HEAD).reshape(1, q_w), jnp.tile(kn, kv_w // HEAD).reshape(1, kv_w),
      _blockdiag_ones(MXU, 1.0 / HEAD), _blockdiag_ones(kv_w, 1.0 / HEAD))

    att = pl.pallas_call(
        functools.partial(_attn_kernel, ct=ct, nt=nt, groups=groups, n_kv=n_kv),
        grid=(b_all, nt),
        in_specs=[row_spec(q_w),
                  pl.BlockSpec((1, t_all, kv_w), lambda b, t: (b, 0, 0)),
                  pl.BlockSpec((1, n_kv, HEAD + V_ONES, t_all), lambda b, t: (b, 0, 0, 0))],
        out_specs=row_spec(q_w),
        out_shape=jax.ShapeDtypeStruct((b_all, t_all, q_w), BF16),
        scratch_shapes=[pltpu.VMEM((q_w, TM), F32)] + [pltpu.VMEM((q_w // HEAD, TM, TM), F32)] * 2,
        compiler_params=_params(("parallel", "parallel")),
    )(q, k, vt)

    n_lc = rnn_w // RG_LANES
    per = RG_LANES // HEAD
    gw = gate_w.reshape(4, n_lc, per, HEAD, HEAD)
    eye = jnp.eye(per, dtype=F32)
    gw = jnp.einsum('gcpde,pq->cgpdqe', gw, eye).reshape(n_lc, 4, RG_LANES, RG_LANES).astype(BF16)
    lane_spec = lambda rows: pl.BlockSpec((rows, RG_LANES), lambda b, c: (0, c))
    seq_spec = pl.BlockSpec((1, t_all, RG_LANES), lambda b, c: (b, 0, c))
    rec = pl.pallas_call(
        functools.partial(_rglru_kernel, cl=cl, n=n),
        grid=(b_all, n_lc),
        in_specs=[seq_spec, seq_spec, lane_spec(CONV_K), lane_spec(1),
                  pl.BlockSpec((1, 4, RG_LANES, RG_LANES), lambda b, c: (c, 0, 0, 0)),
                  lane_spec(4), lane_spec(2)],
        out_specs=seq_spec,
        out_shape=jax.ShapeDtypeStruct((b_all, t_all, rnn_w), BF16),
        scratch_shapes=[pltpu.VMEM((t_all + 2 * SUBLANES, RG_LANES), F32),
                        pltpu.VMEM((2, t_all, RG_LANES), F32),
                        pltpu.VMEM((2, t_all, RG_LANES), F32)],
        compiler_params=_params(("parallel", "parallel")),
    )(xr, gl, conv_w, conv_b.reshape(1, rnn_w), gw, gate_b.reshape(4, rnn_w), lam)
    return (att, rec), w_out.astype(BF16)


def _rwkv_layer(xs, mods, g1, mu, w_rkv, lora_down, lora_up, lora_bias, gate_down, gate_up,
                k_k, k_a, r_k, gn_b, dims):
    b_all, cl, n, d_model = dims
    t_all = cl + n
    ct, nt = cl // TM, t_all // TM
    n8 = t_all // SUBLANES
    per8 = TM // SUBLANES
    lora = lora_down.shape[-1]
    glora = gate_down.shape[-1]
    mod_spec, row_spec = _tile_specs(b_all, ct, d_model)
    dir_spec = pl.BlockSpec((2, 1, TM, d_model), lambda b, t: (0, b, t, 0))
    wdw = jnp.concatenate([lora_down[0, 0], lora_down[1, 0]], axis=1).astype(BF16)
    wda = jnp.concatenate([lora_down[0, 1], lora_down[1, 1]], axis=1).astype(BF16)
    zeros = jnp.zeros((lora, d_model), F32)
    pad_up = lambda j: jnp.stack([jnp.concatenate([lora_up[0, j], zeros], axis=0),
                                  jnp.concatenate([zeros, lora_up[1, j]], axis=0)]).astype(BF16)
    vec = lambda a: a.reshape(1, d_model)
    outs = pl.pallas_call(
        functools.partial(_rw_in_kernel, ct=ct, nt=nt),
        grid=(b_all, nt),
        in_specs=[row_spec(d_model),
                  pl.BlockSpec((1, SUBLANES, d_model), lambda b, t: (b, jnp.maximum(t * per8 - 1, 0), 0)),
                  pl.BlockSpec((1, SUBLANES, d_model), lambda b, t: (b, jnp.minimum((t + 1) * per8, n8 - 1), 0)),
                  mod_spec, _const_spec((1, d_model)), _const_spec((6, d_model)),
                  _const_spec((3, d_model, d_model)), _const_spec((d_model, 2 * lora)),
                  _const_spec((d_model, 2 * lora)), _const_spec((2, 2 * lora, d_model)),
                  _const_spec((2, 2 * lora, d_model)), _const_spec((4, d_model)),
                  _const_spec((d_model, glora)), _const_spec((glora, d_model)),
                  _const_spec((1, d_model)), _const_spec((1, d_model)), _const_spec((1, d_model)),
                  _const_spec((1, d_model)), _const_spec((MXU, MXU))],
        out_specs=[row_spec(d_model), row_spec(d_model), row_spec(d_model),
                   dir_spec, dir_spec, dir_spec, row_spec(d_model), row_spec(d_model)],
        out_shape=[jax.ShapeDtypeStruct((b_all, t_all, d_model), F32)] * 3
        + [jax.ShapeDtypeStruct((2, b_all, t_all, d_model), F32)] * 3
        + [jax.ShapeDtypeStruct((b_all, t_all, d_model), F32)] * 2,
        compiler_params=_params(("parallel", "parallel")),
    )(xs, xs, xs, mods, vec(g1), mu, w_rkv.astype(BF16), wdw, wda, pad_up(0), pad_up(1),
      lora_bias.reshape(4, d_model), gate_down.astype(BF16), gate_up.astype(BF16),
      vec(k_k), vec(k_a), vec(r_k), vec(gn_b), _blockdiag_ones(MXU, 1.0))
    r, kk, v, lw, bb, kd, gate, z = outs

    def tmap(dd, i):
        rev = jnp.where(i < ct, ct - 1 - i, nt - 1 - (i - ct))
        return jnp.where(dd == 0, i, rev)

    blk_lanes = WKV_LANES * WKV_GROUPS
    shared = pl.BlockSpec((1, TM, blk_lanes), lambda dd, b, hh, i: (b, tmap(dd, i), hh))
    per_dir = pl.BlockSpec((1, 1, TM, blk_lanes), lambda dd, b, hh, i: (dd, b, tmap(dd, i), hh))
    y = pl.pallas_call(
        _wkv_kernel,
        grid=(2, b_all, d_model // blk_lanes, nt),
        in_specs=[shared, shared, shared, per_dir, per_dir, per_dir],
        out_specs=per_dir,
        out_shape=jax.ShapeDtypeStruct((2, b_all, t_all, d_model), F32),
        scratch_shapes=[pltpu.VMEM((WKV_GROUPS, WKV_LANES, WKV_LANES), F32)],
        compiler_params=_params(("arbitrary", "arbitrary", "arbitrary", "arbitrary")),
    )(r, kk, v, lw, bb, kd)
    return y, gate, z


def _post(xs, mix_inputs, w_o, mods, g2, w1, w2, dims, odd, gn_g=None, latent_only=False):
    b_all, cl, n, d_model = dims
    t_all = cl + n
    ct, nt = cl // TM, t_all // TM
    t0 = ct if latent_only else 0
    d_ff = w1.shape[1]
    mod_spec = pl.BlockSpec((1, 6, d_model), lambda b, t: (jnp.where(t + t0 < ct, b_all, b), 0, 0))
    row_spec = lambda w: pl.BlockSpec((1, TM, w), lambda b, t: (b, t + t0, 0))
    tail_specs = [_const_spec((d_model, d_model)), mod_spec, _const_spec((1, d_model)),
                  _const_spec((d_model, d_ff)), _const_spec((d_ff, d_model))]
    tail_args = (w_o, mods, g2.reshape(1, d_model), w1.astype(BF16), w2.astype(BF16))
    if odd:
        y, gate, z = mix_inputs
        y_spec = lambda dd: pl.BlockSpec((1, 1, TM, d_model), lambda b, t: (dd, b, t + t0, 0))
        in_specs = [row_spec(d_model), y_spec(0), y_spec(1), row_spec(d_model), row_spec(d_model),
                    _const_spec((1, d_model)), _const_spec((MXU, MXU))] + tail_specs
        args = (xs, y, y, gate, z, gn_g.reshape(1, d_model), _blockdiag_ones(MXU, 1.0 / HEAD)) + tail_args
    else:
        att, rec = mix_inputs
        in_specs = [row_spec(d_model), row_spec(att.shape[-1]), row_spec(rec.shape[-1])] + tail_specs
        args = (xs, att, rec) + tail_args
    rows_out = n if latent_only else t_all
    return pl.pallas_call(
        functools.partial(_post_kernel, odd=odd, ff_chunk=min(d_ff, 4 * MXU)),
        grid=(b_all, nt - t0),
        in_specs=in_specs,
        out_specs=pl.BlockSpec((1, TM, d_model), lambda b, t: (b, t, 0)),
        out_shape=jax.ShapeDtypeStruct((b_all, rows_out, d_model), F32),
        compiler_params=_params(("parallel", "parallel")),
    )(*args)


def kernel(x, c, ctx, c_ctx, ada_w, ada_b, norm_g, mlp_w1, mlp_w2, hy_w_in, hy_w_out, hy_q_norm, hy_k_norm, hy_conv_w, hy_conv_b, hy_gate_w, hy_gate_b, hy_lam, rw_mu, rw_w_rkv, rw_w_o, rw_lora_down, rw_lora_up, rw_lora_bias, rw_gate_down, rw_gate_up, rw_k_k, rw_k_a, rw_r_k, rw_gn_g, rw_gn_b):
    b_all, n, d_model = x.shape
    cl = ctx.shape[1]
    depth = ada_w.shape[0]
    assert cl % TM == 0 and n % TM == 0 and d_model % MXU == 0
    grid_w = 64
    dims = (b_all, cl, n, d_model)
    mods_all = _ada_mods(jnp.concatenate([c, c_ctx[None, :]], axis=0), ada_w, ada_b)
    ropes = _rope_tables(cl, n, grid_w)
    xs = jnp.concatenate([ctx, x], axis=1)
    for l in range(depth):
        i = l // 2
        mods = mods_all[l]
        last = l == depth - 1
        if l % 2 == 0:
            mix, w_o = _hybrid_layer(xs, mods, norm_g[l, 0], hy_w_in[i], hy_w_out[i], hy_q_norm[i], hy_k_norm[i],
                                     hy_conv_w[i], hy_conv_b[i], hy_gate_w[i], hy_gate_b[i], hy_lam[i], ropes, dims)
            xs = _post(xs, mix, w_o, mods, norm_g[l, 1], mlp_w1[l], mlp_w2[l], dims, odd=False, latent_only=last)
        else:
            mix = _rwkv_layer(xs, mods, norm_g[l, 0], rw_mu[i], rw_w_rkv[i], rw_lora_down[i], rw_lora_up[i],
                              rw_lora_bias[i], rw_gate_down[i], rw_gate_up[i], rw_k_k[i], rw_k_a[i],
                              rw_r_k[i], rw_gn_b[i], dims)
            xs = _post(xs, mix, rw_w_o[i].astype(BF16), mods, norm_g[l, 1], mlp_w1[l], mlp_w2[l], dims,
                       odd=True, gn_g=rw_gn_g[i], latent_only=last)
    return xs if xs.shape[1] == n else xs[:, cl:]
```

```python
import functools
import math

import jax
import jax.numpy as jnp
from jax import lax
from jax.experimental import pallas as pl
from jax.experimental.pallas import tpu as pltpu

F32 = jnp.float32
BF16 = jnp.bfloat16

HEAD = 64
LANES = 128
SUBLANES = 8
MXU = 256
TM = 256
VMEM_LIMIT = 56 * 1024 * 1024

EPS = 1e-6
GN_EPS = 64e-5
RG_C = 8.0
ROPE_THETA = 10000.0
DECAY_SCALE = math.exp(-0.5)
CONV_K = 4
CONV_LEFT = 2
Q_SCALE = HEAD ** -0.5 * math.log2(math.e)
V_ONES = 16
RG_LANES = 256
WKV_CHUNK = 64
WKV_LANES = 256
WKV_GROUPS = 4

NT_DIMS = (((1,), (1,)), ((), ()))
TN_DIMS = (((0,), (0,)), ((), ()))


def _params(sem):
    return pltpu.CompilerParams(dimension_semantics=sem, vmem_limit_bytes=VMEM_LIMIT)


def _const_spec(shape):
    nd = len(shape)
    return pl.BlockSpec(shape, lambda *_: (0,) * nd, pipeline_mode=pl.Buffered(1))


def _dot(a, b):
    return jnp.dot(a, b, preferred_element_type=F32)


def _dot_exact(a, b, dims=None):
    if dims is None:
        return jnp.dot(a, b, preferred_element_type=F32, precision=lax.Precision.HIGHEST)
    return lax.dot_general(a, b, dims, preferred_element_type=F32, precision=lax.Precision.HIGHEST)


def _split(x):
    hi = x.astype(BF16)
    lo = (x - hi.astype(F32)).astype(BF16)
    return hi, lo


def _group_reduce(x, ones_ref):
    cw = ones_ref.shape[0]
    ones = ones_ref[...]
    outs = []
    for c in range(x.shape[1] // cw):
        hi, lo = _split(x[:, c * cw:(c + 1) * cw])
        outs.append(_dot(hi, ones) + _dot(lo, ones))
    return outs[0] if len(outs) == 1 else jnp.concatenate(outs, axis=1)


def _sigmoid(x):
    return 0.5 * jnp.tanh(0.5 * x) + 0.5


def _norm_mod(x, g, shift, scale):
    ms = jnp.mean(x * x, axis=-1, keepdims=True)
    return (x * lax.rsqrt(ms + EPS) * g) * (1.0 + scale) + shift


def _ada_kernel(c_ref, w_ref, b_ref, o_ref):
    c = c_ref[...]
    s = c * _sigmoid(c)
    o_ref[0, 0] = _dot_exact(s, w_ref[0]) + b_ref[0]


def _ada_mods(cc, ada_w, ada_b):
    depth, d, _ = ada_w.shape
    rows = cc.shape[0]
    out = pl.pallas_call(
        _ada_kernel,
        grid=(depth, 6),
        in_specs=[pl.BlockSpec((rows, d), lambda l, j: (0, 0)),
                  pl.BlockSpec((1, d, d), lambda l, j: (l, 0, j)),
                  pl.BlockSpec((1, 1, d), lambda l, j: (l * 6 + j, 0, 0))],
        out_specs=pl.BlockSpec((1, 1, rows, d), lambda l, j: (l, j, 0, 0)),
        out_shape=jax.ShapeDtypeStruct((depth, 6, rows, d), F32),
        compiler_params=_params(("arbitrary", "arbitrary")),
    )(cc, ada_w, ada_b.reshape(depth * 6, 1, d))
    return jnp.transpose(out, (0, 2, 1, 3))


def _hy_in_kernel(x_ref, mod_ref, g_ref, w_ref, cos_ref, sa_ref, sb_ref, qg_ref, kg_ref, oq_ref, ok_ref,
                  q_out, k_out, vt_out, xr_out, gl_out, *, q_w, kv_w, rnn_w):
    x = x_ref[0]
    h = _norm_mod(x, g_ref[...], mod_ref[0, 0:1, :], mod_ref[0, 1:2, :]).astype(BF16)
    z = _dot(h, w_ref[...])
    c0, c1, c2, c3 = q_w, q_w + kv_w, q_w + 2 * kv_w, q_w + 2 * kv_w + rnn_w

    def norm_rope(u, gain, ones_ref):
        width = u.shape[1]
        un = u * lax.rsqrt(_group_reduce(u * u, ones_ref) + EPS) * gain
        reps = width // LANES
        tile = lambda r: jnp.concatenate([r[...]] * reps, axis=1) if reps > 1 else r[...]
        nxt = pltpu.roll(un, width - 1, 1)
        prv = pltpu.roll(un, 1, 1)
        return un * tile(cos_ref) + nxt * tile(sa_ref) + prv * tile(sb_ref)

    q_out[0] = (norm_rope(z[:, :c0], qg_ref[...], oq_ref) * Q_SCALE).astype(BF16)
    k_out[0] = norm_rope(z[:, c0:c1], kg_ref[...], ok_ref).astype(BF16)
    vt = z[:, c1:c2].T
    ones = jnp.ones((V_ONES, TM), F32)
    for hk in range(kv_w // HEAD):
        vt_out[0, hk] = jnp.concatenate([vt[hk * HEAD:(hk + 1) * HEAD], ones], axis=0).astype(BF16)
    xr_out[0] = z[:, c2:c3]
    gl_out[0] = z[:, c3:]


def _attn_kernel(q_ref, k_ref, vt_ref, o_ref, ot_ref, sa_ref, sb_ref, *, ct, nt, groups, n_kv):
    t = pl.program_id(1)
    nkv = jnp.where(t < ct, ct, nt)
    q = q_ref[0].astype(F32)
    lane = lax.broadcasted_iota(jnp.int32, (TM, LANES), 1)
    low = lane < HEAD
    blocks = []
    for head in range(groups * n_kv):
        hk = head // groups
        qc = q[:, LANES * (head // 2):LANES * (head // 2 + 1)]
        if head % 2 != hk % 2:
            qc = pltpu.roll(qc, HEAD, 1)
        blocks.append((hk, jnp.where(low if hk % 2 == 0 else jnp.logical_not(low), qc, 0.0).astype(BF16)))

    def scores(i, dst_ref, heads):
        r0 = pl.multiple_of(i * TM, TM)
        for head in heads:
            hk, qm = blocks[head]
            kc = k_ref[0, pl.ds(r0, TM), LANES * (hk // 2):LANES * (hk // 2 + 1)]
            dst_ref[head] = lax.dot_general(kc, qm, NT_DIMS, preferred_element_type=F32)

    def absorb(i, src_ref, stats, nxt=None):
        r0 = pl.multiple_of(i * TM, TM)
        out = []
        for head, ((hk, _), (m, acc)) in enumerate(zip(blocks, stats)):
            if nxt is not None:
                scores(nxt[0], nxt[1], [head])
            st = src_ref[head]
            m_new = jnp.maximum(m, jnp.max(st, axis=0, keepdims=True))
            out.append((m_new, _dot(vt_ref[0, hk, :, pl.ds(r0, TM)], jnp.exp2(st - m_new).astype(BF16))))
        return tuple((m_new, jnp.exp2(m - m_new) * acc + pv) for (m_new, pv), (m, acc) in zip(out, stats))

    def pair(j, stats):
        stats = absorb(2 * j, sa_ref, stats, (2 * j + 1, sb_ref))
        return absorb(2 * j + 1, sb_ref, stats, (2 * j + 2, sa_ref))

    scores(0, sa_ref, range(len(blocks)))
    init = tuple((jnp.full((1, TM), -1e30, F32), jnp.zeros((vt_ref.shape[2], TM), F32)) for _ in blocks)
    res = absorb(nkv - 1, sa_ref, lax.fori_loop(0, (nkv - 1) // 2, pair, init))
    for head, (_, acc) in enumerate(res):
        ot_ref[HEAD * head:HEAD * (head + 1), :] = acc[:HEAD] / acc[HEAD:HEAD + 1]
    o_ref[0] = ot_ref[...].T.astype(BF16)


def _scan8(a, u, reverse):
    row = lax.broadcasted_iota(jnp.int32, a.shape, 0)
    for d in (1, 2, 4):
        if reverse:
            a_s, u_s, ok = pltpu.roll(a, SUBLANES - d, 0), pltpu.roll(u, SUBLANES - d, 0), row < SUBLANES - d
        else:
            a_s, u_s, ok = pltpu.roll(a, d, 0), pltpu.roll(u, d, 0), row >= d
        u = a * jnp.where(ok, u_s, 0.0) + u
        a = a * jnp.where(ok, a_s, 1.0)
    return a, u


def _rglru_kernel(xr_ref, gl_ref, cw_ref, cb_ref, gw_ref, gb_ref, lam_ref, o_ref,
                  xs_ref, a_ref, u_ref, *, cl, n):
    t_all = cl + n
    pad = SUBLANES
    width = xs_ref.shape[1]
    xs_ref[0:pad, :] = jnp.zeros((pad, width), F32)
    xs_ref[pad + t_all:, :] = jnp.zeros((pad, width), F32)
    xs_ref[pad:pad + t_all, :] = xr_ref[0]
    lam = lam_ref[...]
    z = -lam
    softplus = jnp.maximum(z, 0.0) + jnp.log(1.0 + jnp.exp(-jnp.abs(z)))
    cw = cw_ref[...]
    cb = cb_ref[...]
    gb = gb_ref[...]

    def coeffs(i, _):
        r0 = pl.multiple_of(i * TM, TM)
        blk = xs_ref[pl.ds(r0, TM + 2 * pad), :]
        rows = r0 + lax.broadcasted_iota(jnp.int32, (TM, 1), 0)
        in_lat = rows >= cl
        pos = jnp.where(in_lat, rows - cl, rows)
        seqlen = jnp.where(in_lat, n, cl)
        xc = jnp.zeros((TM, width), F32) + cb
        for j in range(CONV_K):
            off = j - CONV_LEFT
            tap = blk[pad + off:pad + off + TM, :]
            ok = jnp.logical_and(pos + off >= 0, pos + off < seqlen)
            xc = xc + jnp.where(ok, tap, 0.0) * cw[j:j + 1, :]
        xcb = xc.astype(BF16)
        for d in range(2):
            r = _sigmoid(_dot(xcb, gw_ref[0, 2 * d]) + gb[2 * d:2 * d + 1, :])
            ig = _sigmoid(_dot(xcb, gw_ref[0, 2 * d + 1]) + gb[2 * d + 1:2 * d + 2, :])
            a = jnp.exp(-RG_C * r * softplus[d:d + 1, :])
            a_ref[d, pl.ds(r0, TM), :] = a
            u_ref[d, pl.ds(r0, TM), :] = jnp.sqrt(1.0 - a * a) * (ig * xc)
        return 0

    lax.fori_loop(0, t_all // TM, coeffs, 0)

    g_all, g_ctx = t_all // SUBLANES, cl // SUBLANES

    def step(i, carry):
        h_fwd, h_rev = carry
        g_rev = jnp.where(i < g_ctx, g_ctx - 1 - i, g_all - 1 - (i - g_ctx))
        sf = pl.ds(pl.multiple_of(i * SUBLANES, SUBLANES), SUBLANES)
        sr = pl.ds(pl.multiple_of(g_rev * SUBLANES, SUBLANES), SUBLANES)
        af, uf = _scan8(a_ref[0, sf, :], u_ref[0, sf, :], False)
        ar, ur = _scan8(a_ref[1, sr, :], u_ref[1, sr, :], True)
        hf = af * h_fwd + uf
        hr = ar * h_rev + ur
        u_ref[0, sf, :] = hf
        u_ref[1, sr, :] = hr
        return hf[SUBLANES - 1:SUBLANES, :], hr[0:1, :]

    zero = jnp.zeros((1, width), F32)
    lax.fori_loop(0, g_all, step, (zero, zero), unroll=4)

    def combine(i, _):
        r0 = pl.multiple_of(i * TM, TM)
        hsum = u_ref[0, pl.ds(r0, TM), :] + u_ref[1, pl.ds(r0, TM), :]
        o_ref[0, pl.ds(r0, TM), :] = (jax.nn.gelu(gl_ref[0, pl.ds(r0, TM), :]) * hsum).astype(BF16)
        return 0

    lax.fori_loop(0, t_all // TM, combine, 0)


def _rw_in_kernel(x_ref, xp_ref, xn_ref, mod_ref, g_ref, mu_ref, wrkv_ref, wdw_ref, wda_ref, wuw_ref, wua_ref,
                  lb_ref, gd_ref, gu_ref, kk_ref, ka_ref, rk_ref, gnb_ref, ones_ref,
                  r_out, kk_out, v_out, lw_out, bb_out, kd_out, g_out, z_out, *, ct, nt):
    t = pl.program_id(1)
    g = g_ref[...]
    shift, scale = mod_ref[0, 0:1, :], mod_ref[0, 1:2, :]
    h = _norm_mod(x_ref[0], g, shift, scale)
    first = jnp.logical_or(t == 0, t == ct)
    last = jnp.logical_or(t == ct - 1, t == nt - 1)
    hp = _norm_mod(xp_ref[0], g, shift, scale)[SUBLANES - 1:SUBLANES, :]
    hn = _norm_mod(xn_ref[0], g, shift, scale)[0:1, :]
    hp = jnp.where(first, 0.0, hp)
    hn = jnp.where(last, 0.0, hn)
    row = lax.broadcasted_iota(jnp.int32, (TM, 1), 0)
    h_prev = jnp.where(row == 0, hp, pltpu.roll(h, 1, 0))
    h_next = jnp.where(row == TM - 1, hn, pltpu.roll(h, TM - 1, 0))
    xx = 0.5 * (h_prev + h_next) - h
    lerp = lambda j: (h + xx * mu_ref[j:j + 1, :]).astype(BF16)

    r = _dot(lerp(0), wrkv_ref[0])
    k = _dot(lerp(2), wrkv_ref[1])
    v = _dot(lerp(3), wrkv_ref[2])
    gate = _dot(_sigmoid(_dot(lerp(5), gd_ref[...])).astype(BF16), gu_ref[...])
    tw = jnp.tanh(_dot(lerp(1), wdw_ref[...])).astype(BF16)
    ta = _dot(lerp(4), wda_ref[...]).astype(BF16)

    kk = k * kk_ref[...]
    nrm = jnp.sqrt(_group_reduce(kk * kk, ones_ref))
    kk = kk / jnp.maximum(nrm, 1e-12)
    r_out[0] = r
    kk_out[0] = kk
    v_out[0] = v
    kd_sum = jnp.zeros_like(k)
    for d in range(2):
        dec = lb_ref[2 * d:2 * d + 1, :] + _dot(tw, wuw_ref[d])
        a = _sigmoid(lb_ref[2 * d + 1:2 * d + 2, :] + _dot(ta, wua_ref[d]))
        kd = k * (1.0 + (a - 1.0) * ka_ref[...])
        lw_out[d, 0] = -DECAY_SCALE * _sigmoid(dec)
        bb_out[d, 0] = kk * a
        kd_out[d, 0] = kd
        kd_sum = kd_sum + kd
    bonus = _group_reduce(r * kd_sum * rk_ref[...], ones_ref)
    g_out[0] = gate
    z_out[0] = (gnb_ref[...] + bonus * v) * gate


def _wkv_kernel(r_ref, kk_ref, v_ref, lw_ref, bb_ref, kd_ref, y_ref, h_ref):
    c, lw_n = WKV_CHUNK, WKV_LANES
    n_heads = lw_n // HEAD
    s_rows = n_heads * c
    d = pl.program_id(0)
    i = pl.program_id(3)

    @pl.when(i == 0)
    def _():
        h_ref[...] = jnp.zeros_like(h_ref)

    rev = d == 1
    row = lax.broadcasted_iota(jnp.int32, (c, c), 0)
    col = lax.broadcasted_iota(jnp.int32, (c, c), 1)
    tri = (jnp.where(rev, col - row, row - col) >= 0).astype(BF16)
    srow = lax.broadcasted_iota(jnp.int32, (s_rows, s_rows), 0)
    scol = lax.broadcasted_iota(jnp.int32, (s_rows, s_rows), 1)
    ahead = jnp.where(rev, scol % c - srow % c, srow % c - scol % c)
    strict = ahead > 0
    incl = ahead >= 0
    eye = (srow == scol).astype(F32)
    same_block = lambda size: (srow // size) == (scol // size)
    lrow =lax.broadcasted_iota(jnp.int32, (lw_n, lw_n), 0)
    lcol = lax.broadcasted_iota(jnp.int32, (lw_n, lw_n), 1)
    same_head = (lrow // HEAD) == (lcol // HEAD)
    diag = lrow == lcol
    head_of_lane = lax.broadcasted_iota(jnp.int32, (s_rows, lw_n), 1) // HEAD
    head_of_row = lax.broadcasted_iota(jnp.int32, (s_rows, lw_n), 0) // c
    own = head_of_lane == head_of_row
    n_chunks = TM // c

    def stack(x):
        return jnp.where(own, jnp.concatenate([x] * n_heads, axis=0), 0.0)

    def unstack(xs):
        out = xs[0:c]
        for hh in range(1, n_heads):
            out = out + xs[hh * c:(hh + 1) * c]
        return out

    def local(grp, ci):
        cidx = jnp.where(rev, n_chunks - 1 - ci, ci)
        sl = pl.ds(pl.multiple_of(cidx * c, c), c)
        ls = slice(grp * lw_n, (grp + 1) * lw_n)
        r, kk, v = r_ref[0, sl, ls], kk_ref[0, sl, ls], v_ref[0, sl, ls]
        lw, bb, kd = lw_ref[0, 0, sl, ls], bb_ref[0, 0, sl, ls], kd_ref[0, 0, sl, ls]
        lw_hi, lw_rest = _split(lw)[0], lw - _split(lw)[0].astype(F32)
        lw_mid, lw_lo = _split(lw_rest)
        cum = _dot(tri, lw_hi) + (_dot(tri, lw_mid) + _dot(tri, lw_lo))
        yield
        tot = jnp.sum(lw, axis=0, keepdims=True)
        e_neg = jnp.exp(-cum)
        e_end = jnp.exp(tot - cum)
        kd_g, b_g = kd * e_end, bb * e_end
        kk_s = stack(kk * jnp.exp(cum - lw))
        r_s = stack(r * jnp.exp(cum))
        v_s = stack(v).astype(BF16)
        lhs = jnp.concatenate([kk_s, r_s], axis=0)
        yield
        p1 = _bdot(lhs, stack(kd * e_neg), NT_DIMS)
        yield
        p2 = _bdot(lhs, stack(bb * e_neg), NT_DIMS)
        yield
        a_kd = jnp.where(strict, p1[:s_rows], 0.0)
        b_kd = jnp.where(incl, p1[s_rows:], 0.0)
        nmat = jnp.where(strict, p2[:s_rows], 0.0)
        b_b = jnp.where(incl, p2[s_rows:], 0.0).astype(BF16)
        tinv = eye - jnp.where(same_block(2), nmat, 0.0)
        size = 2
        while size < c:
            n_off = jnp.where(jnp.logical_and(same_block(2 * size), jnp.logical_not(same_block(size))), nmat, 0.0)
            tb = tinv.astype(BF16)
            nt = _bdot(n_off, tb)
            yield
            tinv = tinv - _bdot(tb, nt)
            yield
            size *= 2
        av = _bdot(a_kd, v_s)
        yield
        tw = _bdot(tinv, jnp.concatenate([kk_s, av], axis=1))
        yield
        bw = _bdot(b_b, tw)
        yield
        kkp, u0 = unstack(tw[:, :lw_n]), unstack(tw[:, lw_n:])
        rp = unstack(r_s - bw[:, :lw_n])
        y0 = unstack(_bdot(b_kd, v_s) - bw[:, lw_n:])
        m_mat = jnp.where(diag, jnp.exp(tot), 0.0) - jnp.where(same_head, _bdot(b_g, kkp, TN_DIMS), 0.0)
        g_mat = jnp.where(same_head, _bdot(jnp.concatenate([kd_g, b_g], axis=0),
                                           jnp.concatenate([v, -u0], axis=0), TN_DIMS), 0.0)
        return grp, sl, ls, rp, y0, m_mat, g_mat

    chains = [local(grp, ci) for ci in range(n_chunks) for grp in range(h_ref.shape[0])]
    parts = [None] * len(chains)
    while any(p is None for p in parts):
        for idx, chain in enumerate(chains):
            if parts[idx] is None:
                try:
                    next(chain)
                except StopIteration as done:
                    parts[idx] = done.value
    hs = [h_ref[grp] for grp in range(h_ref.shape[0])]
    for grp, sl, ls, rp, y0, m_mat, g_mat in parts:
        y_ref[0, 0, sl, ls] = _bdot(rp, hs[grp]) + y0
        hs[grp] = _bdot(m_mat, hs[grp]) + g_mat
    for grp, h_new in enumerate(hs):
        h_ref[grp] = h_new


def _bdot(a, b, dims=(((1,), (0,)), ((), ()))):
    return lax.dot_general(a.astype(BF16), b.astype(BF16), dims, preferred_element_type=F32)


def _post_kernel(*refs, odd, ff_chunk):
    if odd:
        (x_ref, y0_ref, y1_ref, gate_ref, z_ref, gng_ref, ones_ref,
         wo_ref, mod_ref, g2_ref, w1_ref, w2_ref, o_ref) = refs
        y = y0_ref[0, 0] + y1_ref[0, 0]
        dlt = y - _group_reduce(y, ones_ref)
        var = _group_reduce(dlt * dlt, ones_ref)
        mix = (dlt * lax.rsqrt(var + GN_EPS) * gng_ref[...] * gate_ref[0] + z_ref[0]).astype(BF16)
    else:
        x_ref, att_ref, rec_ref, wo_ref, mod_ref, g2_ref, w1_ref, w2_ref, o_ref = refs
        mix = jnp.concatenate([att_ref[0], rec_ref[0]], axis=1)
    x1 = x_ref[0] + mod_ref[0, 2:3, :] * _dot(mix, wo_ref[...])
    h2 = _norm_mod(x1, g2_ref[...], mod_ref[0, 3:4, :], mod_ref[0, 4:5, :]).astype(BF16)
    acc = jnp.zeros_like(x1)
    for c in range(w1_ref.shape[1] // ff_chunk):
        a = _dot(h2, w1_ref[:, c * ff_chunk:(c + 1) * ff_chunk])
        a = jnp.square(jnp.maximum(a, 0.0)).astype(BF16)
        acc = acc + _dot(a, w2_ref[c * ff_chunk:(c + 1) * ff_chunk, :])
    o_ref[0] = x1 + mod_ref[0, 5:6, :] * acc


def _blockdiag_ones(width, value):
    idx = jnp.arange(width) // HEAD
    return jnp.where(idx[:, None] == idx[None, :], value, 0.0).astype(BF16)


def _rope_tables(cl, n, grid_w):
    rows = n // grid_w
    row = jnp.repeat(jnp.arange(rows, dtype=F32), grid_w)
    col = jnp.tile(jnp.arange(grid_w, dtype=F32), rows)
    half = HEAD // 2
    inv = ROPE_THETA ** (-jnp.arange(0, half, 2, dtype=F32) / half)
    ang = jnp.concatenate([row[:, None] * inv, col[:, None] * inv], axis=-1)
    cos = jnp.repeat(jnp.cos(ang), 2, axis=-1)
    sin = jnp.repeat(jnp.sin(ang), 2, axis=-1)
    even = (jnp.arange(HEAD) % 2 == 0)[None, :]
    sa = jnp.where(even, -sin, 0.0)
    sb = jnp.where(even, 0.0, sin)
    ctx = lambda fill: jnp.full((cl, HEAD), fill, F32)
    full = lambda lat, fill: jnp.tile(jnp.concatenate([ctx(fill), lat], axis=0), (1, LANES // HEAD))
    return full(cos, 1.0), full(sa, 0.0), full(sb, 0.0)


def _tile_specs(b_all, ct, d_model):
    mod_spec = pl.BlockSpec((1, 6, d_model), lambda b, t: (jnp.where(t < ct, b_all, b), 0, 0))
    row_spec = lambda w: pl.BlockSpec((1, TM, w), lambda b, t: (b, t, 0))
    return mod_spec, row_spec


def _hybrid_layer(xs, mods, g1, w_in, w_out, qn, kn, conv_w, conv_b, gate_w, gate_b, lam, ropes, dims):
    b_all, cl, n, d_model = dims
    t_all = cl + n
    ct, nt = cl // TM, t_all // TM
    in_w = w_in.shape[1]
    rnn_w = conv_w.shape[1]
    kv_w = (in_w - 2 * rnn_w - d_model // 2) // 2
    q_w = in_w - 2 * kv_w - 2 * rnn_w
    n_kv = kv_w // HEAD
    assert ct % 2 == 1 and nt % 2 == 1, "the attention kernel walks key tiles in pairs plus one"
    groups = q_w // kv_w
    mod_spec, row_spec = _tile_specs(b_all, ct, d_model)
    rope_spec = pl.BlockSpec((TM, LANES), lambda b, t: (t, 0))
    cos, sa, sb = ropes

    q, k, vt, xr, gl = pl.pallas_call(
        functools.partial(_hy_in_kernel, q_w=q_w, kv_w=kv_w, rnn_w=rnn_w),
        grid=(b_all, nt),
        in_specs=[row_spec(d_model), mod_spec, _const_spec((1, d_model)), _const_spec((d_model, in_w)),
                  rope_spec, rope_spec, rope_spec, _const_spec((1, q_w)), _const_spec((1, kv_w)),
                  _const_spec((MXU, MXU)), _const_spec((kv_w, kv_w))],
        out_specs=[row_spec(q_w), row_spec(kv_w),
                   pl.BlockSpec((1, n_kv, HEAD + V_ONES, TM), lambda b, t: (b, 0, 0, t)),
                   row_spec(rnn_w), row_spec(rnn_w)],
        out_shape=[jax.ShapeDtypeStruct((b_all, t_all, q_w), BF16),
                   jax.ShapeDtypeStruct((b_all, t_all, kv_w), BF16),
                   jax.ShapeDtypeStruct((b_all, n_kv, HEAD + V_ONES, t_all), BF16),
                   jax.ShapeDtypeStruct((b_all, t_all, rnn_w), F32),
                   jax.ShapeDtypeStruct((b_all, t_all, rnn_w), F32)],
        compiler_params=_params(("parallel", "parallel")),
    )(xs, mods, g1.reshape(1, d_model), w_in.astype(BF16), cos, sa, sb,
      jnp.tile(qn, q_w // HEAD).reshape(1, q_w), jnp.tile(kn, kv_w // HEAD).reshape(1, kv_w),
      _blockdiag_ones(MXU, 1.0 / HEAD), _blockdiag_ones(kv_w, 1.0 / HEAD))

    att = pl.pallas_call(
        functools.partial(_attn_kernel, ct=ct, nt=nt, groups=groups, n_kv=n_kv),
        grid=(b_all, nt),
        in_specs=[row_spec(q_w),
                  pl.BlockSpec((1, t_all, kv_w), lambda b, t: (b, 0, 0)),
                  pl.BlockSpec((1, n_kv, HEAD + V_ONES, t_all), lambda b, t: (b, 0, 0, 0))],
        out_specs=row_spec(q_w),
        out_shape=jax.ShapeDtypeStruct((b_all, t_all, q_w), BF16),
        scratch_shapes=[pltpu.VMEM((q_w, TM), F32)] + [pltpu.VMEM((q_w // HEAD, TM, TM), F32)] * 2,
        compiler_params=_params(("parallel", "parallel")),
    )(q, k, vt)

    n_lc = rnn_w // RG_LANES
    per = RG_LANES // HEAD
    gw = gate_w.reshape(4, n_lc, per, HEAD, HEAD)
    eye = jnp.eye(per, dtype=F32)
    gw = jnp.einsum('gcpde,pq->cgpdqe', gw, eye).reshape(n_lc, 4, RG_LANES, RG_LANES).astype(BF16)
    lane_spec = lambda rows: pl.BlockSpec((rows, RG_LANES), lambda b, c: (0, c))
    seq_spec = pl.BlockSpec((1, t_all, RG_LANES), lambda b, c: (b, 0, c))
    rec = pl.pallas_call(
        functools.partial(_rglru_kernel, cl=cl, n=n),
        grid=(b_all, n_lc),
        in_specs=[seq_spec, seq_spec, lane_spec(CONV_K), lane_spec(1),
                  pl.BlockSpec((1, 4, RG_LANES, RG_LANES), lambda b, c: (c, 0, 0, 0)),
                  lane_spec(4), lane_spec(2)],
        out_specs=seq_spec,
        out_shape=jax.ShapeDtypeStruct((b_all, t_all, rnn_w), BF16),
        scratch_shapes=[pltpu.VMEM((t_all + 2 * SUBLANES, RG_LANES), F32),
                        pltpu.VMEM((2, t_all, RG_LANES), F32),
                        pltpu.VMEM((2, t_all, RG_LANES), F32)],
        compiler_params=_params(("parallel", "parallel")),
    )(xr, gl, conv_w, conv_b.reshape(1, rnn_w), gw, gate_b.reshape(4, rnn_w), lam)
    return (att, rec), w_out.astype(BF16)


def _rwkv_layer(xs, mods, g1, mu, w_rkv, lora_down, lora_up, lora_bias, gate_down, gate_up,
                k_k, k_a, r_k, gn_b, dims):
    b_all, cl, n, d_model = dims
    t_all = cl + n
    ct, nt = cl // TM, t_all // TM
    n8 = t_all // SUBLANES
    per8 = TM // SUBLANES
    lora = lora_down.shape[-1]
    glora = gate_down.shape[-1]
    mod_spec, row_spec = _tile_specs(b_all, ct, d_model)
    dir_spec = pl.BlockSpec((2, 1, TM, d_model), lambda b, t: (0, b, t, 0))
    wdw = jnp.concatenate([lora_down[0, 0], lora_down[1, 0]], axis=1).astype(BF16)
    wda = jnp.concatenate([lora_down[0, 1], lora_down[1, 1]], axis=1).astype(BF16)
    zeros = jnp.zeros((lora, d_model), F32)
    pad_up = lambda j: jnp.stack([jnp.concatenate([lora_up[0, j], zeros], axis=0),
                                  jnp.concatenate([zeros, lora_up[1, j]], axis=0)]).astype(BF16)
    vec = lambda a: a.reshape(1, d_model)
    outs = pl.pallas_call(
        functools.partial(_rw_in_kernel, ct=ct, nt=nt),
        grid=(b_all, nt),
        in_specs=[row_spec(d_model),
                  pl.BlockSpec((1, SUBLANES, d_model), lambda b, t: (b, jnp.maximum(t * per8 - 1, 0), 0)),
                  pl.BlockSpec((1, SUBLANES, d_model), lambda b, t: (b, jnp.minimum((t + 1) * per8, n8 - 1), 0)),
                  mod_spec, _const_spec((1, d_model)), _const_spec((6, d_model)),
                  _const_spec((3, d_model, d_model)), _const_spec((d_model, 2 * lora)),
                  _const_spec((d_model, 2 * lora)), _const_spec((2, 2 * lora, d_model)),
                  _const_spec((2, 2 * lora, d_model)), _const_spec((4, d_model)),
                  _const_spec((d_model, glora)), _const_spec((glora, d_model)),
                  _const_spec((1, d_model)), _const_spec((1, d_model)), _const_spec((1, d_model)),
                  _const_spec((1, d_model)), _const_spec((MXU, MXU))],
        out_specs=[row_spec(d_model), row_spec(d_model), row_spec(d_model),
                   dir_spec, dir_spec, dir_spec, row_spec(d_model), row_spec(d_model)],
        out_shape=[jax.ShapeDtypeStruct((b_all, t_all, d_model), F32)] * 3
        + [jax.ShapeDtypeStruct((2, b_all, t_all, d_model), F32)] * 3
        + [jax.ShapeDtypeStruct((b_all, t_all, d_model), F32)] * 2,
        compiler_params=_params(("parallel", "parallel")),
    )(xs, xs, xs, mods, vec(g1), mu, w_rkv.astype(BF16), wdw, wda, pad_up(0), pad_up(1),
      lora_bias.reshape(4, d_model), gate_down.astype(BF16), gate_up.astype(BF16),
      vec(k_k), vec(k_a), vec(r_k), vec(gn_b), _blockdiag_ones(MXU, 1.0))
    r, kk, v, lw, bb, kd, gate, z = outs

    def tmap(dd, i):
        rev = jnp.where(i < ct, ct - 1 - i, nt - 1 - (i - ct))
        return jnp.where(dd == 0, i, rev)

    blk_lanes = WKV_LANES * WKV_GROUPS
    shared = pl.BlockSpec((1, TM, blk_lanes), lambda dd, b, hh, i: (b, tmap(dd, i), hh))
    per_dir = pl.BlockSpec((1, 1, TM, blk_lanes), lambda dd, b, hh, i: (dd, b, tmap(dd, i), hh))
    y = pl.pallas_call(
        _wkv_kernel,
        grid=(2, b_all, d_model // blk_lanes, nt),
        in_specs=[shared, shared, shared, per_dir, per_dir, per_dir],
        out_specs=per_dir,
        out_shape=jax.ShapeDtypeStruct((2, b_all, t_all, d_model), F32),
        scratch_shapes=[pltpu.VMEM((WKV_GROUPS, WKV_LANES, WKV_LANES), F32)],
        compiler_params=_params(("arbitrary", "arbitrary", "arbitrary", "arbitrary")),
    )(r, kk, v, lw, bb, kd)
    return y, gate, z


def _post(xs, mix_inputs, w_o, mods, g2, w1, w2, dims, odd, gn_g=None, latent_only=False):
    b_all, cl, n, d_model = dims
    t_all = cl + n
    ct, nt = cl // TM, t_all // TM
    t0 = ct if latent_only else 0
    d_ff = w1.shape[1]
    mod_spec = pl.BlockSpec((1, 6, d_model), lambda b, t: (jnp.where(t + t0 < ct, b_all, b), 0, 0))
    row_spec = lambda w: pl.BlockSpec((1, TM, w), lambda b, t: (b, t + t0, 0))
    tail_specs = [_const_spec((d_model, d_model)), mod_spec, _const_spec((1, d_model)),
                  _const_spec((d_model, d_ff)), _const_spec((d_ff, d_model))]
    tail_args = (w_o, mods, g2.reshape(1, d_model), w1.astype(BF16), w2.astype(BF16))
    if odd:
        y, gate, z = mix_inputs
        y_spec = lambda dd: pl.BlockSpec((1, 1, TM, d_model), lambda b, t: (dd, b, t + t0, 0))
        in_specs = [row_spec(d_model), y_spec(0), y_spec(1), row_spec(d_model), row_spec(d_model),
                    _const_spec((1, d_model)), _const_spec((MXU, MXU))] + tail_specs
        args = (xs, y, y, gate, z, gn_g.reshape(1, d_model), _blockdiag_ones(MXU, 1.0 / HEAD)) + tail_args
    else:
        att, rec = mix_inputs
        in_specs = [row_spec(d_model), row_spec(att.shape[-1]), row_spec(rec.shape[-1])] + tail_specs
        args = (xs, att, rec) + tail_args
    rows_out = n if latent_only else t_all
    return pl.pallas_call(
        functools.partial(_post_kernel, odd=odd, ff_chunk=min(d_ff, 4 * MXU)),
        grid=(b_all, nt - t0),
        in_specs=in_specs,
        out_specs=pl.BlockSpec((1, TM, d_model), lambda b, t: (b, t, 0)),
        out_shape=jax.ShapeDtypeStruct((b_all, rows_out, d_model), F32),
        compiler_params=_params(("parallel", "parallel")),
    )(*args)


def kernel(x, c, ctx, c_ctx, ada_w, ada_b, norm_g, mlp_w1, mlp_w2, hy_w_in, hy_w_out, hy_q_norm, hy_k_norm, hy_conv_w, hy_conv_b, hy_gate_w, hy_gate_b, hy_lam, rw_mu, rw_w_rkv, rw_w_o, rw_lora_down, rw_lora_up, rw_lora_bias, rw_gate_down, rw_gate_up, rw_k_k, rw_k_a, rw_r_k, rw_gn_g, rw_gn_b):
    b_all, n, d_model = x.shape
    cl = ctx.shape[1]
    depth = ada_w.shape[0]
    assert cl % TM == 0 and n % TM == 0 and d_model % MXU == 0
    grid_w = 64
    dims = (b_all, cl, n, d_model)
    mods_all = _ada_mods(jnp.concatenate([c, c_ctx[None, :]], axis=0), ada_w, ada_b)
    ropes = _rope_tables(cl, n, grid_w)
    xs = jnp.concatenate([ctx, x], axis=1)
    for l in range(depth):
        i = l // 2
        mods = mods_all[l]
        last = l == depth - 1
        if l % 2 == 0:
            mix, w_o = _hybrid_layer(xs, mods, norm_g[l, 0], hy_w_in[i], hy_w_out[i], hy_q_norm[i], hy_k_norm[i],
                                     hy_conv_w[i], hy_conv_b[i], hy_gate_w[i], hy_gate_b[i], hy_lam[i], ropes, dims)
            xs = _post(xs, mix, w_o, mods, norm_g[l, 1], mlp_w1[l], mlp_w2[l], dims, odd=False, latent_only=last)
        else:
            mix = _rwkv_layer(xs, mods, norm_g[l, 0], rw_mu[i], rw_w_rkv[i], rw_lora_down[i], rw_lora_up[i],
                              rw_lora_bias[i], rw_gate_down[i], rw_gate_up[i], rw_k_k[i], rw_k_a[i],
                              rw_r_k[i], rw_gn_b[i], dims)
            xs = _post(xs, mix, rw_w_o[i].astype(BF16), mods, norm_g[l, 1], mlp_w1[l], mlp_w2[l], dims,
                       odd=True, gn_g=rw_gn_g[i], latent_only=last)
    return xs if xs.shape[1] == n else xs[:, cl:]
```

```python
import functools
import math

import jax
import jax.numpy as jnp
from jax import lax
from jax.experimental import pallas as pl
from jax.experimental.pallas import tpu as pltpu

F32 = jnp.float32
BF16 = jnp.bfloat16

HEAD = 64
LANES = 128
SUBLANES = 8
MXU = 256
TM = 256
VMEM_LIMIT = 56 * 1024 * 1024

EPS = 1e-6
GN_EPS = 64e-5
RG_C = 8.0
ROPE_THETA = 10000.0
GRID_W = 64
DECAY_SCALE = math.exp(-0.5)
CONV_K = 4
CONV_LEFT = 2
Q_SCALE = HEAD ** -0.5 * math.log2(math.e)
V_ONES = 16
RG_LANES = 256
WKV_CHUNK = 64
WKV_LANES = 256
WKV_GROUPS = 4

NT_DIMS = (((1,), (1,)), ((), ()))
TN_DIMS = (((0,), (0,)), ((), ()))


def _params(sem):
    return pltpu.CompilerParams(dimension_semantics=sem, vmem_limit_bytes=VMEM_LIMIT)


def _const_spec(shape):
    nd = len(shape)
    return pl.BlockSpec(shape, lambda *_: (0,) * nd, pipeline_mode=pl.Buffered(1))


def _dot(a, b):
    return jnp.dot(a, b, preferred_element_type=F32)


def _dot_exact(a, b):
    return jnp.dot(a, b, preferred_element_type=F32, precision=lax.Precision.HIGHEST)


def _split(x):
    hi = x.astype(BF16)
    lo = (x - hi.astype(F32)).astype(BF16)
    return hi, lo


def _group_reduce(x, ones_ref):
    cw = ones_ref.shape[0]
    ones = ones_ref[...]
    outs = []
    for c in range(x.shape[1] // cw):
        hi, lo = _split(x[:, c * cw:(c + 1) * cw])
        outs.append(_dot(hi, ones) + _dot(lo, ones))
    return outs[0] if len(outs) == 1 else jnp.concatenate(outs, axis=1)


def _sigmoid(x):
    return 0.5 * jnp.tanh(0.5 * x) + 0.5


def _norm_mod(x, g, shift, scale):
    ms = jnp.mean(x * x, axis=-1, keepdims=True)
    return x * lax.rsqrt(ms + EPS) * (g * (1.0 + scale)) + shift


def _ada_kernel(c_ref, w_ref, b_ref, o_ref):
    c = c_ref[...]
    s = c * _sigmoid(c)
    o_ref[0, 0] = _dot_exact(s, w_ref[0]) + b_ref[0]


def _ada_mods(cc, ada_w, ada_b):
    depth, d, _ = ada_w.shape
    rows = cc.shape[0]
    out = pl.pallas_call(
        _ada_kernel,
        grid=(depth, 6),
        in_specs=[pl.BlockSpec((rows, d), lambda l, j: (0, 0)),
                  pl.BlockSpec((1, d, d), lambda l, j: (l, 0, j)),
                  pl.BlockSpec((1, 1, d), lambda l, j: (l * 6 + j, 0, 0))],
        out_specs=pl.BlockSpec((1, 1, rows, d), lambda l, j: (l, j, 0, 0)),
        out_shape=jax.ShapeDtypeStruct((depth, 6, rows, d), F32),
        compiler_params=_params(("arbitrary", "arbitrary")),
    )(cc, ada_w, ada_b.reshape(depth * 6, 1, d))
    return jnp.transpose(out, (0, 2, 1, 3))


def _hy_in_kernel(x_ref, mod_ref, g_ref, w_ref, cos_ref, sa_ref, sb_ref, qg_ref, kg_ref, oq_ref, ok_ref,
                  q_out, k_out, vt_out, xr_out, gl_out, *, q_w, kv_w, rnn_w):
    x = x_ref[0]
    h = _norm_mod(x, g_ref[...], mod_ref[0, 0:1, :], mod_ref[0, 1:2, :]).astype(BF16)
    z = _dot(h, w_ref[...])
    c0, c1, c2, c3 = q_w, q_w + kv_w, q_w + 2 * kv_w, q_w + 2 * kv_w + rnn_w

    def norm_rope(u, gain, ones_ref):
        width = u.shape[1]
        un = u * lax.rsqrt(_group_reduce(u * u, ones_ref) + EPS) * gain
        reps = width // LANES
        tile = lambda r: jnp.concatenate([r[...]] * reps, axis=1) if reps > 1 else r[...]
        nxt = pltpu.roll(un, width - 1, 1)
        prv = pltpu.roll(un, 1, 1)
        return un * tile(cos_ref) + nxt * tile(sa_ref) + prv * tile(sb_ref)

    q_out[0] = (norm_rope(z[:, :c0], qg_ref[...], oq_ref) * Q_SCALE).astype(BF16)
    k_out[0] = norm_rope(z[:, c0:c1], kg_ref[...], ok_ref).astype(BF16)
    vt = z[:, c1:c2].T
    ones = jnp.ones((V_ONES, TM), F32)
    for hk in range(kv_w // HEAD):
        vt_out[0, hk] = jnp.concatenate([vt[hk * HEAD:(hk + 1) * HEAD], ones], axis=0).astype(BF16)
    xr_out[0] = z[:, c2:c3]
    gl_out[0] = z[:, c3:]


def _attn_kernel(q_ref, k_ref, vt_ref, o_ref, ot_ref, sa_ref, sb_ref, *, ct, nt, groups, n_kv):
    t = pl.program_id(1)
    nkv = jnp.where(t < ct, ct, nt)
    q = q_ref[0].astype(F32)
    lane = lax.broadcasted_iota(jnp.int32, (TM, LANES), 1)
    low = lane < HEAD
    blocks = []
    for head in range(groups * n_kv):
        hk = head // groups
        qc = q[:, LANES * (head // 2):LANES * (head // 2 + 1)]
        if head % 2 != hk % 2:
            qc = pltpu.roll(qc, HEAD, 1)
        blocks.append((hk, jnp.where(low if hk % 2 == 0 else jnp.logical_not(low), qc, 0.0).astype(BF16)))

    def scores(i, dst_ref, heads):
        r0 = pl.multiple_of(i * TM, TM)
        for head in heads:
            hk, qm = blocks[head]
            kc = k_ref[0, pl.ds(r0, TM), LANES * (hk // 2):LANES * (hk // 2 + 1)]
            dst_ref[head] = lax.dot_general(kc, qm, NT_DIMS, preferred_element_type=F32)

    def absorb(i, src_ref, stats, nxt=None):
        r0 = pl.multiple_of(i * TM, TM)
        out = []
        for head, ((hk, _), (m, acc)) in enumerate(zip(blocks, stats)):
            if nxt is not None:
                scores(nxt[0], nxt[1], [head])
            st = src_ref[head]
            m_new = jnp.maximum(m, jnp.max(st, axis=0, keepdims=True))
            out.append((m_new, _dot(vt_ref[0, hk, :, pl.ds(r0, TM)], jnp.exp2(st - m_new).astype(BF16))))
        return tuple((m_new, jnp.exp2(m - m_new) * acc + pv) for (m_new, pv), (m, acc) in zip(out, stats))

    def pair(j, stats):
        stats = absorb(2 * j, sa_ref, stats, (2 * j + 1, sb_ref))
        return absorb(2 * j + 1, sb_ref, stats, (2 * j + 2, sa_ref))

    scores(0, sa_ref, range(len(blocks)))
    init = tuple((jnp.full((1, TM), -1e30, F32), jnp.zeros((vt_ref.shape[2], TM), F32)) for _ in blocks)
    res = absorb(nkv - 1, sa_ref, lax.fori_loop(0, (nkv - 1) // 2, pair, init))
    for head, (_, acc) in enumerate(res):
        ot_ref[HEAD * head:HEAD * (head + 1), :] = acc[:HEAD] / acc[HEAD:HEAD + 1]
    o_ref[0] = ot_ref[...].T.astype(BF16)


def _scan8(a, u, reverse):
    row = lax.broadcasted_iota(jnp.int32, a.shape, 0)
    for d in (1, 2, 4):
        if reverse:
            a_s, u_s, ok = pltpu.roll(a, SUBLANES - d, 0), pltpu.roll(u, SUBLANES - d, 0), row < SUBLANES - d
        else:
            a_s, u_s, ok = pltpu.roll(a, d, 0), pltpu.roll(u, d, 0), row >= d
        u = a * jnp.where(ok, u_s, 0.0) + u
        a = a * jnp.where(ok, a_s, 1.0)
    return a, u


def _rglru_kernel(xr_ref, gl_ref, cw_ref, cb_ref, gw_ref, gb_ref, lam_ref, o_ref,
                  xs_ref, a_ref, u_ref, *, cl, n):
    t_all = cl + n
    pad = SUBLANES
    width = xs_ref.shape[1]
    xs_ref[0:pad, :] = jnp.zeros((pad, width), F32)
    xs_ref[pad + t_all:, :] = jnp.zeros((pad, width), F32)
    xs_ref[pad:pad + t_all, :] = xr_ref[0]
    lam = lam_ref[...]
    z = -lam
    softplus = jnp.maximum(z, 0.0) + jnp.log(1.0 + jnp.exp(-jnp.abs(z)))
    cw = cw_ref[...]
    cb = cb_ref[...]
    gb = gb_ref[...]

    def coeffs(i, _):
        r0 = pl.multiple_of(i * TM, TM)
        blk = xs_ref[pl.ds(r0, TM + 2 * pad), :]
        rows = r0 + lax.broadcasted_iota(jnp.int32, (TM, 1), 0)
        in_lat = rows >= cl
        pos = jnp.where(in_lat, rows - cl, rows)
        seqlen = jnp.where(in_lat, n, cl)
        xc = jnp.zeros((TM, width), F32) + cb
        for j in range(CONV_K):
            off = j - CONV_LEFT
            tap = blk[pad + off:pad + off + TM, :]
            ok = jnp.logical_and(pos + off >= 0, pos + off < seqlen)
            xc = xc + jnp.where(ok, tap, 0.0) * cw[j:j + 1, :]
        xcb = xc.astype(BF16)
        for d in range(2):
            r = _sigmoid(_dot(xcb, gw_ref[0, 2 * d]) + gb[2 * d:2 * d + 1, :])
            ig = _sigmoid(_dot(xcb, gw_ref[0, 2 * d + 1]) + gb[2 * d + 1:2 * d + 2, :])
            a = jnp.exp(-RG_C * r * softplus[d:d + 1, :])
            a_ref[d, pl.ds(r0, TM), :] = a
            u_ref[d, pl.ds(r0, TM), :] = jnp.sqrt(1.0 - a * a) * (ig * xc)
        return 0

    lax.fori_loop(0, t_all // TM, coeffs, 0)

    g_all, g_ctx = t_all // SUBLANES, cl // SUBLANES

    def step(i, carry):
        h_fwd, h_rev = carry
        g_rev = jnp.where(i < g_ctx, g_ctx - 1 - i, g_all - 1 - (i - g_ctx))
        sf = pl.ds(pl.multiple_of(i * SUBLANES, SUBLANES), SUBLANES)
        sr = pl.ds(pl.multiple_of(g_rev * SUBLANES, SUBLANES), SUBLANES)
        af, uf = _scan8(a_ref[0, sf, :], u_ref[0, sf, :], False)
        ar, ur = _scan8(a_ref[1, sr, :], u_ref[1, sr, :], True)
        hf = af * h_fwd + uf
        hr = ar * h_rev + ur
        u_ref[0, sf, :] = hf
        u_ref[1, sr, :] = hr
        return hf[SUBLANES - 1:SUBLANES, :], hr[0:1, :]

    zero = jnp.zeros((1, width), F32)
    lax.fori_loop(0, g_all, step, (zero, zero), unroll=4)

    def combine(i, _):
        r0 = pl.multiple_of(i * TM, TM)
        hsum = u_ref[0, pl.ds(r0, TM), :] + u_ref[1, pl.ds(r0, TM), :]
        o_ref[0, pl.ds(r0, TM), :] = (jax.nn.gelu(gl_ref[0, pl.ds(r0, TM), :]) * hsum).astype(BF16)
        return 0

    lax.fori_loop(0, t_all // TM, combine, 0)


def _rw_in_kernel(x_ref, xp_ref, xn_ref, mod_ref, g_ref, mu_ref, wrkv_ref, wdw_ref, wda_ref, wuw_ref, wua_ref,
                  lb_ref, gd_ref, gu_ref, kk_ref, ka_ref, rk_ref, gnb_ref, ones_ref,
                  r_out, kk_out, v_out, lw_out, bb_out, kd_out, g_out, z_out, *, ct, nt):
    t = pl.program_id(1)
    g = g_ref[...]
    shift, scale = mod_ref[0, 0:1, :], mod_ref[0, 1:2, :]
    h = _norm_mod(x_ref[0], g, shift, scale)
    first = jnp.logical_or(t == 0, t == ct)
    last = jnp.logical_or(t == ct - 1, t == nt - 1)
    hp = _norm_mod(xp_ref[0], g, shift, scale)[SUBLANES - 1:SUBLANES, :]
    hn = _norm_mod(xn_ref[0], g, shift, scale)[0:1, :]
    hp = jnp.where(first, 0.0, hp)
    hn = jnp.where(last, 0.0, hn)
    row = lax.broadcasted_iota(jnp.int32, (TM, 1), 0)
    h_prev = jnp.where(row == 0, hp, pltpu.roll(h, 1, 0))
    h_next = jnp.where(row == TM - 1, hn, pltpu.roll(h, TM - 1, 0))
    xx = 0.5 * (h_prev + h_next) - h
    lerp = lambda j: (h + xx * mu_ref[j:j + 1, :]).astype(BF16)

    r = _dot(lerp(0), wrkv_ref[0])
    k = _dot(lerp(2), wrkv_ref[1])
    v = _dot(lerp(3), wrkv_ref[2])
    gate = _dot(_sigmoid(_dot(lerp(5), gd_ref[...])).astype(BF16), gu_ref[...])
    tw = jnp.tanh(_dot(lerp(1), wdw_ref[...])).astype(BF16)
    ta = _dot(lerp(4), wda_ref[...]).astype(BF16)

    kk = k * kk_ref[...]
    nrm = jnp.sqrt(_group_reduce(kk * kk, ones_ref))
    kk = kk / jnp.maximum(nrm, 1e-12)
    r_out[0] = r
    kk_out[0] = kk
    v_out[0] = v
    kd_sum = jnp.zeros_like(k)
    for d in range(2):
        dec = lb_ref[2 * d:2 * d + 1, :] + _dot(tw, wuw_ref[d])
        a = _sigmoid(lb_ref[2 * d + 1:2 * d + 2, :] + _dot(ta, wua_ref[d]))
        kd = k * (1.0 + (a - 1.0) * ka_ref[...])
        lw_out[d, 0] = -DECAY_SCALE * _sigmoid(dec)
        bb_out[d, 0] = kk * a
        kd_out[d, 0] = kd
        kd_sum = kd_sum + kd
    bonus = _group_reduce(r * kd_sum * rk_ref[...], ones_ref)
    g_out[0] = gate
    z_out[0] = (gnb_ref[...] + bonus * v) * gate


def _wkv_kernel(r_ref, kk_ref, v_ref, lw_ref, bb_ref, kd_ref, y_ref, h_ref):
    c, lw_n = WKV_CHUNK, WKV_LANES
    n_heads = lw_n // HEAD
    s_rows = n_heads * c
    d = pl.program_id(0)
    i = pl.program_id(3)

    @pl.when(i == 0)
    def _():
        h_ref[...] = jnp.zeros_like(h_ref)

    rev = d == 1
    row = lax.broadcasted_iota(jnp.int32, (c, c), 0)
    col = lax.broadcasted_iota(jnp.int32, (c, c), 1)
    tri = (jnp.where(rev, col - row, row - col) >= 0).astype(BF16)
    srow = lax.broadcasted_iota(jnp.int32, (s_rows, s_rows), 0)
    scol = lax.broadcasted_iota(jnp.int32, (s_rows, s_rows), 1)
    ahead = jnp.where(rev, scol % c - srow % c, srow % c - scol % c)
    strict = ahead > 0
    incl = ahead >= 0
    eye = (srow == scol).astype(F32)
    same_block = lambda size: (srow // size) == (scol // size)
    lrow =lax.broadcasted_iota(jnp.int32, (lw_n, lw_n), 0)
    lcol = lax.broadcasted_iota(jnp.int32, (lw_n, lw_n), 1)
    same_head = (lrow // HEAD) == (lcol // HEAD)
    diag = lrow == lcol
    head_of_lane = lax.broadcasted_iota(jnp.int32, (s_rows, lw_n), 1) // HEAD
    head_of_row = lax.broadcasted_iota(jnp.int32, (s_rows, lw_n), 0) // c
    own = head_of_lane == head_of_row
    n_chunks = TM // c

    def stack(x):
        return jnp.where(own, jnp.concatenate([x] * n_heads, axis=0), 0.0)

    def unstack(xs):
        out = xs[0:c]
        for hh in range(1, n_heads):
            out = out + xs[hh * c:(hh + 1) * c]
        return out

    def local(grp, ci):
        cidx = jnp.where(rev, n_chunks - 1 - ci, ci)
        sl = pl.ds(pl.multiple_of(cidx * c, c), c)
        ls = slice(grp * lw_n, (grp + 1) * lw_n)
        r, kk, v = r_ref[0, sl, ls], kk_ref[0, sl, ls], v_ref[0, sl, ls]
        lw, bb, kd = lw_ref[0, 0, sl, ls], bb_ref[0, 0, sl, ls], kd_ref[0, 0, sl, ls]
        lw_hi, lw_rest = _split(lw)[0], lw - _split(lw)[0].astype(F32)
        lw_mid, lw_lo = _split(lw_rest)
        cum = _dot(tri, lw_hi) + (_dot(tri, lw_mid) + _dot(tri, lw_lo))
        yield
        tot = jnp.sum(lw, axis=0, keepdims=True)
        e_neg = jnp.exp(-cum)
        e_end = jnp.exp(tot - cum)
        kd_g, b_g = kd * e_end, bb * e_end
        kk_s = stack(kk * jnp.exp(cum - lw))
        r_s = stack(r * jnp.exp(cum))
        v_s = stack(v).astype(BF16)
        lhs = jnp.concatenate([kk_s, r_s], axis=0)
        yield
        p1 = _bdot(lhs, stack(kd * e_neg), NT_DIMS)
        yield
        p2 = _bdot(lhs, stack(bb * e_neg), NT_DIMS)
        yield
        a_kd = jnp.where(strict, p1[:s_rows], 0.0)
        b_kd = jnp.where(incl, p1[s_rows:], 0.0)
        nmat = jnp.where(strict, p2[:s_rows], 0.0)
        b_b = jnp.where(incl, p2[s_rows:], 0.0).astype(BF16)
        tinv = eye - jnp.where(same_block(2), nmat, 0.0)
        size = 2
        while size < c:
            n_off = jnp.where(jnp.logical_and(same_block(2 * size), jnp.logical_not(same_block(size))), nmat, 0.0)
            tb = tinv.astype(BF16)
            nt = _bdot(n_off, tb)
            yield
            tinv = tinv - _bdot(tb, nt)
            yield
            size *= 2
        av = _bdot(a_kd, v_s)
        yield
        tw = _bdot(tinv, jnp.concatenate([kk_s, av], axis=1))
        yield
        bw = _bdot(b_b, tw)
        yield
        kkp, u0 = unstack(tw[:, :lw_n]), unstack(tw[:, lw_n:])
        rp = unstack(r_s - bw[:, :lw_n])
        y0 = unstack(_bdot(b_kd, v_s) - bw[:, lw_n:])
        m_mat = jnp.where(diag, jnp.exp(tot), 0.0) - jnp.where(same_head, _bdot(b_g, kkp, TN_DIMS), 0.0)
        g_mat = jnp.where(same_head, _bdot(jnp.concatenate([kd_g, b_g], axis=0),
                                           jnp.concatenate([v, -u0], axis=0), TN_DIMS), 0.0)
        return grp, sl, ls, rp, y0, m_mat, g_mat

    chains = [local(grp, ci) for ci in range(n_chunks) for grp in range(h_ref.shape[0])]
    parts = [None] * len(chains)
    while any(p is None for p in parts):
        for idx, chain in enumerate(chains):
            if parts[idx] is None:
                try:
                    next(chain)
                except StopIteration as done:
                    parts[idx] = done.value
    hs = [h_ref[grp] for grp in range(h_ref.shape[0])]
    for grp, sl, ls, rp, y0, m_mat, g_mat in parts:
        y_ref[0, 0, sl, ls] = _bdot(rp, hs[grp]) + y0
        hs[grp] = _bdot(m_mat, hs[grp]) + g_mat
    for grp, h_new in enumerate(hs):
        h_ref[grp] = h_new


def _bdot(a, b, dims=(((1,), (0,)), ((), ()))):
    return lax.dot_general(a.astype(BF16), b.astype(BF16), dims, preferred_element_type=F32)


def _post_kernel(*refs, odd, ff_chunk):
    if odd:
        (x_ref, y0_ref, y1_ref, gate_ref, z_ref, gng_ref, ones_ref,
         wo_ref, mod_ref, g2_ref, w1_ref, w2_ref, o_ref) = refs
        y = y0_ref[0, 0] + y1_ref[0, 0]
        dlt = y - _group_reduce(y, ones_ref)
        var = _group_reduce(dlt * dlt, ones_ref)
        mix = (dlt * lax.rsqrt(var + GN_EPS) * gng_ref[...] * gate_ref[0] + z_ref[0]).astype(BF16)
    else:
        x_ref, att_ref, rec_ref, wo_ref, mod_ref, g2_ref, w1_ref, w2_ref, o_ref = refs
        mix = jnp.concatenate([att_ref[0], rec_ref[0]], axis=1)
    x1 = x_ref[0] + mod_ref[0, 2:3, :] * _dot(mix, wo_ref[...])
    h2 = _norm_mod(x1, g2_ref[...], mod_ref[0, 3:4, :], mod_ref[0, 4:5, :]).astype(BF16)
    acc = jnp.zeros_like(x1)
    for c in range(w1_ref.shape[1] // ff_chunk):
        a = _dot(h2, w1_ref[:, c * ff_chunk:(c + 1) * ff_chunk])
        a = jnp.square(jnp.maximum(a, 0.0)).astype(BF16)
        acc = acc + _dot(a, w2_ref[c * ff_chunk:(c + 1) * ff_chunk, :])
    o_ref[0] = x1 + mod_ref[0, 5:6, :] * acc


def _blockdiag_ones(width, value):
    idx = jnp.arange(width) // HEAD
    return jnp.where(idx[:, None] == idx[None, :], value, 0.0).astype(BF16)


def _rope_tables(cl, n, grid_w):
    rows = n // grid_w
    row = jnp.repeat(jnp.arange(rows, dtype=F32), grid_w)
    col = jnp.tile(jnp.arange(grid_w, dtype=F32), rows)
    half = HEAD // 2
    inv = ROPE_THETA ** (-jnp.arange(0, half, 2, dtype=F32) / half)
    ang = jnp.concatenate([row[:, None] * inv, col[:, None] * inv], axis=-1)
    cos = jnp.repeat(jnp.cos(ang), 2, axis=-1)
    sin = jnp.repeat(jnp.sin(ang), 2, axis=-1)
    even = (jnp.arange(HEAD) % 2 == 0)[None, :]
    sa = jnp.where(even, -sin, 0.0)
    sb = jnp.where(even, 0.0, sin)
    ctx = lambda fill: jnp.full((cl, HEAD), fill, F32)
    full = lambda lat, fill: jnp.tile(jnp.concatenate([ctx(fill), lat], axis=0), (1, LANES // HEAD))
    return full(cos, 1.0), full(sa, 0.0), full(sb, 0.0)


def _tile_specs(b_all, ct, d_model):
    mod_spec = pl.BlockSpec((1, 6, d_model), lambda b, t: (jnp.where(t < ct, b_all, b), 0, 0))
    row_spec = lambda w: pl.BlockSpec((1, TM, w), lambda b, t: (b, t, 0))
    return mod_spec, row_spec


def _hybrid_layer(xs, mods, g1, w_in, w_out, qn, kn, conv_w, conv_b, gate_w, gate_b, lam, ropes, dims):
    b_all, cl, n, d_model = dims
    t_all = cl + n
    ct, nt = cl // TM, t_all // TM
    in_w = w_in.shape[1]
    rnn_w = conv_w.shape[1]
    kv_w = (in_w - 2 * rnn_w - d_model // 2) // 2
    q_w = in_w - 2 * kv_w - 2 * rnn_w
    n_kv = kv_w // HEAD
    assert ct % 2 == 1 and nt % 2 == 1, "the attention kernel walks key tiles in pairs plus one"
    groups = q_w // kv_w
    mod_spec, row_spec = _tile_specs(b_all, ct, d_model)
    rope_spec = pl.BlockSpec((TM, LANES), lambda b, t: (t, 0))
    cos, sa, sb = ropes

    q, k, vt, xr, gl = pl.pallas_call(
        functools.partial(_hy_in_kernel, q_w=q_w, kv_w=kv_w, rnn_w=rnn_w),
        grid=(b_all, nt),
        in_specs=[row_spec(d_model), mod_spec, _const_spec((1, d_model)), _const_spec((d_model, in_w)),
                  rope_spec, rope_spec, rope_spec, _const_spec((1, q_w)), _const_spec((1, kv_w)),
                  _const_spec((MXU, MXU)), _const_spec((kv_w, kv_w))],
        out_specs=[row_spec(q_w), row_spec(kv_w),
                   pl.BlockSpec((1, n_kv, HEAD + V_ONES, TM), lambda b, t: (b, 0, 0, t)),
                   row_spec(rnn_w), row_spec(rnn_w)],
        out_shape=[jax.ShapeDtypeStruct((b_all, t_all, q_w), BF16),
                   jax.ShapeDtypeStruct((b_all, t_all, kv_w), BF16),
                   jax.ShapeDtypeStruct((b_all, n_kv, HEAD + V_ONES, t_all), BF16),
                   jax.ShapeDtypeStruct((b_all, t_all, rnn_w), F32),
                   jax.ShapeDtypeStruct((b_all, t_all, rnn_w), F32)],
        compiler_params=_params(("parallel", "parallel")),
    )(xs, mods, g1.reshape(1, d_model), w_in.astype(BF16), cos, sa, sb,
      jnp.tile(qn, q_w // HEAD).reshape(1, q_w), jnp.tile(kn, kv_w // HEAD).reshape(1, kv_w),
      _blockdiag_ones(MXU, 1.0 / HEAD), _blockdiag_ones(kv_w, 1.0 / HEAD))

    att = pl.pallas_call(
        functools.partial(_attn_kernel, ct=ct, nt=nt, groups=groups, n_kv=n_kv),
        grid=(b_all, nt),
        in_specs=[row_spec(q_w),
                  pl.BlockSpec((1, t_all, kv_w), lambda b, t: (b, 0, 0)),
                  pl.BlockSpec((1, n_kv, HEAD + V_ONES, t_all), lambda b, t: (b, 0, 0, 0))],
        out_specs=row_spec(q_w),
        out_shape=jax.ShapeDtypeStruct((b_all, t_all, q_w), BF16),
        scratch_shapes=[pltpu.VMEM((q_w, TM), F32)] + [pltpu.VMEM((q_w // HEAD, TM, TM), F32)] * 2,
        compiler_params=_params(("parallel", "parallel")),
    )(q, k, vt)

    n_lc = rnn_w // RG_LANES
    per = RG_LANES // HEAD
    gw = gate_w.reshape(4, n_lc, per, HEAD, HEAD)
    eye = jnp.eye(per, dtype=F32)
    gw = jnp.einsum('gcpde,pq->cgpdqe', gw, eye).reshape(n_lc, 4, RG_LANES, RG_LANES).astype(BF16)
    lane_spec = lambda rows: pl.BlockSpec((rows, RG_LANES), lambda b, c: (0, c))
    seq_spec = pl.BlockSpec((1, t_all, RG_LANES), lambda b, c: (b, 0, c))
    rec = pl.pallas_call(
        functools.partial(_rglru_kernel, cl=cl, n=n),
        grid=(b_all, n_lc),
        in_specs=[seq_spec, seq_spec, lane_spec(CONV_K), lane_spec(1),
                  pl.BlockSpec((1, 4, RG_LANES, RG_LANES), lambda b, c: (c, 0, 0, 0)),
                  lane_spec(4), lane_spec(2)],
        out_specs=seq_spec,
        out_shape=jax.ShapeDtypeStruct((b_all, t_all, rnn_w), BF16),
        scratch_shapes=[pltpu.VMEM((t_all + 2 * SUBLANES, RG_LANES), F32),
                        pltpu.VMEM((2, t_all, RG_LANES), F32),
                        pltpu.VMEM((2, t_all, RG_LANES), F32)],
        compiler_params=_params(("parallel", "parallel")),
    )(xr, gl, conv_w, conv_b.reshape(1, rnn_w), gw, gate_b.reshape(4, rnn_w), lam)
    return (att, rec), w_out.astype(BF16)


def _rwkv_layer(xs, mods, g1, mu, w_rkv, lora_down, lora_up, lora_bias, gate_down, gate_up,
                k_k, k_a, r_k, gn_b, dims):
    b_all, cl, n, d_model = dims
    t_all = cl + n
    ct, nt = cl // TM, t_all // TM
    n8 = t_all // SUBLANES
    per8 = TM // SUBLANES
    lora = lora_down.shape[-1]
    glora = gate_down.shape[-1]
    mod_spec, row_spec = _tile_specs(b_all, ct, d_model)
    dir_spec = pl.BlockSpec((2, 1, TM, d_model), lambda b, t: (0, b, t, 0))
    wdw = jnp.concatenate([lora_down[0, 0], lora_down[1, 0]], axis=1).astype(BF16)
    wda = jnp.concatenate([lora_down[0, 1], lora_down[1, 1]], axis=1).astype(BF16)
    zeros = jnp.zeros((lora, d_model), F32)
    pad_up = lambda j: jnp.stack([jnp.concatenate([lora_up[0, j], zeros], axis=0),
                                  jnp.concatenate([zeros, lora_up[1, j]], axis=0)]).astype(BF16)
    vec = lambda a: a.reshape(1, d_model)
    outs = pl.pallas_call(
        functools.partial(_rw_in_kernel, ct=ct, nt=nt),
        grid=(b_all, nt),
        in_specs=[row_spec(d_model),
                  pl.BlockSpec((1, SUBLANES, d_model), lambda b, t: (b, jnp.maximum(t * per8 - 1, 0), 0)),
                  pl.BlockSpec((1, SUBLANES, d_model), lambda b, t: (b, jnp.minimum((t + 1) * per8, n8 - 1), 0)),
                  mod_spec, _const_spec((1, d_model)), _const_spec((6, d_model)),
                  _const_spec((3, d_model, d_model)), _const_spec((d_model, 2 * lora)),
                  _const_spec((d_model, 2 * lora)), _const_spec((2, 2 * lora, d_model)),
                  _const_spec((2, 2 * lora, d_model)), _const_spec((4, d_model)),
                  _const_spec((d_model, glora)), _const_spec((glora, d_model)),
                  _const_spec((1, d_model)), _const_spec((1, d_model)), _const_spec((1, d_model)),
                  _const_spec((1, d_model)), _const_spec((MXU, MXU))],
        out_specs=[row_spec(d_model), row_spec(d_model), row_spec(d_model),
                   dir_spec, dir_spec, dir_spec, row_spec(d_model), row_spec(d_model)],
        out_shape=[jax.ShapeDtypeStruct((b_all, t_all, d_model), F32)] * 3
        + [jax.ShapeDtypeStruct((2, b_all, t_all, d_model), F32)] * 3
        + [jax.ShapeDtypeStruct((b_all, t_all, d_model), F32)] * 2,
        compiler_params=_params(("parallel", "parallel")),
    )(xs, xs, xs, mods, vec(g1), mu, w_rkv.astype(BF16), wdw, wda, pad_up(0), pad_up(1),
      lora_bias.reshape(4, d_model), gate_down.astype(BF16), gate_up.astype(BF16),
      vec(k_k), vec(k_a), vec(r_k), vec(gn_b), _blockdiag_ones(MXU, 1.0))
    r, kk, v, lw, bb, kd, gate, z = outs

    def tmap(dd, i):
        rev = jnp.where(i < ct, ct - 1 - i, nt - 1 - (i - ct))
        return jnp.where(dd == 0, i, rev)

    blk_lanes = WKV_LANES * WKV_GROUPS
    shared = pl.BlockSpec((1, TM, blk_lanes), lambda dd, b, hh, i: (b, tmap(dd, i), hh))
    per_dir = pl.BlockSpec((1, 1, TM, blk_lanes), lambda dd, b, hh, i: (dd, b, tmap(dd, i), hh))
    y = pl.pallas_call(
        _wkv_kernel,
        grid=(2, b_all, d_model // blk_lanes, nt),
        in_specs=[shared, shared, shared, per_dir, per_dir, per_dir],
        out_specs=per_dir,
        out_shape=jax.ShapeDtypeStruct((2, b_all, t_all, d_model), F32),
        scratch_shapes=[pltpu.VMEM((WKV_GROUPS, WKV_LANES, WKV_LANES), F32)],
        compiler_params=_params(("arbitrary", "arbitrary", "arbitrary", "arbitrary")),
    )(r, kk, v, lw, bb, kd)
    return y, gate, z


def _post(xs, mix_inputs, w_o, mods, g2, w1, w2, dims, odd, gn_g=None, latent_only=False):
    b_all, cl, n, d_model = dims
    t_all = cl + n
    ct, nt = cl // TM, t_all // TM
    t0 = ct if latent_only else 0
    d_ff = w1.shape[1]
    mod_spec = pl.BlockSpec((1, 6, d_model), lambda b, t: (jnp.where(t + t0 < ct, b_all, b), 0, 0))
    row_spec = lambda w: pl.BlockSpec((1, TM, w), lambda b, t: (b, t + t0, 0))
    tail_specs = [_const_spec((d_model, d_model)), mod_spec, _const_spec((1, d_model)),
                  _const_spec((d_model, d_ff)), _const_spec((d_ff, d_model))]
    tail_args = (w_o, mods, g2.reshape(1, d_model), w1.astype(BF16), w2.astype(BF16))
    if odd:
        y, gate, z = mix_inputs
        y_spec = lambda dd: pl.BlockSpec((1, 1, TM, d_model), lambda b, t: (dd, b, t + t0, 0))
        in_specs = [row_spec(d_model), y_spec(0), y_spec(1), row_spec(d_model), row_spec(d_model),
                    _const_spec((1, d_model)), _const_spec((MXU, MXU))] + tail_specs
        args = (xs, y, y, gate, z, gn_g.reshape(1, d_model), _blockdiag_ones(MXU, 1.0 / HEAD)) + tail_args
    else:
        att, rec = mix_inputs
        in_specs = [row_spec(d_model), row_spec(att.shape[-1]), row_spec(rec.shape[-1])] + tail_specs
        args = (xs, att, rec) + tail_args
    rows_out = n if latent_only else t_all
    return pl.pallas_call(
        functools.partial(_post_kernel, odd=odd, ff_chunk=min(d_ff, 4 * MXU)),
        grid=(b_all, nt - t0),
        in_specs=in_specs,
        out_specs=pl.BlockSpec((1, TM, d_model), lambda b, t: (b, t, 0)),
        out_shape=jax.ShapeDtypeStruct((b_all, rows_out, d_model), F32),
        compiler_params=_params(("parallel", "parallel")),
    )(*args)


def kernel(x, c, ctx, c_ctx, ada_w, ada_b, norm_g, mlp_w1, mlp_w2, hy_w_in, hy_w_out, hy_q_norm, hy_k_norm, hy_conv_w, hy_conv_b, hy_gate_w, hy_gate_b, hy_lam, rw_mu, rw_w_rkv, rw_w_o, rw_lora_down, rw_lora_up, rw_lora_bias, rw_gate_down, rw_gate_up, rw_k_k, rw_k_a, rw_r_k, rw_gn_g, rw_gn_b):
    b_all, n, d_model = x.shape
    cl = ctx.shape[1]
    depth = ada_w.shape[0]
    assert cl % TM == 0 and n % TM == 0 and d_model % MXU == 0
    dims = (b_all, cl, n, d_model)
    mods_all = _ada_mods(jnp.concatenate([c, c_ctx[None, :]], axis=0), ada_w, ada_b)
    ropes = _rope_tables(cl, n, GRID_W)
    xs = jnp.concatenate([ctx, x], axis=1)
    for l in range(depth):
        i = l // 2
        mods = mods_all[l]
        last = l == depth - 1
        if l % 2 == 0:
            mix, w_o = _hybrid_layer(xs, mods, norm_g[l, 0], hy_w_in[i], hy_w_out[i], hy_q_norm[i], hy_k_norm[i],
                                     hy_conv_w[i], hy_conv_b[i], hy_gate_w[i], hy_gate_b[i], hy_lam[i], ropes, dims)
            xs = _post(xs, mix, w_o, mods, norm_g[l, 1], mlp_w1[l], mlp_w2[l], dims, odd=False, latent_only=last)
        else:
            mix = _rwkv_layer(xs, mods, norm_g[l, 0], rw_mu[i], rw_w_rkv[i], rw_lora_down[i], rw_lora_up[i],
                              rw_lora_bias[i], rw_gate_down[i], rw_gate_up[i], rw_k_k[i], rw_k_a[i],
                              rw_r_k[i], rw_gn_b[i], dims)
            xs = _post(xs, mix, rw_w_o[i].astype(BF16), mods, norm_g[l, 1], mlp_w1[l], mlp_w2[l], dims,
                       odd=True, gn_g=rw_gn_g[i], latent_only=last)
    return xs if xs.shape[1] == n else xs[:, cl:]
```

```python
import functools
import math

import jax
import jax.numpy as jnp
from jax import lax
from jax.experimental import pallas as pl
from jax.experimental.pallas import tpu as pltpu

F32 = jnp.float32
BF16 = jnp.bfloat16

HEAD = 64
LANES = 128
SUBLANES = 8
MXU = 256
TM = 256
VMEM_LIMIT = 56 * 1024 * 1024

EPS = 1e-6
GN_EPS = 64e-5
RG_C = 8.0
ROPE_THETA = 10000.0
GRID_W = 64
DECAY_SCALE = math.exp(-0.5)
CONV_K = 4
CONV_LEFT = 2
Q_SCALE = HEAD ** -0.5 * math.log2(math.e)
V_ONES = 16
RG_LANES = 256
WKV_CHUNK = 64
WKV_LANES = 256
WKV_GROUPS = 4

NT_DIMS = (((1,), (1,)), ((), ()))
TN_DIMS = (((0,), (0,)), ((), ()))


def _params(sem):
    return pltpu.CompilerParams(dimension_semantics=sem, vmem_limit_bytes=VMEM_LIMIT)


def _const_spec(shape):
    nd = len(shape)
    return pl.BlockSpec(shape, lambda *_: (0,) * nd, pipeline_mode=pl.Buffered(1))


def _dot(a, b):
    return jnp.dot(a, b, preferred_element_type=F32)


def _dot_exact(a, b):
    return jnp.dot(a, b, preferred_element_type=F32, precision=lax.Precision.HIGHEST)


def _split(x):
    hi = x.astype(BF16)
    lo = (x - hi.astype(F32)).astype(BF16)
    return hi, lo


def _group_reduce(x, ones_ref):
    cw = ones_ref.shape[0]
    ones = ones_ref[...]
    outs = []
    for c in range(x.shape[1] // cw):
        hi, lo = _split(x[:, c * cw:(c + 1) * cw])
        outs.append(_dot(hi, ones) + _dot(lo, ones))
    return outs[0] if len(outs) == 1 else jnp.concatenate(outs, axis=1)


def _sigmoid(x):
    return 0.5 * jnp.tanh(0.5 * x) + 0.5


def _norm_mod(x, g, shift, scale):
    ms = jnp.mean(x * x, axis=-1, keepdims=True)
    return x * lax.rsqrt(ms + EPS) * (g * (1.0 + scale)) + shift


def _ada_kernel(c_ref, w_ref, b_ref, o_ref):
    c = c_ref[...]
    s = c * _sigmoid(c)
    o_ref[0, 0] = _dot_exact(s, w_ref[0]) + b_ref[0]


def _ada_mods(cc, ada_w, ada_b):
    depth, d, _ = ada_w.shape
    rows = cc.shape[0]
    out = pl.pallas_call(
        _ada_kernel,
        grid=(depth, 6),
        in_specs=[pl.BlockSpec((rows, d), lambda l, j: (0, 0)),
                  pl.BlockSpec((1, d, d), lambda l, j: (l, 0, j)),
                  pl.BlockSpec((1, 1, d), lambda l, j: (l * 6 + j, 0, 0))],
        out_specs=pl.BlockSpec((1, 1, rows, d), lambda l, j: (l, j, 0, 0)),
        out_shape=jax.ShapeDtypeStruct((depth, 6, rows, d), F32),
        compiler_params=_params(("arbitrary", "arbitrary")),
    )(cc, ada_w, ada_b.reshape(depth * 6, 1, d))
    return jnp.transpose(out, (0, 2, 1, 3))


def _hy_in_kernel(x_ref, mod_ref, g_ref, w_ref, cos_ref, sa_ref, sb_ref, qg_ref, kg_ref, oq_ref, ok_ref,
                  q_out, k_out, vt_out, xr_out, gl_out, *, q_w, kv_w, rnn_w):
    x = x_ref[0]
    h = _norm_mod(x, g_ref[...], mod_ref[0, 0:1, :], mod_ref[0, 1:2, :]).astype(BF16)
    z = _dot(h, w_ref[...])
    c0, c1, c2, c3 = q_w, q_w + kv_w, q_w + 2 * kv_w, q_w + 2 * kv_w + rnn_w

    def norm_rope(u, gain, ones_ref):
        width = u.shape[1]
        un = u * lax.rsqrt(_group_reduce(u * u, ones_ref) + EPS) * gain
        reps = width // LANES
        tile = lambda r: jnp.concatenate([r[...]] * reps, axis=1) if reps > 1 else r[...]
        nxt = pltpu.roll(un, width - 1, 1)
        prv = pltpu.roll(un, 1, 1)
        return un * tile(cos_ref) + nxt * tile(sa_ref) + prv * tile(sb_ref)

    q_out[0] = (norm_rope(z[:, :c0], qg_ref[...], oq_ref) * Q_SCALE).astype(BF16)
    k_out[0] = norm_rope(z[:, c0:c1], kg_ref[...], ok_ref).astype(BF16)
    vt = z[:, c1:c2].T
    ones = jnp.ones((V_ONES, TM), F32)
    for hk in range(kv_w // HEAD):
        vt_out[0, hk] = jnp.concatenate([vt[hk * HEAD:(hk + 1) * HEAD], ones], axis=0).astype(BF16)
    xr_out[0] = z[:, c2:c3]
    gl_out[0] = z[:, c3:]


def _attn_kernel(q_ref, k_ref, vt_ref, o_ref, ot_ref, sa_ref, sb_ref, *, ct, nt, groups, n_kv):
    t = pl.program_id(1)
    nkv = jnp.where(t < ct, ct, nt)
    q = q_ref[0].astype(F32)
    lane = lax.broadcasted_iota(jnp.int32, (TM, LANES), 1)
    low = lane < HEAD
    blocks = []
    for head in range(groups * n_kv):
        hk = head // groups
        qc = q[:, LANES * (head // 2):LANES * (head // 2 + 1)]
        if head % 2 != hk % 2:
            qc = pltpu.roll(qc, HEAD, 1)
        blocks.append((hk, jnp.where(low if hk % 2 == 0 else jnp.logical_not(low), qc, 0.0).astype(BF16)))

    def scores(i, dst_ref, heads):
        r0 = pl.multiple_of(i * TM, TM)
        for head in heads:
            hk, qm = blocks[head]
            kc = k_ref[0, pl.ds(r0, TM), LANES * (hk // 2):LANES * (hk // 2 + 1)]
            dst_ref[head] = lax.dot_general(kc, qm, NT_DIMS, preferred_element_type=F32)

    def absorb(i, src_ref, stats, nxt=None):
        r0 = pl.multiple_of(i * TM, TM)
        out = []
        for head, ((hk, _), (m, acc)) in enumerate(zip(blocks, stats)):
            if nxt is not None:
                scores(nxt[0], nxt[1], [head])
            st = src_ref[head]
            m_new = jnp.maximum(m, jnp.max(st, axis=0, keepdims=True))
            out.append((m_new, _dot(vt_ref[0, hk, :, pl.ds(r0, TM)], jnp.exp2(st - m_new).astype(BF16))))
        return tuple((m_new, jnp.exp2(m - m_new) * acc + pv) for (m_new, pv), (m, acc) in zip(out, stats))

    def pair(j, stats):
        stats = absorb(2 * j, sa_ref, stats, (2 * j + 1, sb_ref))
        return absorb(2 * j + 1, sb_ref, stats, (2 * j + 2, sa_ref))

    scores(0, sa_ref, range(len(blocks)))
    init = tuple((jnp.full((1, TM), -jnp.inf, F32), jnp.zeros((vt_ref.shape[2], TM), F32)) for _ in blocks)
    res = absorb(nkv - 1, sa_ref, lax.fori_loop(0, (nkv - 1) // 2, pair, init))
    for head, (_, acc) in enumerate(res):
        ot_ref[HEAD * head:HEAD * (head + 1), :] = acc[:HEAD] / acc[HEAD:HEAD + 1]
    o_ref[0] = ot_ref[...].T.astype(BF16)


def _scan8(a, u, reverse):
    row = lax.broadcasted_iota(jnp.int32, a.shape, 0)
    for d in (1, 2, 4):
        if reverse:
            a_s, u_s, ok = pltpu.roll(a, SUBLANES - d, 0), pltpu.roll(u, SUBLANES - d, 0), row < SUBLANES - d
        else:
            a_s, u_s, ok = pltpu.roll(a, d, 0), pltpu.roll(u, d, 0), row >= d
        u = a * jnp.where(ok, u_s, 0.0) + u
        a = a * jnp.where(ok, a_s, 1.0)
    return a, u


def _rglru_kernel(xr_ref, gl_ref, cw_ref, cb_ref, gw_ref, gb_ref, lam_ref, o_ref,
                  xs_ref, a_ref, u_ref, *, cl, n):
    t_all = cl + n
    pad = SUBLANES
    width = xs_ref.shape[1]
    xs_ref[0:pad, :] = jnp.zeros((pad, width), F32)
    xs_ref[pad + t_all:, :] = jnp.zeros((pad, width), F32)
    xs_ref[pad:pad + t_all, :] = xr_ref[0]
    lam = lam_ref[...]
    z = -lam
    softplus = jnp.maximum(z, 0.0) + jnp.log(1.0 + jnp.exp(-jnp.abs(z)))
    cw = cw_ref[...]
    cb = cb_ref[...]
    gb = gb_ref[...]

    def coeffs(i, _):
        r0 = pl.multiple_of(i * TM, TM)
        blk = xs_ref[pl.ds(r0, TM + 2 * pad), :]
        rows = r0 + lax.broadcasted_iota(jnp.int32, (TM, 1), 0)
        in_lat = rows >= cl
        pos = jnp.where(in_lat, rows - cl, rows)
        seqlen = jnp.where(in_lat, n, cl)
        xc = jnp.zeros((TM, width), F32) + cb
        for j in range(CONV_K):
            off = j - CONV_LEFT
            tap = blk[pad + off:pad + off + TM, :]
            ok = jnp.logical_and(pos + off >= 0, pos + off < seqlen)
            xc = xc + jnp.where(ok, tap, 0.0) * cw[j:j + 1, :]
        xcb = xc.astype(BF16)
        for d in range(2):
            r = _sigmoid(_dot(xcb, gw_ref[0, 2 * d]) + gb[2 * d:2 * d + 1, :])
            ig = _sigmoid(_dot(xcb, gw_ref[0, 2 * d + 1]) + gb[2 * d + 1:2 * d + 2, :])
            a = jnp.exp(-RG_C * r * softplus[d:d + 1, :])
            a_ref[d, pl.ds(r0, TM), :] = a
            u_ref[d, pl.ds(r0, TM), :] = jnp.sqrt(1.0 - a * a) * (ig * xc)
        return 0

    lax.fori_loop(0, t_all // TM, coeffs, 0)

    g_all, g_ctx = t_all // SUBLANES, cl // SUBLANES

    def step(i, carry):
        h_fwd, h_rev = carry
        g_rev = jnp.where(i < g_ctx, g_ctx - 1 - i, g_all - 1 - (i - g_ctx))
        sf = pl.ds(pl.multiple_of(i * SUBLANES, SUBLANES), SUBLANES)
        sr = pl.ds(pl.multiple_of(g_rev * SUBLANES, SUBLANES), SUBLANES)
        af, uf = _scan8(a_ref[0, sf, :], u_ref[0, sf, :], False)
        ar, ur = _scan8(a_ref[1, sr, :], u_ref[1, sr, :], True)
        hf = af * h_fwd + uf
        hr = ar * h_rev + ur
        u_ref[0, sf, :] = hf
        u_ref[1, sr, :] = hr
        return hf[SUBLANES - 1:SUBLANES, :], hr[0:1, :]

    zero = jnp.zeros((1, width), F32)
    lax.fori_loop(0, g_all, step, (zero, zero), unroll=4)

    def combine(i, _):
        r0 = pl.multiple_of(i * TM, TM)
        hsum = u_ref[0, pl.ds(r0, TM), :] + u_ref[1, pl.ds(r0, TM), :]
        o_ref[0, pl.ds(r0, TM), :] = (jax.nn.gelu(gl_ref[0, pl.ds(r0, TM), :]) * hsum).astype(BF16)
        return 0

    lax.fori_loop(0, t_all // TM, combine, 0)


def _rw_in_kernel(x_ref, xp_ref, xn_ref, mod_ref, g_ref, mu_ref, wrkv_ref, wdw_ref, wda_ref, wuw_ref, wua_ref,
                  lb_ref, gd_ref, gu_ref, kk_ref, ka_ref, rk_ref, gnb_ref, ones_ref,
                  r_out, kk_out, v_out, lw_out, bb_out, kd_out, g_out, z_out, *, ct, nt):
    t = pl.program_id(1)
    g = g_ref[...]
    shift, scale = mod_ref[0, 0:1, :], mod_ref[0, 1:2, :]
    h = _norm_mod(x_ref[0], g, shift, scale)
    first = jnp.logical_or(t == 0, t == ct)
    last = jnp.logical_or(t == ct - 1, t == nt - 1)
    hp = _norm_mod(xp_ref[0], g, shift, scale)[SUBLANES - 1:SUBLANES, :]
    hn = _norm_mod(xn_ref[0], g, shift, scale)[0:1, :]
    hp = jnp.where(first, 0.0, hp)
    hn = jnp.where(last, 0.0, hn)
    row = lax.broadcasted_iota(jnp.int32, (TM, 1), 0)
    h_prev = jnp.where(row == 0, hp, pltpu.roll(h, 1, 0))
    h_next = jnp.where(row == TM - 1, hn, pltpu.roll(h, TM - 1, 0))
    xx = 0.5 * (h_prev + h_next) - h
    lerp = lambda j: (h + xx * mu_ref[j:j + 1, :]).astype(BF16)

    r = _dot(lerp(0), wrkv_ref[0])
    k = _dot(lerp(2), wrkv_ref[1])
    v = _dot(lerp(3), wrkv_ref[2])
    gate = _dot(_sigmoid(_dot(lerp(5), gd_ref[...])).astype(BF16), gu_ref[...])
    tw = jnp.tanh(_dot(lerp(1), wdw_ref[...])).astype(BF16)
    ta = _dot(lerp(4), wda_ref[...]).astype(BF16)

    kk = k * kk_ref[...]
    nrm = jnp.sqrt(_group_reduce(kk * kk, ones_ref))
    kk = kk / jnp.maximum(nrm, 1e-12)
    r_out[0] = r
    kk_out[0] = kk
    v_out[0] = v
    kd_sum = jnp.zeros_like(k)
    for d in range(2):
        dec = lb_ref[2 * d:2 * d + 1, :] + _dot(tw, wuw_ref[d])
        a = _sigmoid(lb_ref[2 * d + 1:2 * d + 2, :] + _dot(ta, wua_ref[d]))
        kd = k * (1.0 + (a - 1.0) * ka_ref[...])
        lw_out[d, 0] = -DECAY_SCALE * _sigmoid(dec)
        bb_out[d, 0] = kk * a
        kd_out[d, 0] = kd
        kd_sum = kd_sum + kd
    bonus = _group_reduce(r * kd_sum * rk_ref[...], ones_ref)
    g_out[0] = gate
    z_out[0] = (gnb_ref[...] + bonus * v) * gate


def _wkv_kernel(r_ref, kk_ref, v_ref, lw_ref, bb_ref, kd_ref, y_ref, h_ref):
    c, lw_n = WKV_CHUNK, WKV_LANES
    n_heads = lw_n // HEAD
    s_rows = n_heads * c
    d = pl.program_id(0)
    i = pl.program_id(3)

    @pl.when(i == 0)
    def _():
        h_ref[...] = jnp.zeros_like(h_ref)

    rev = d == 1
    row = lax.broadcasted_iota(jnp.int32, (c, c), 0)
    col = lax.broadcasted_iota(jnp.int32, (c, c), 1)
    tri = (jnp.where(rev, col - row, row - col) >= 0).astype(BF16)
    srow = lax.broadcasted_iota(jnp.int32, (s_rows, s_rows), 0)
    scol = lax.broadcasted_iota(jnp.int32, (s_rows, s_rows), 1)
    ahead = jnp.where(rev, scol % c - srow % c, srow % c - scol % c)
    strict = ahead > 0
    incl = ahead >= 0
    eye = (srow == scol).astype(F32)
    same_block = lambda size: (srow // size) == (scol // size)
    lrow =lax.broadcasted_iota(jnp.int32, (lw_n, lw_n), 0)
    lcol = lax.broadcasted_iota(jnp.int32, (lw_n, lw_n), 1)
    same_head = (lrow // HEAD) == (lcol // HEAD)
    diag = lrow == lcol
    head_of_lane = lax.broadcasted_iota(jnp.int32, (s_rows, lw_n), 1) // HEAD
    head_of_row = lax.broadcasted_iota(jnp.int32, (s_rows, lw_n), 0) // c
    own = head_of_lane == head_of_row
    n_chunks = TM // c

    def stack(x):
        return jnp.where(own, jnp.concatenate([x] * n_heads, axis=0), 0.0)

    def unstack(xs):
        out = xs[0:c]
        for hh in range(1, n_heads):
            out = out + xs[hh * c:(hh + 1) * c]
        return out

    def local(grp, ci):
        cidx = jnp.where(rev, n_chunks - 1 - ci, ci)
        sl = pl.ds(pl.multiple_of(cidx * c, c), c)
        ls = slice(grp * lw_n, (grp + 1) * lw_n)
        r, kk, v = r_ref[0, sl, ls], kk_ref[0, sl, ls], v_ref[0, sl, ls]
        lw, bb, kd = lw_ref[0, 0, sl, ls], bb_ref[0, 0, sl, ls], kd_ref[0, 0, sl, ls]
        lw_hi, lw_rest = _split(lw)[0], lw - _split(lw)[0].astype(F32)
        lw_mid, lw_lo = _split(lw_rest)
        cum = _dot(tri, lw_hi) + (_dot(tri, lw_mid) + _dot(tri, lw_lo))
        yield
        tot = jnp.sum(lw, axis=0, keepdims=True)
        e_neg = jnp.exp(-cum)
        e_end = jnp.exp(tot - cum)
        kd_g, b_g = kd * e_end, bb * e_end
        kk_s = stack(kk * jnp.exp(cum - lw))
        r_s = stack(r * jnp.exp(cum))
        v_s = stack(v).astype(BF16)
        lhs = jnp.concatenate([kk_s, r_s], axis=0)
        yield
        p1 = _bdot(lhs, stack(kd * e_neg), NT_DIMS)
        yield
        p2 = _bdot(lhs, stack(bb * e_neg), NT_DIMS)
        yield
        a_kd = jnp.where(strict, p1[:s_rows], 0.0)
        b_kd = jnp.where(incl, p1[s_rows:], 0.0)
        nmat = jnp.where(strict, p2[:s_rows], 0.0)
        b_b = jnp.where(incl, p2[s_rows:], 0.0).astype(BF16)
        tinv = eye - jnp.where(same_block(2), nmat, 0.0)
        size = 2
        while size < c:
            n_off = jnp.where(jnp.logical_and(same_block(2 * size), jnp.logical_not(same_block(size))), nmat, 0.0)
            tb = tinv.astype(BF16)
            nt = _bdot(n_off, tb)
            yield
            tinv = tinv - _bdot(tb, nt)
            yield
            size *= 2
        av = _bdot(a_kd, v_s)
        yield
        tw = _bdot(tinv, jnp.concatenate([kk_s, av], axis=1))
        yield
        bw = _bdot(b_b, tw)
        yield
        kkp, u0 = unstack(tw[:, :lw_n]), unstack(tw[:, lw_n:])
        rp = unstack(r_s - bw[:, :lw_n])
        y0 = unstack(_bdot(b_kd, v_s) - bw[:, lw_n:])
        m_mat = jnp.where(diag, jnp.exp(tot), 0.0) - jnp.where(same_head, _bdot(b_g, kkp, TN_DIMS), 0.0)
        g_mat = jnp.where(same_head, _bdot(jnp.concatenate([kd_g, b_g], axis=0),
                                           jnp.concatenate([v, -u0], axis=0), TN_DIMS), 0.0)
        return grp, sl, ls, rp, y0, m_mat, g_mat

    chains = [local(grp, ci) for ci in range(n_chunks) for grp in range(h_ref.shape[0])]
    parts = [None] * len(chains)
    while any(p is None for p in parts):
        for idx, chain in enumerate(chains):
            if parts[idx] is None:
                try:
                    next(chain)
                except StopIteration as done:
                    parts[idx] = done.value
    hs = [h_ref[grp] for grp in range(h_ref.shape[0])]
    for grp, sl, ls, rp, y0, m_mat, g_mat in parts:
        y_ref[0, 0, sl, ls] = _bdot(rp, hs[grp]) + y0
        hs[grp] = _bdot(m_mat, hs[grp]) + g_mat
    for grp, h_new in enumerate(hs):
        h_ref[grp] = h_new


def _bdot(a, b, dims=(((1,), (0,)), ((), ()))):
    return lax.dot_general(a.astype(BF16), b.astype(BF16), dims, preferred_element_type=F32)


def _post_kernel(*refs, odd, ff_chunk):
    if odd:
        (x_ref, y0_ref, y1_ref, gate_ref, z_ref, gng_ref, ones_ref,
         wo_ref, mod_ref, g2_ref, w1_ref, w2_ref, o_ref) = refs
        y = y0_ref[0, 0] + y1_ref[0, 0]
        dlt = y - _group_reduce(y, ones_ref)
        var = _group_reduce(dlt * dlt, ones_ref)
        mix = (dlt * lax.rsqrt(var + GN_EPS) * gng_ref[...] * gate_ref[0] + z_ref[0]).astype(BF16)
    else:
        x_ref, att_ref, rec_ref, wo_ref, mod_ref, g2_ref, w1_ref, w2_ref, o_ref = refs
        mix = jnp.concatenate([att_ref[0], rec_ref[0]], axis=1)
    x1 = x_ref[0] + mod_ref[0, 2:3, :] * _dot(mix, wo_ref[...])
    h2 = _norm_mod(x1, g2_ref[...], mod_ref[0, 3:4, :], mod_ref[0, 4:5, :]).astype(BF16)
    acc = jnp.zeros_like(x1)
    for c in range(w1_ref.shape[1] // ff_chunk):
        a = _dot(h2, w1_ref[:, c * ff_chunk:(c + 1) * ff_chunk])
        a = jnp.square(jnp.maximum(a, 0.0)).astype(BF16)
        acc = acc + _dot(a, w2_ref[c * ff_chunk:(c + 1) * ff_chunk, :])
    o_ref[0] = x1 + mod_ref[0, 5:6, :] * acc


def _blockdiag_ones(width, value):
    idx = jnp.arange(width) // HEAD
    return jnp.where(idx[:, None] == idx[None, :], value, 0.0).astype(BF16)


def _rope_tables(cl, n, grid_w):
    rows = n // grid_w
    row = jnp.repeat(jnp.arange(rows, dtype=F32), grid_w)
    col = jnp.tile(jnp.arange(grid_w, dtype=F32), rows)
    half = HEAD // 2
    inv = ROPE_THETA ** (-jnp.arange(0, half, 2, dtype=F32) / half)
    ang = jnp.concatenate([row[:, None] * inv, col[:, None] * inv], axis=-1)
    cos = jnp.repeat(jnp.cos(ang), 2, axis=-1)
    sin = jnp.repeat(jnp.sin(ang), 2, axis=-1)
    even = (jnp.arange(HEAD) % 2 == 0)[None, :]
    sa = jnp.where(even, -sin, 0.0)
    sb = jnp.where(even, 0.0, sin)
    ctx = lambda fill: jnp.full((cl, HEAD), fill, F32)
    full = lambda lat, fill: jnp.tile(jnp.concatenate([ctx(fill), lat], axis=0), (1, LANES // HEAD))
    return full(cos, 1.0), full(sa, 0.0), full(sb, 0.0)


def _tile_specs(b_all, ct, d_model):
    mod_spec = pl.BlockSpec((1, 6, d_model), lambda b, t: (jnp.where(t < ct, b_all, b), 0, 0))
    row_spec = lambda w: pl.BlockSpec((1, TM, w), lambda b, t: (b, t, 0))
    return mod_spec, row_spec


def _hybrid_layer(xs, mods, g1, w_in, w_out, qn, kn, conv_w, conv_b, gate_w, gate_b, lam, ropes, dims):
    b_all, cl, n, d_model = dims
    t_all = cl + n
    ct, nt = cl // TM, t_all // TM
    in_w = w_in.shape[1]
    rnn_w = conv_w.shape[1]
    kv_w = (in_w - 2 * rnn_w - d_model // 2) // 2
    q_w = in_w - 2 * kv_w - 2 * rnn_w
    n_kv = kv_w // HEAD
    assert ct % 2 == 1 and nt % 2 == 1, "the attention kernel walks key tiles in pairs plus one"
    groups = q_w // kv_w
    mod_spec, row_spec = _tile_specs(b_all, ct, d_model)
    rope_spec = pl.BlockSpec((TM, LANES), lambda b, t: (t, 0))
    cos, sa, sb = ropes

    q, k, vt, xr, gl = pl.pallas_call(
        functools.partial(_hy_in_kernel, q_w=q_w, kv_w=kv_w, rnn_w=rnn_w),
        grid=(b_all, nt),
        in_specs=[row_spec(d_model), mod_spec, _const_spec((1, d_model)), _const_spec((d_model, in_w)),
                  rope_spec, rope_spec, rope_spec, _const_spec((1, q_w)), _const_spec((1, kv_w)),
                  _const_spec((MXU, MXU)), _const_spec((kv_w, kv_w))],
        out_specs=[row_spec(q_w), row_spec(kv_w),
                   pl.BlockSpec((1, n_kv, HEAD + V_ONES, TM), lambda b, t: (b, 0, 0, t)),
                   row_spec(rnn_w), row_spec(rnn_w)],
        out_shape=[jax.ShapeDtypeStruct((b_all, t_all, q_w), BF16),
                   jax.ShapeDtypeStruct((b_all, t_all, kv_w), BF16),
                   jax.ShapeDtypeStruct((b_all, n_kv, HEAD + V_ONES, t_all), BF16),
                   jax.ShapeDtypeStruct((b_all, t_all, rnn_w), F32),
                   jax.ShapeDtypeStruct((b_all, t_all, rnn_w), F32)],
        compiler_params=_params(("parallel", "parallel")),
    )(xs, mods, g1.reshape(1, d_model), w_in.astype(BF16), cos, sa, sb,
      jnp.tile(qn, q_w // HEAD).reshape(1, q_w), jnp.tile(kn, kv_w // HEAD).reshape(1, kv_w),
      _blockdiag_ones(MXU, 1.0 / HEAD), _blockdiag_ones(kv_w, 1.0 / HEAD))

    att = pl.pallas_call(
        functools.partial(_attn_kernel, ct=ct, nt=nt, groups=groups, n_kv=n_kv),
        grid=(b_all, nt),
        in_specs=[row_spec(q_w),
                  pl.BlockSpec((1, t_all, kv_w), lambda b, t: (b, 0, 0)),
                  pl.BlockSpec((1, n_kv, HEAD + V_ONES, t_all), lambda b, t: (b, 0, 0, 0))],
        out_specs=row_spec(q_w),
        out_shape=jax.ShapeDtypeStruct((b_all, t_all, q_w), BF16),
        scratch_shapes=[pltpu.VMEM((q_w, TM), F32)] + [pltpu.VMEM((q_w // HEAD, TM, TM), F32)] * 2,
        compiler_params=_params(("parallel", "parallel")),
    )(q, k, vt)

    n_lc = rnn_w // RG_LANES
    per = RG_LANES // HEAD
    gw = gate_w.reshape(4, n_lc, per, HEAD, HEAD)
    eye = jnp.eye(per, dtype=F32)
    gw = jnp.einsum('gcpde,pq->cgpdqe', gw, eye).reshape(n_lc, 4, RG_LANES, RG_LANES).astype(BF16)
    lane_spec = lambda rows: pl.BlockSpec((rows, RG_LANES), lambda b, c: (0, c))
    seq_spec = pl.BlockSpec((1, t_all, RG_LANES), lambda b, c: (b, 0, c))
    rec = pl.pallas_call(
        functools.partial(_rglru_kernel, cl=cl, n=n),
        grid=(b_all, n_lc),
        in_specs=[seq_spec, seq_spec, lane_spec(CONV_K), lane_spec(1),
                  pl.BlockSpec((1, 4, RG_LANES, RG_LANES), lambda b, c: (c, 0, 0, 0)),
                  lane_spec(4), lane_spec(2)],
        out_specs=seq_spec,
        out_shape=jax.ShapeDtypeStruct((b_all, t_all, rnn_w), BF16),
        scratch_shapes=[pltpu.VMEM((t_all + 2 * SUBLANES, RG_LANES), F32),
                        pltpu.VMEM((2, t_all, RG_LANES), F32),
                        pltpu.VMEM((2, t_all, RG_LANES), F32)],
        compiler_params=_params(("parallel", "parallel")),
    )(xr, gl, conv_w, conv_b.reshape(1, rnn_w), gw, gate_b.reshape(4, rnn_w), lam)
    return (att, rec), w_out.astype(BF16)


def _rwkv_layer(xs, mods, g1, mu, w_rkv, lora_down, lora_up, lora_bias, gate_down, gate_up,
                k_k, k_a, r_k, gn_b, dims):
    b_all, cl, n, d_model = dims
    t_all = cl + n
    ct, nt = cl // TM, t_all // TM
    n8 = t_all // SUBLANES
    per8 = TM // SUBLANES
    lora = lora_down.shape[-1]
    glora = gate_down.shape[-1]
    mod_spec, row_spec = _tile_specs(b_all, ct, d_model)
    dir_spec = pl.BlockSpec((2, 1, TM, d_model), lambda b, t: (0, b, t, 0))
    wdw = jnp.concatenate([lora_down[0, 0], lora_down[1, 0]], axis=1).astype(BF16)
    wda = jnp.concatenate([lora_down[0, 1], lora_down[1, 1]], axis=1).astype(BF16)
    zeros = jnp.zeros((lora, d_model), F32)
    pad_up = lambda j: jnp.stack([jnp.concatenate([lora_up[0, j], zeros], axis=0),
                                  jnp.concatenate([zeros, lora_up[1, j]], axis=0)]).astype(BF16)
    vec = lambda a: a.reshape(1, d_model)
    outs = pl.pallas_call(
        functools.partial(_rw_in_kernel, ct=ct, nt=nt),
        grid=(b_all, nt),
        in_specs=[row_spec(d_model),
                  pl.BlockSpec((1, SUBLANES, d_model), lambda b, t: (b, jnp.maximum(t * per8 - 1, 0), 0)),
                  pl.BlockSpec((1, SUBLANES, d_model), lambda b, t: (b, jnp.minimum((t + 1) * per8, n8 - 1), 0)),
                  mod_spec, _const_spec((1, d_model)), _const_spec((6, d_model)),
                  _const_spec((3, d_model, d_model)), _const_spec((d_model, 2 * lora)),
                  _const_spec((d_model, 2 * lora)), _const_spec((2, 2 * lora, d_model)),
                  _const_spec((2, 2 * lora, d_model)), _const_spec((4, d_model)),
                  _const_spec((d_model, glora)), _const_spec((glora, d_model)),
                  _const_spec((1, d_model)), _const_spec((1, d_model)), _const_spec((1, d_model)),
                  _const_spec((1, d_model)), _const_spec((MXU, MXU))],
        out_specs=[row_spec(d_model), row_spec(d_model), row_spec(d_model),
                   dir_spec, dir_spec, dir_spec, row_spec(d_model), row_spec(d_model)],
        out_shape=[jax.ShapeDtypeStruct((b_all, t_all, d_model), F32)] * 3
        + [jax.ShapeDtypeStruct((2, b_all, t_all, d_model), F32)] * 3
        + [jax.ShapeDtypeStruct((b_all, t_all, d_model), F32)] * 2,
        compiler_params=_params(("parallel", "parallel")),
    )(xs, xs, xs, mods, vec(g1), mu, w_rkv.astype(BF16), wdw, wda, pad_up(0), pad_up(1),
      lora_bias.reshape(4, d_model), gate_down.astype(BF16), gate_up.astype(BF16),
      vec(k_k), vec(k_a), vec(r_k), vec(gn_b), _blockdiag_ones(MXU, 1.0))
    r, kk, v, lw, bb, kd, gate, z = outs

    def tmap(dd, i):
        rev = jnp.where(i < ct, ct - 1 - i, nt - 1 - (i - ct))
        return jnp.where(dd == 0, i, rev)

    blk_lanes = WKV_LANES * WKV_GROUPS
    shared = pl.BlockSpec((1, TM, blk_lanes), lambda dd, b, hh, i: (b, tmap(dd, i), hh))
    per_dir = pl.BlockSpec((1, 1, TM, blk_lanes), lambda dd, b, hh, i: (dd, b, tmap(dd, i), hh))
    y = pl.pallas_call(
        _wkv_kernel,
        grid=(2, b_all, d_model // blk_lanes, nt),
        in_specs=[shared, shared, shared, per_dir, per_dir, per_dir],
        out_specs=per_dir,
        out_shape=jax.ShapeDtypeStruct((2, b_all, t_all, d_model), F32),
        scratch_shapes=[pltpu.VMEM((WKV_GROUPS, WKV_LANES, WKV_LANES), F32)],
        compiler_params=_params(("arbitrary", "arbitrary", "arbitrary", "arbitrary")),
    )(r, kk, v, lw, bb, kd)
    return y, gate, z


def _post(xs, mix_inputs, w_o, mods, g2, w1, w2, dims, odd, gn_g=None, latent_only=False):
    b_all, cl, n, d_model = dims
    t_all = cl + n
    ct, nt = cl // TM, t_all // TM
    t0 = ct if latent_only else 0
    d_ff = w1.shape[1]
    mod_spec = pl.BlockSpec((1, 6, d_model), lambda b, t: (jnp.where(t + t0 < ct, b_all, b), 0, 0))
    row_spec = lambda w: pl.BlockSpec((1, TM, w), lambda b, t: (b, t + t0, 0))
    tail_specs = [_const_spec((d_model, d_model)), mod_spec, _const_spec((1, d_model)),
                  _const_spec((d_model, d_ff)), _const_spec((d_ff, d_model))]
    tail_args = (w_o, mods, g2.reshape(1, d_model), w1.astype(BF16), w2.astype(BF16))
    if odd:
        y, gate, z = mix_inputs
        y_spec = lambda dd: pl.BlockSpec((1, 1, TM, d_model), lambda b, t: (dd, b, t + t0, 0))
        in_specs = [row_spec(d_model), y_spec(0), y_spec(1), row_spec(d_model), row_spec(d_model),
                    _const_spec((1, d_model)), _const_spec((MXU, MXU))] + tail_specs
        args = (xs, y, y, gate, z, gn_g.reshape(1, d_model), _blockdiag_ones(MXU, 1.0 / HEAD)) + tail_args
    else:
        att, rec = mix_inputs
        in_specs = [row_spec(d_model), row_spec(att.shape[-1]), row_spec(rec.shape[-1])] + tail_specs
        args = (xs, att, rec) + tail_args
    rows_out = n if latent_only else t_all
    return pl.pallas_call(
        functools.partial(_post_kernel, odd=odd, ff_chunk=min(d_ff, 4 * MXU)),
        grid=(b_all, nt - t0),
        in_specs=in_specs,
        out_specs=pl.BlockSpec((1, TM, d_model), lambda b, t: (b, t, 0)),
        out_shape=jax.ShapeDtypeStruct((b_all, rows_out, d_model), F32),
        compiler_params=_params(("parallel", "parallel")),
    )(*args)


def kernel(x, c, ctx, c_ctx, ada_w, ada_b, norm_g, mlp_w1, mlp_w2, hy_w_in, hy_w_out, hy_q_norm, hy_k_norm, hy_conv_w, hy_conv_b, hy_gate_w, hy_gate_b, hy_lam, rw_mu, rw_w_rkv, rw_w_o, rw_lora_down, rw_lora_up, rw_lora_bias, rw_gate_down, rw_gate_up, rw_k_k, rw_k_a, rw_r_k, rw_gn_g, rw_gn_b):
    b_all, n, d_model = x.shape
    cl = ctx.shape[1]
    depth = ada_w.shape[0]
    assert cl % TM == 0 and n % TM == 0 and d_model % MXU == 0
    dims = (b_all, cl, n, d_model)
    mods_all = _ada_mods(jnp.concatenate([c, c_ctx[None, :]], axis=0), ada_w, ada_b)
    ropes = _rope_tables(cl, n, GRID_W)
    xs = jnp.concatenate([ctx, x], axis=1)
    for l in range(depth):
        i = l // 2
        mods = mods_all[l]
        last = l == depth - 1
        if l % 2 == 0:
            mix, w_o = _hybrid_layer(xs, mods, norm_g[l, 0], hy_w_in[i], hy_w_out[i], hy_q_norm[i], hy_k_norm[i],
                                     hy_conv_w[i], hy_conv_b[i], hy_gate_w[i], hy_gate_b[i], hy_lam[i], ropes, dims)
            xs = _post(xs, mix, w_o, mods, norm_g[l, 1], mlp_w1[l], mlp_w2[l], dims, odd=False, latent_only=last)
        else:
            mix = _rwkv_layer(xs, mods, norm_g[l, 0], rw_mu[i], rw_w_rkv[i], rw_lora_down[i], rw_lora_up[i],
                              rw_lora_bias[i], rw_gate_down[i], rw_gate_up[i], rw_k_k[i], rw_k_a[i],
                              rw_r_k[i], rw_gn_b[i], dims)
            xs = _post(xs, mix, rw_w_o[i].astype(BF16), mods, norm_g[l, 1], mlp_w1[l], mlp_w2[l], dims,
                       odd=True, gn_g=rw_gn_g[i], latent_only=last)
    return xs if xs.shape[1] == n else xs[:, cl:]
```

```python
import functools
import math

import jax
import jax.numpy as jnp
from jax import lax
from jax.experimental import pallas as pl
from jax.experimental.pallas import tpu as pltpu

F32 = jnp.float32
BF16 = jnp.bfloat16

HEAD = 64
LANES = 128
SUBLANES = 8
MXU = 256
TM = 256
VMEM_LIMIT = 56 * 1024 * 1024

EPS = 1e-6
GN_EPS = 64e-5
RG_C = 8.0
ROPE_THETA = 10000.0
GRID_W = 64
DECAY_SCALE = math.exp(-0.5)
CONV_K = 4
CONV_LEFT = 2
Q_SCALE = HEAD ** -0.5 * math.log2(math.e)
V_ONES = 16
RG_LANES = 256
WKV_CHUNK = 64
WKV_LANES = 256
WKV_GROUPS = 4

NT_DIMS = (((1,), (1,)), ((), ()))
TN_DIMS = (((0,), (0,)), ((), ()))


def _params(sem):
    return pltpu.CompilerParams(dimension_semantics=sem, vmem_limit_bytes=VMEM_LIMIT)


def _const_spec(shape):
    nd = len(shape)
    return pl.BlockSpec(shape, lambda *_: (0,) * nd, pipeline_mode=pl.Buffered(1))


def _dot(a, b):
    return jnp.dot(a, b, preferred_element_type=F32)


def _dot_exact(a, b):
    return jnp.dot(a, b, preferred_element_type=F32, precision=lax.Precision.HIGHEST)


def _split(x):
    hi = x.astype(BF16)
    lo = (x - hi.astype(F32)).astype(BF16)
    return hi, lo


def _group_reduce(x, ones_ref):
    cw = ones_ref.shape[0]
    ones = ones_ref[...]
    outs = []
    for c in range(x.shape[1] // cw):
        hi, lo = _split(x[:, c * cw:(c + 1) * cw])
        outs.append(_dot(hi, ones) + _dot(lo, ones))
    return outs[0] if len(outs) == 1 else jnp.concatenate(outs, axis=1)


def _sigmoid(x):
    return 0.5 * jnp.tanh(0.5 * x) + 0.5


def _norm_mod(x, g, shift, scale):
    ms = jnp.mean(x * x, axis=-1, keepdims=True)
    return x * lax.rsqrt(ms + EPS) * (g * (1.0 + scale)) + shift


def _ada_kernel(c_ref, w_ref, b_ref, o_ref):
    c = c_ref[...]
    s = c * _sigmoid(c)
    o_ref[0, 0] = _dot_exact(s, w_ref[0]) + b_ref[0]


def _ada_mods(cc, ada_w, ada_b):
    depth, d, _ = ada_w.shape
    rows = cc.shape[0]
    out = pl.pallas_call(
        _ada_kernel,
        grid=(depth, 6),
        in_specs=[pl.BlockSpec((rows, d), lambda l, j: (0, 0)),
                  pl.BlockSpec((1, d, d), lambda l, j: (l, 0, j)),
                  pl.BlockSpec((1, 1, d), lambda l, j: (l * 6 + j, 0, 0))],
        out_specs=pl.BlockSpec((1, 1, rows, d), lambda l, j: (l, j, 0, 0)),
        out_shape=jax.ShapeDtypeStruct((depth, 6, rows, d), F32),
        compiler_params=_params(("arbitrary", "arbitrary")),
    )(cc, ada_w, ada_b.reshape(depth * 6, 1, d))
    return jnp.transpose(out, (0, 2, 1, 3))


def _hy_in_kernel(x_ref, mod_ref, g_ref, w_ref, cos_ref, sa_ref, sb_ref, qg_ref, kg_ref, oq_ref, ok_ref,
                  q_out, k_out, vt_out, xr_out, gl_out, *, q_w, kv_w, rnn_w):
    x = x_ref[0]
    h = _norm_mod(x, g_ref[...], mod_ref[0, 0:1, :], mod_ref[0, 1:2, :]).astype(BF16)
    z = _dot(h, w_ref[...])
    c0, c1, c2, c3 = q_w, q_w + kv_w, q_w + 2 * kv_w, q_w + 2 * kv_w + rnn_w

    def norm_rope(u, gain, ones_ref):
        width = u.shape[1]
        un = u * lax.rsqrt(_group_reduce(u * u, ones_ref) + EPS) * gain
        reps = width // LANES
        tile = lambda r: jnp.concatenate([r[...]] * reps, axis=1) if reps > 1 else r[...]
        nxt = pltpu.roll(un, width - 1, 1)
        prv = pltpu.roll(un, 1, 1)
        return un * tile(cos_ref) + nxt * tile(sa_ref) + prv * tile(sb_ref)

    q_out[0] = (norm_rope(z[:, :c0], qg_ref[...], oq_ref) * Q_SCALE).astype(BF16)
    k_out[0] = norm_rope(z[:, c0:c1], kg_ref[...], ok_ref).astype(BF16)
    vt = z[:, c1:c2].T
    ones = jnp.ones((V_ONES, TM), F32)
    for hk in range(kv_w // HEAD):
        vt_out[0, hk] = jnp.concatenate([vt[hk * HEAD:(hk + 1) * HEAD], ones], axis=0).astype(BF16)
    xr_out[0] = z[:, c2:c3]
    gl_out[0] = z[:, c3:]


def _attn_kernel(q_ref, k_ref, vt_ref, o_ref, ot_ref, sa_ref, sb_ref, *, ct, nt, groups, n_kv):
    t = pl.program_id(1)
    nkv = jnp.where(t < ct, ct, nt)
    q = q_ref[0].astype(F32)
    lane = lax.broadcasted_iota(jnp.int32, (TM, LANES), 1)
    low = lane < HEAD
    blocks = []
    for head in range(groups * n_kv):
        hk = head // groups
        qc = q[:, LANES * (head // 2):LANES * (head // 2 + 1)]
        if head % 2 != hk % 2:
            qc = pltpu.roll(qc, HEAD, 1)
        blocks.append((hk, jnp.where(low if hk % 2 == 0 else jnp.logical_not(low), qc, 0.0).astype(BF16)))

    def scores(i, dst_ref, heads):
        r0 = pl.multiple_of(i * TM, TM)
        for head in heads:
            hk, qm = blocks[head]
            kc = k_ref[0, pl.ds(r0, TM), LANES * (hk // 2):LANES * (hk // 2 + 1)]
            dst_ref[head] = lax.dot_general(kc, qm, NT_DIMS, preferred_element_type=F32)

    def absorb(i, src_ref, stats, nxt=None):
        r0 = pl.multiple_of(i * TM, TM)
        out = []
        for head, ((hk, _), (m, acc)) in enumerate(zip(blocks, stats)):
            if nxt is not None:
                scores(nxt[0], nxt[1], [head])
            st = src_ref[head]
            m_new = jnp.maximum(m, jnp.max(st, axis=0, keepdims=True))
            out.append((m_new, _dot(vt_ref[0, hk, :, pl.ds(r0, TM)], jnp.exp2(st - m_new).astype(BF16))))
        return tuple((m_new, jnp.exp2(m - m_new) * acc + pv) for (m_new, pv), (m, acc) in zip(out, stats))

    def pair(j, stats):
        stats = absorb(2 * j, sa_ref, stats, (2 * j + 1, sb_ref))
        return absorb(2 * j + 1, sb_ref, stats, (2 * j + 2, sa_ref))

    scores(0, sa_ref, range(len(blocks)))
    init = tuple((jnp.full((1, TM), -jnp.inf, F32), jnp.zeros((vt_ref.shape[2], TM), F32)) for _ in blocks)
    res = absorb(nkv - 1, sa_ref, lax.fori_loop(0, (nkv - 1) // 2, pair, init))
    for head, (_, acc) in enumerate(res):
        ot_ref[HEAD * head:HEAD * (head + 1), :] = acc[:HEAD] / acc[HEAD:HEAD + 1]
    o_ref[0] = ot_ref[...].T.astype(BF16)


def _scan8(a, u, reverse):
    row = lax.broadcasted_iota(jnp.int32, a.shape, 0)
    for d in (1, 2, 4):
        if reverse:
            a_s, u_s, ok = pltpu.roll(a, SUBLANES - d, 0), pltpu.roll(u, SUBLANES - d, 0), row < SUBLANES - d
        else:
            a_s, u_s, ok = pltpu.roll(a, d, 0), pltpu.roll(u, d, 0), row >= d
        u = a * jnp.where(ok, u_s, 0.0) + u
        a = a * jnp.where(ok, a_s, 1.0)
    return a, u


def _rglru_kernel(xr_ref, gl_ref, cw_ref, cb_ref, gw_ref, gb_ref, lam_ref, o_ref,
                  xs_ref, a_ref, u_ref, *, cl, n):
    t_all = cl + n
    pad = SUBLANES
    width = xs_ref.shape[1]
    xs_ref[0:pad, :] = jnp.zeros((pad, width), F32)
    xs_ref[pad + t_all:, :] = jnp.zeros((pad, width), F32)
    xs_ref[pad:pad + t_all, :] = xr_ref[0]
    lam = lam_ref[...]
    z = -lam
    softplus = jnp.maximum(z, 0.0) + jnp.log(1.0 + jnp.exp(-jnp.abs(z)))
    cw = cw_ref[...]
    cb = cb_ref[...]
    gb = gb_ref[...]

    def coeffs(i, _):
        r0 = pl.multiple_of(i * TM, TM)
        blk = xs_ref[pl.ds(r0, TM + 2 * pad), :]
        rows = r0 + lax.broadcasted_iota(jnp.int32, (TM, 1), 0)
        in_lat = rows >= cl
        pos = jnp.where(in_lat, rows - cl, rows)
        seqlen = jnp.where(in_lat, n, cl)
        xc = jnp.zeros((TM, width), F32) + cb
        for j in range(CONV_K):
            off = j - CONV_LEFT
            tap = blk[pad + off:pad + off + TM, :]
            ok = jnp.logical_and(pos + off >= 0, pos + off < seqlen)
            xc = xc + jnp.where(ok, tap, 0.0) * cw[j:j + 1, :]
        xcb = xc.astype(BF16)
        for d in range(2):
            r = _sigmoid(_dot(xcb, gw_ref[0, 2 * d]) + gb[2 * d:2 * d + 1, :])
            ig = _sigmoid(_dot(xcb, gw_ref[0, 2 * d + 1]) + gb[2 * d + 1:2 * d + 2, :])
            a = jnp.exp(-RG_C * r * softplus[d:d + 1, :])
            a_ref[d, pl.ds(r0, TM), :] = a
            u_ref[d, pl.ds(r0, TM), :] = jnp.sqrt(1.0 - a * a) * (ig * xc)
        return 0

    lax.fori_loop(0, t_all // TM, coeffs, 0)

    g_all, g_ctx = t_all // SUBLANES, cl // SUBLANES

    def step(i, carry):
        h_fwd, h_rev = carry
        g_rev = jnp.where(i < g_ctx, g_ctx - 1 - i, g_all - 1 - (i - g_ctx))
        sf = pl.ds(pl.multiple_of(i * SUBLANES, SUBLANES), SUBLANES)
        sr = pl.ds(pl.multiple_of(g_rev * SUBLANES, SUBLANES), SUBLANES)
        af, uf = _scan8(a_ref[0, sf, :], u_ref[0, sf, :], False)
        ar, ur = _scan8(a_ref[1, sr, :], u_ref[1, sr, :], True)
        hf = af * h_fwd + uf
        hr = ar * h_rev + ur
        u_ref[0, sf, :] = hf
        u_ref[1, sr, :] = hr
        return hf[SUBLANES - 1:SUBLANES, :], hr[0:1, :]

    zero = jnp.zeros((1, width), F32)
    lax.fori_loop(0, g_all, step, (zero, zero), unroll=4)

    def combine(i, _):
        r0 = pl.multiple_of(i * TM, TM)
        hsum = u_ref[0, pl.ds(r0, TM), :] + u_ref[1, pl.ds(r0, TM), :]
        o_ref[0, pl.ds(r0, TM), :] = (jax.nn.gelu(gl_ref[0, pl.ds(r0, TM), :]) * hsum).astype(BF16)
        return 0

    lax.fori_loop(0, t_all // TM, combine, 0)


def _rw_in_kernel(x_ref, xp_ref, xn_ref, mod_ref, g_ref, mu_ref, wrkv_ref, wdw_ref, wda_ref, wuw_ref, wua_ref,
                  lb_ref, gd_ref, gu_ref, kk_ref, ka_ref, rk_ref, gnb_ref, ones_ref,
                  r_out, kk_out, v_out, lw_out, bb_out, kd_out, g_out, z_out, *, ct, nt):
    t = pl.program_id(1)
    g = g_ref[...]
    shift, scale = mod_ref[0, 0:1, :], mod_ref[0, 1:2, :]
    h = _norm_mod(x_ref[0], g, shift, scale)
    first = jnp.logical_or(t == 0, t == ct)
    last = jnp.logical_or(t == ct - 1, t == nt - 1)
    hp = _norm_mod(xp_ref[0], g, shift, scale)[SUBLANES - 1:SUBLANES, :]
    hn = _norm_mod(xn_ref[0], g, shift, scale)[0:1, :]
    hp = jnp.where(first, 0.0, hp)
    hn = jnp.where(last, 0.0, hn)
    row = lax.broadcasted_iota(jnp.int32, (TM, 1), 0)
    h_prev = jnp.where(row == 0, hp, pltpu.roll(h, 1, 0))
    h_next = jnp.where(row == TM - 1, hn, pltpu.roll(h, TM - 1, 0))
    xx = 0.5 * (h_prev + h_next) - h
    lerp = lambda j: (h + xx * mu_ref[j:j + 1, :]).astype(BF16)

    r = _dot(lerp(0), wrkv_ref[0])
    k = _dot(lerp(2), wrkv_ref[1])
    v = _dot(lerp(3), wrkv_ref[2])
    gate = _dot(_sigmoid(_dot(lerp(5), gd_ref[...])).astype(BF16), gu_ref[...])
    tw = jnp.tanh(_dot(lerp(1), wdw_ref[...])).astype(BF16)
    ta = _dot(lerp(4), wda_ref[...]).astype(BF16)

    kk = k * kk_ref[...]
    nrm = jnp.sqrt(_group_reduce(kk * kk, ones_ref))
    kk = kk / jnp.maximum(nrm, 1e-12)
    r_out[0] = r
    kk_out[0] = kk
    v_out[0] = v
    kd_sum = jnp.zeros_like(k)
    for d in range(2):
        dec = lb_ref[2 * d:2 * d + 1, :] + _dot(tw, wuw_ref[d])
        a = _sigmoid(lb_ref[2 * d + 1:2 * d + 2, :] + _dot(ta, wua_ref[d]))
        kd = k * (1.0 + (a - 1.0) * ka_ref[...])
        lw_out[d, 0] = -DECAY_SCALE * _sigmoid(dec)
        bb_out[d, 0] = kk * a
        kd_out[d, 0] = kd
        kd_sum = kd_sum + kd
    bonus = _group_reduce(r * kd_sum * rk_ref[...], ones_ref)
    g_out[0] = gate
    z_out[0] = (gnb_ref[...] + bonus * v) * gate


def _wkv_kernel(r_ref, kk_ref, v_ref, lw_ref, bb_ref, kd_ref, y_ref, h_ref):
    c, lw_n = WKV_CHUNK, WKV_LANES
    n_heads = lw_n // HEAD
    s_rows = n_heads * c
    d = pl.program_id(0)
    i = pl.program_id(3)

    @pl.when(i == 0)
    def _():
        h_ref[...] = jnp.zeros_like(h_ref)

    rev = d == 1
    crow = lax.broadcasted_iota(jnp.int32, (c, lw_n), 0)
    srow = lax.broadcasted_iota(jnp.int32, (s_rows, s_rows), 0)
    scol = lax.broadcasted_iota(jnp.int32, (s_rows, s_rows), 1)
    ahead = jnp.where(rev, scol % c - srow % c, srow % c - scol % c)
    strict = ahead > 0
    incl = ahead >= 0
    eye = (srow == scol).astype(F32)
    same_block = lambda size: (srow // size) == (scol // size)
    lrow =lax.broadcasted_iota(jnp.int32, (lw_n, lw_n), 0)
    lcol = lax.broadcasted_iota(jnp.int32, (lw_n, lw_n), 1)
    same_head = (lrow // HEAD) == (lcol // HEAD)
    diag = lrow == lcol
    head_of_lane = lax.broadcasted_iota(jnp.int32, (s_rows, lw_n), 1) // HEAD
    head_of_row = lax.broadcasted_iota(jnp.int32, (s_rows, lw_n), 0) // c
    own = head_of_lane == head_of_row
    n_chunks = TM // c

    def stack(x):
        return jnp.where(own, jnp.concatenate([x] * n_heads, axis=0), 0.0)

    def unstack(xs):
        out = xs[0:c]
        for hh in range(1, n_heads):
            out = out + xs[hh * c:(hh + 1) * c]
        return out

    def local(grp, ci):
        cidx = jnp.where(rev, n_chunks - 1 - ci, ci)
        sl = pl.ds(pl.multiple_of(cidx * c, c), c)
        ls = slice(grp * lw_n, (grp + 1) * lw_n)
        r, kk, v = r_ref[0, sl, ls], kk_ref[0, sl, ls], v_ref[0, sl, ls]
        lw, bb, kd = lw_ref[0, 0, sl, ls], bb_ref[0, 0, sl, ls], kd_ref[0, 0, sl, ls]
        pre = lw
        shift = 1
        while shift < c:
            pre = pre + jnp.where(crow >= shift, pltpu.roll(pre, shift, 0), 0.0)
            shift *= 2
        tot = pre[c - 1:c, :]
        cum = jnp.where(rev, tot - pre + lw, pre)
        yield
        e_neg = jnp.exp(-cum)
        e_end = jnp.exp(tot - cum)
        kd_g, b_g = kd * e_end, bb * e_end
        kk_s = stack(kk * jnp.exp(cum - lw))
        r_s = stack(r * jnp.exp(cum))
        v_s = stack(v).astype(BF16)
        lhs = jnp.concatenate([kk_s, r_s], axis=0)
        yield
        p1 = _bdot(lhs, stack(kd * e_neg), NT_DIMS)
        yield
        p2 = _bdot(lhs, stack(bb * e_neg), NT_DIMS)
        yield
        a_kd = jnp.where(strict, p1[:s_rows], 0.0)
        b_kd = jnp.where(incl, p1[s_rows:], 0.0)
        nmat = jnp.where(strict, p2[:s_rows], 0.0)
        b_b = jnp.where(incl, p2[s_rows:], 0.0).astype(BF16)
        tinv = eye - jnp.where(same_block(2), nmat, 0.0)
        size = 2
        while size < c:
            n_off = jnp.where(jnp.logical_and(same_block(2 * size), jnp.logical_not(same_block(size))), nmat, 0.0)
            tb = tinv.astype(BF16)
            nt = _bdot(n_off, tb)
            yield
            tinv = tinv - _bdot(tb, nt)
            yield
            size *= 2
        av = _bdot(a_kd, v_s)
        yield
        tw = _bdot(tinv, jnp.concatenate([kk_s, av], axis=1))
        yield
        bw = _bdot(b_b, tw)
        yield
        kkp, u0 = unstack(tw[:, :lw_n]), unstack(tw[:, lw_n:])
        rp = unstack(r_s - bw[:, :lw_n])
        y0 = unstack(_bdot(b_kd, v_s) - bw[:, lw_n:])
        m_mat = jnp.where(diag, jnp.exp(tot), 0.0) - jnp.where(same_head, _bdot(b_g, kkp, TN_DIMS), 0.0)
        g_mat = jnp.where(same_head, _bdot(jnp.concatenate([kd_g, b_g], axis=0),
                                           jnp.concatenate([v, -u0], axis=0), TN_DIMS), 0.0)
        return grp, sl, ls, rp, y0, m_mat, g_mat

    chains = [local(grp, ci) for ci in range(n_chunks) for grp in range(h_ref.shape[0])]
    parts = [None] * len(chains)
    while any(p is None for p in parts):
        for idx, chain in enumerate(chains):
            if parts[idx] is None:
                try:
                    next(chain)
                except StopIteration as done:
                    parts[idx] = done.value
    hs = [h_ref[grp] for grp in range(h_ref.shape[0])]
    for grp, sl, ls, rp, y0, m_mat, g_mat in parts:
        y_ref[0, 0, sl, ls] = _bdot(rp, hs[grp]) + y0
        hs[grp] = _bdot(m_mat, hs[grp]) + g_mat
    for grp, h_new in enumerate(hs):
        h_ref[grp] = h_new


def _bdot(a, b, dims=(((1,), (0,)), ((), ()))):
    return lax.dot_general(a.astype(BF16), b.astype(BF16), dims, preferred_element_type=F32)


def _post_kernel(*refs, odd, ff_chunk):
    if odd:
        (x_ref, y0_ref, y1_ref, gate_ref, z_ref, gng_ref, ones_ref,
         wo_ref, mod_ref, g2_ref, w1_ref, w2_ref, o_ref) = refs
        y = y0_ref[0, 0] + y1_ref[0, 0]
        dlt = y - _group_reduce(y, ones_ref)
        var = _group_reduce(dlt * dlt, ones_ref)
        mix = (dlt * lax.rsqrt(var + GN_EPS) * gng_ref[...] * gate_ref[0] + z_ref[0]).astype(BF16)
    else:
        x_ref, att_ref, rec_ref, wo_ref, mod_ref, g2_ref, w1_ref, w2_ref, o_ref = refs
        mix = jnp.concatenate([att_ref[0], rec_ref[0]], axis=1)
    x1 = x_ref[0] + mod_ref[0, 2:3, :] * _dot(mix, wo_ref[...])
    h2 = _norm_mod(x1, g2_ref[...], mod_ref[0, 3:4, :], mod_ref[0, 4:5, :]).astype(BF16)
    acc = jnp.zeros_like(x1)
    for c in range(w1_ref.shape[1] // ff_chunk):
        a = _dot(h2, w1_ref[:, c * ff_chunk:(c + 1) * ff_chunk])
        a = jnp.square(jnp.maximum(a, 0.0)).astype(BF16)
        acc = acc + _dot(a, w2_ref[c * ff_chunk:(c + 1) * ff_chunk, :])
    o_ref[0] = x1 + mod_ref[0, 5:6, :] * acc


def _blockdiag_ones(width, value):
    idx = jnp.arange(width) // HEAD
    return jnp.where(idx[:, None] == idx[None, :], value, 0.0).astype(BF16)


def _rope_tables(cl, n, grid_w):
    rows = n // grid_w
    row = jnp.repeat(jnp.arange(rows, dtype=F32), grid_w)
    col = jnp.tile(jnp.arange(grid_w, dtype=F32), rows)
    half = HEAD // 2
    inv = ROPE_THETA ** (-jnp.arange(0, half, 2, dtype=F32) / half)
    ang = jnp.concatenate([row[:, None] * inv, col[:, None] * inv], axis=-1)
    cos = jnp.repeat(jnp.cos(ang), 2, axis=-1)
    sin = jnp.repeat(jnp.sin(ang), 2, axis=-1)
    even = (jnp.arange(HEAD) % 2 == 0)[None, :]
    sa = jnp.where(even, -sin, 0.0)
    sb = jnp.where(even, 0.0, sin)
    ctx = lambda fill: jnp.full((cl, HEAD), fill, F32)
    full = lambda lat, fill: jnp.tile(jnp.concatenate([ctx(fill), lat], axis=0), (1, LANES // HEAD))
    return full(cos, 1.0), full(sa, 0.0), full(sb, 0.0)


def _tile_specs(b_all, ct, d_model):
    mod_spec = pl.BlockSpec((1, 6, d_model), lambda b, t: (jnp.where(t < ct, b_all, b), 0, 0))
    row_spec = lambda w: pl.BlockSpec((1, TM, w), lambda b, t: (b, t, 0))
    return mod_spec, row_spec


def _hybrid_layer(xs, mods, g1, w_in, w_out, qn, kn, conv_w, conv_b, gate_w, gate_b, lam, ropes, dims):
    b_all, cl, n, d_model = dims
    t_all = cl + n
    ct, nt = cl // TM, t_all // TM
    in_w = w_in.shape[1]
    rnn_w = conv_w.shape[1]
    kv_w = (in_w - 2 * rnn_w - d_model // 2) // 2
    q_w = in_w - 2 * kv_w - 2 * rnn_w
    n_kv = kv_w // HEAD
    assert ct % 2 == 1 and nt % 2 == 1, "the attention kernel walks key tiles in pairs plus one"
    groups = q_w // kv_w
    mod_spec, row_spec = _tile_specs(b_all, ct, d_model)
    rope_spec = pl.BlockSpec((TM, LANES), lambda b, t: (t, 0))
    cos, sa, sb = ropes

    q, k, vt, xr, gl = pl.pallas_call(
        functools.partial(_hy_in_kernel, q_w=q_w, kv_w=kv_w, rnn_w=rnn_w),
        grid=(b_all, nt),
        in_specs=[row_spec(d_model), mod_spec, _const_spec((1, d_model)), _const_spec((d_model, in_w)),
                  rope_spec, rope_spec, rope_spec, _const_spec((1, q_w)), _const_spec((1, kv_w)),
                  _const_spec((MXU, MXU)), _const_spec((kv_w, kv_w))],
        out_specs=[row_spec(q_w), row_spec(kv_w),
                   pl.BlockSpec((1, n_kv, HEAD + V_ONES, TM), lambda b, t: (b, 0, 0, t)),
                   row_spec(rnn_w), row_spec(rnn_w)],
        out_shape=[jax.ShapeDtypeStruct((b_all, t_all, q_w), BF16),
                   jax.ShapeDtypeStruct((b_all, t_all, kv_w), BF16),
                   jax.ShapeDtypeStruct((b_all, n_kv, HEAD + V_ONES, t_all), BF16),
                   jax.ShapeDtypeStruct((b_all, t_all, rnn_w), F32),
                   jax.ShapeDtypeStruct((b_all, t_all, rnn_w), F32)],
        compiler_params=_params(("parallel", "parallel")),
    )(xs, mods, g1.reshape(1, d_model), w_in.astype(BF16), cos, sa, sb,
      jnp.tile(qn, q_w // HEAD).reshape(1, q_w), jnp.tile(kn, kv_w // HEAD).reshape(1, kv_w),
      _blockdiag_ones(MXU, 1.0 / HEAD), _blockdiag_ones(kv_w, 1.0 / HEAD))

    att = pl.pallas_call(
        functools.partial(_attn_kernel, ct=ct, nt=nt, groups=groups, n_kv=n_kv),
        grid=(b_all, nt),
        in_specs=[row_spec(q_w),
                  pl.BlockSpec((1, t_all, kv_w), lambda b, t: (b, 0, 0)),
                  pl.BlockSpec((1, n_kv, HEAD + V_ONES, t_all), lambda b, t: (b, 0, 0, 0))],
        out_specs=row_spec(q_w),
        out_shape=jax.ShapeDtypeStruct((b_all, t_all, q_w), BF16),
        scratch_shapes=[pltpu.VMEM((q_w, TM), F32)] + [pltpu.VMEM((q_w // HEAD, TM, TM), F32)] * 2,
        compiler_params=_params(("parallel", "parallel")),
    )(q, k, vt)

    n_lc = rnn_w // RG_LANES
    per = RG_LANES // HEAD
    gw = gate_w.reshape(4, n_lc, per, HEAD, HEAD)
    eye = jnp.eye(per, dtype=F32)
    gw = jnp.einsum('gcpde,pq->cgpdqe', gw, eye).reshape(n_lc, 4, RG_LANES, RG_LANES).astype(BF16)
    lane_spec = lambda rows: pl.BlockSpec((rows, RG_LANES), lambda b, c: (0, c))
    seq_spec = pl.BlockSpec((1, t_all, RG_LANES), lambda b, c: (b, 0, c))
    rec = pl.pallas_call(
        functools.partial(_rglru_kernel, cl=cl, n=n),
        grid=(b_all, n_lc),
        in_specs=[seq_spec, seq_spec, lane_spec(CONV_K), lane_spec(1),
                  pl.BlockSpec((1, 4, RG_LANES, RG_LANES), lambda b, c: (c, 0, 0, 0)),
                  lane_spec(4), lane_spec(2)],
        out_specs=seq_spec,
        out_shape=jax.ShapeDtypeStruct((b_all, t_all, rnn_w), BF16),
        scratch_shapes=[pltpu.VMEM((t_all + 2 * SUBLANES, RG_LANES), F32),
                        pltpu.VMEM((2, t_all, RG_LANES), F32),
                        pltpu.VMEM((2, t_all, RG_LANES), F32)],
        compiler_params=_params(("parallel", "parallel")),
    )(xr, gl, conv_w, conv_b.reshape(1, rnn_w), gw, gate_b.reshape(4, rnn_w), lam)
    return (att, rec), w_out.astype(BF16)


def _rwkv_layer(xs, mods, g1, mu, w_rkv, lora_down, lora_up, lora_bias, gate_down, gate_up,
                k_k, k_a, r_k, gn_b, dims):
    b_all, cl, n, d_model = dims
    t_all = cl + n
    ct, nt = cl // TM, t_all // TM
    n8 = t_all // SUBLANES
    per8 = TM // SUBLANES
    lora = lora_down.shape[-1]
    glora = gate_down.shape[-1]
    mod_spec, row_spec = _tile_specs(b_all, ct, d_model)
    dir_spec = pl.BlockSpec((2, 1, TM, d_model), lambda b, t: (0, b, t, 0))
    wdw = jnp.concatenate([lora_down[0, 0], lora_down[1, 0]], axis=1).astype(BF16)
    wda = jnp.concatenate([lora_down[0, 1], lora_down[1, 1]], axis=1).astype(BF16)
    zeros = jnp.zeros((lora, d_model), F32)
    pad_up = lambda j: jnp.stack([jnp.concatenate([lora_up[0, j], zeros], axis=0),
                                  jnp.concatenate([zeros, lora_up[1, j]], axis=0)]).astype(BF16)
    vec = lambda a: a.reshape(1, d_model)
    outs = pl.pallas_call(
        functools.partial(_rw_in_kernel, ct=ct, nt=nt),
        grid=(b_all, nt),
        in_specs=[row_spec(d_model),
                  pl.BlockSpec((1, SUBLANES, d_model), lambda b, t: (b, jnp.maximum(t * per8 - 1, 0), 0)),
                  pl.BlockSpec((1, SUBLANES, d_model), lambda b, t: (b, jnp.minimum((t + 1) * per8, n8 - 1), 0)),
                  mod_spec, _const_spec((1, d_model)), _const_spec((6, d_model)),
                  _const_spec((3, d_model, d_model)), _const_spec((d_model, 2 * lora)),
                  _const_spec((d_model, 2 * lora)), _const_spec((2, 2 * lora, d_model)),
                  _const_spec((2, 2 * lora, d_model)), _const_spec((4, d_model)),
                  _const_spec((d_model, glora)), _const_spec((glora, d_model)),
                  _const_spec((1, d_model)), _const_spec((1, d_model)), _const_spec((1, d_model)),
                  _const_spec((1, d_model)), _const_spec((MXU, MXU))],
        out_specs=[row_spec(d_model), row_spec(d_model), row_spec(d_model),
                   dir_spec, dir_spec, dir_spec, row_spec(d_model), row_spec(d_model)],
        out_shape=[jax.ShapeDtypeStruct((b_all, t_all, d_model), F32)] * 3
        + [jax.ShapeDtypeStruct((2, b_all, t_all, d_model), F32)] * 3
        + [jax.ShapeDtypeStruct((b_all, t_all, d_model), F32)] * 2,
        compiler_params=_params(("parallel", "parallel")),
    )(xs, xs, xs, mods, vec(g1), mu, w_rkv.astype(BF16), wdw, wda, pad_up(0), pad_up(1),
      lora_bias.reshape(4, d_model), gate_down.astype(BF16), gate_up.astype(BF16),
      vec(k_k), vec(k_a), vec(r_k), vec(gn_b), _blockdiag_ones(MXU, 1.0))
    r, kk, v, lw, bb, kd, gate, z = outs

    def tmap(dd, i):
        rev = jnp.where(i < ct, ct - 1 - i, nt - 1 - (i - ct))
        return jnp.where(dd == 0, i, rev)

    blk_lanes = WKV_LANES * WKV_GROUPS
    shared = pl.BlockSpec((1, TM, blk_lanes), lambda dd, b, hh, i: (b, tmap(dd, i), hh))
    per_dir = pl.BlockSpec((1, 1, TM, blk_lanes), lambda dd, b, hh, i: (dd, b, tmap(dd, i), hh))
    y = pl.pallas_call(
        _wkv_kernel,
        grid=(2, b_all, d_model // blk_lanes, nt),
        in_specs=[shared, shared, shared, per_dir, per_dir, per_dir],
        out_specs=per_dir,
        out_shape=jax.ShapeDtypeStruct((2, b_all, t_all, d_model), F32),
        scratch_shapes=[pltpu.VMEM((WKV_GROUPS, WKV_LANES, WKV_LANES), F32)],
        compiler_params=_params(("arbitrary", "arbitrary", "arbitrary", "arbitrary")),
    )(r, kk, v, lw, bb, kd)
    return y, gate, z


def _post(xs, mix_inputs, w_o, mods, g2, w1, w2, dims, odd, gn_g=None, latent_only=False):
    b_all, cl, n, d_model = dims
    t_all = cl + n
    ct, nt = cl // TM, t_all // TM
    t0 = ct if latent_only else 0
    d_ff = w1.shape[1]
    mod_spec = pl.BlockSpec((1, 6, d_model), lambda b, t: (jnp.where(t + t0 < ct, b_all, b), 0, 0))
    row_spec = lambda w: pl.BlockSpec((1, TM, w), lambda b, t: (b, t + t0, 0))
    tail_specs = [_const_spec((d_model, d_model)), mod_spec, _const_spec((1, d_model)),
                  _const_spec((d_model, d_ff)), _const_spec((d_ff, d_model))]
    tail_args = (w_o, mods, g2.reshape(1, d_model), w1.astype(BF16), w2.astype(BF16))
    if odd:
        y, gate, z = mix_inputs
        y_spec = lambda dd: pl.BlockSpec((1, 1, TM, d_model), lambda b, t: (dd, b, t + t0, 0))
        in_specs = [row_spec(d_model), y_spec(0), y_spec(1), row_spec(d_model), row_spec(d_model),
                    _const_spec((1, d_model)), _const_spec((MXU, MXU))] + tail_specs
        args = (xs, y, y, gate, z, gn_g.reshape(1, d_model), _blockdiag_ones(MXU, 1.0 / HEAD)) + tail_args
    else:
        att, rec = mix_inputs
        in_specs = [row_spec(d_model), row_spec(att.shape[-1]), row_spec(rec.shape[-1])] + tail_specs
        args = (xs, att, rec) + tail_args
    rows_out = n if latent_only else t_all
    return pl.pallas_call(
        functools.partial(_post_kernel, odd=odd, ff_chunk=min(d_ff, 4 * MXU)),
        grid=(b_all, nt - t0),
        in_specs=in_specs,
        out_specs=pl.BlockSpec((1, TM, d_model), lambda b, t: (b, t, 0)),
        out_shape=jax.ShapeDtypeStruct((b_all, rows_out, d_model), F32),
        compiler_params=_params(("parallel", "parallel")),
    )(*args)


def kernel(x, c, ctx, c_ctx, ada_w, ada_b, norm_g, mlp_w1, mlp_w2, hy_w_in, hy_w_out, hy_q_norm, hy_k_norm, hy_conv_w, hy_conv_b, hy_gate_w, hy_gate_b, hy_lam, rw_mu, rw_w_rkv, rw_w_o, rw_lora_down, rw_lora_up, rw_lora_bias, rw_gate_down, rw_gate_up, rw_k_k, rw_k_a, rw_r_k, rw_gn_g, rw_gn_b):
    b_all, n, d_model = x.shape
    cl = ctx.shape[1]
    depth = ada_w.shape[0]
    assert cl % TM == 0 and n % TM == 0 and d_model % MXU == 0
    dims = (b_all, cl, n, d_model)
    mods_all = _ada_mods(jnp.concatenate([c, c_ctx[None, :]], axis=0), ada_w, ada_b)
    ropes = _rope_tables(cl, n, GRID_W)
    xs = jnp.concatenate([ctx, x], axis=1)
    for l in range(depth):
        i = l // 2
        mods = mods_all[l]
        last = l == depth - 1
        if l % 2 == 0:
            mix, w_o = _hybrid_layer(xs, mods, norm_g[l, 0], hy_w_in[i], hy_w_out[i], hy_q_norm[i], hy_k_norm[i],
                                     hy_conv_w[i], hy_conv_b[i], hy_gate_w[i], hy_gate_b[i], hy_lam[i], ropes, dims)
            xs = _post(xs, mix, w_o, mods, norm_g[l, 1], mlp_w1[l], mlp_w2[l], dims, odd=False, latent_only=last)
        else:
            mix = _rwkv_layer(xs, mods, norm_g[l, 0], rw_mu[i], rw_w_rkv[i], rw_lora_down[i], rw_lora_up[i],
                              rw_lora_bias[i], rw_gate_down[i], rw_gate_up[i], rw_k_k[i], rw_k_a[i],
                              rw_r_k[i], rw_gn_b[i], dims)
            xs = _post(xs, mix, rw_w_o[i].astype(BF16), mods, norm_g[l, 1], mlp_w1[l], mlp_w2[l], dims,
                       odd=True, gn_g=rw_gn_g[i], latent_only=last)
    return xs if xs.shape[1] == n else xs[:, cl:]
```

```python
import functools
import math

import jax
import jax.numpy as jnp
from jax import lax
from jax.experimental import pallas as pl
from jax.experimental.pallas import tpu as pltpu

F32 = jnp.float32
BF16 = jnp.bfloat16

HEAD = 64
LANES = 128
SUBLANES = 8
MXU = 256
TM = 256
VMEM_LIMIT = 56 * 1024 * 1024

EPS = 1e-6
GN_EPS = 64e-5
RG_C = 8.0
ROPE_THETA = 10000.0
GRID_W = 64
DECAY_SCALE = math.exp(-0.5)
CONV_K = 4
CONV_LEFT = 2
Q_SCALE = HEAD ** -0.5 * math.log2(math.e)
V_ONES = 16
RG_LANES = 256
WKV_CHUNK = 64
WKV_LANES = 256
WKV_GROUPS = 4

NT_DIMS = (((1,), (1,)), ((), ()))
TN_DIMS = (((0,), (0,)), ((), ()))


def _params(sem):
    return pltpu.CompilerParams(dimension_semantics=sem, vmem_limit_bytes=VMEM_LIMIT)


def _const_spec(shape):
    nd = len(shape)
    return pl.BlockSpec(shape, lambda *_: (0,) * nd, pipeline_mode=pl.Buffered(1))


def _dot(a, b):
    return jnp.dot(a, b, preferred_element_type=F32)


def _dot_exact(a, b):
    return jnp.dot(a, b, preferred_element_type=F32, precision=lax.Precision.HIGHEST)


def _split(x):
    hi = x.astype(BF16)
    lo = (x - hi.astype(F32)).astype(BF16)
    return hi, lo


def _group_reduce(x, ones_ref):
    cw = ones_ref.shape[0]
    ones = ones_ref[...]
    outs = []
    for c in range(x.shape[1] // cw):
        hi, lo = _split(x[:, c * cw:(c + 1) * cw])
        outs.append(_dot(hi, ones) + _dot(lo, ones))
    return outs[0] if len(outs) == 1 else jnp.concatenate(outs, axis=1)


def _sigmoid(x):
    return 0.5 * jnp.tanh(0.5 * x) + 0.5


def _norm_mod(x, g, shift, scale):
    ms = jnp.mean(x * x, axis=-1, keepdims=True)
    return x * lax.rsqrt(ms + EPS) * (g * (1.0 + scale)) + shift


def _ada_kernel(c_ref, w_ref, b_ref, o_ref):
    c = c_ref[...]
    s = c * _sigmoid(c)
    o_ref[0, 0] = _dot_exact(s, w_ref[0]) + b_ref[0]


def _ada_mods(cc, ada_w, ada_b):
    depth, d, _ = ada_w.shape
    rows = cc.shape[0]
    out = pl.pallas_call(
        _ada_kernel,
        grid=(depth, 6),
        in_specs=[pl.BlockSpec((rows, d), lambda l, j: (0, 0)),
                  pl.BlockSpec((1, d, d), lambda l, j: (l, 0, j)),
                  pl.BlockSpec((1, 1, d), lambda l, j: (l * 6 + j, 0, 0))],
        out_specs=pl.BlockSpec((1, 1, rows, d), lambda l, j: (l, j, 0, 0)),
        out_shape=jax.ShapeDtypeStruct((depth, 6, rows, d), F32),
        compiler_params=_params(("arbitrary", "arbitrary")),
    )(cc, ada_w, ada_b.reshape(depth * 6, 1, d))
    return jnp.transpose(out, (0, 2, 1, 3))


def _hy_in_kernel(x_ref, mod_ref, g_ref, w_ref, cos_ref, sa_ref, sb_ref, qg_ref, kg_ref, oq_ref, ok_ref,
                  q_out, k_out, vt_out, xr_out, gl_out, *, q_w, kv_w, rnn_w):
    x = x_ref[0]
    h = _norm_mod(x, g_ref[...], mod_ref[0, 0:1, :], mod_ref[0, 1:2, :]).astype(BF16)
    z = _dot(h, w_ref[...])
    c0, c1, c2, c3 = q_w, q_w + kv_w, q_w + 2 * kv_w, q_w + 2 * kv_w + rnn_w

    def norm_rope(u, gain, ones_ref):
        width = u.shape[1]
        un = u * lax.rsqrt(_group_reduce(u * u, ones_ref) + EPS) * gain
        reps = width // LANES
        tile = lambda r: jnp.concatenate([r[...]] * reps, axis=1) if reps > 1 else r[...]
        nxt = pltpu.roll(un, width - 1, 1)
        prv = pltpu.roll(un, 1, 1)
        return un * tile(cos_ref) + nxt * tile(sa_ref) + prv * tile(sb_ref)

    q_out[0] = (norm_rope(z[:, :c0], qg_ref[...], oq_ref) * Q_SCALE).astype(BF16)
    k_out[0] = norm_rope(z[:, c0:c1], kg_ref[...], ok_ref).astype(BF16)
    vt = z[:, c1:c2].T
    ones = jnp.ones((V_ONES, TM), F32)
    for hk in range(kv_w // HEAD):
        vt_out[0, hk] = jnp.concatenate([vt[hk * HEAD:(hk + 1) * HEAD], ones], axis=0).astype(BF16)
    xr_out[0] = z[:, c2:c3]
    gl_out[0] = z[:, c3:]


def _attn_kernel(q_ref, k_ref, vt_ref, o_ref, ot_ref, sa_ref, sb_ref, *, ct, nt, groups, n_kv):
    t = pl.program_id(1)
    nkv = jnp.where(t < ct, ct, nt)
    q = q_ref[0].astype(F32)
    lane = lax.broadcasted_iota(jnp.int32, (TM, LANES), 1)
    low = lane < HEAD
    blocks = []
    for head in range(groups * n_kv):
        hk = head // groups
        qc = q[:, LANES * (head // 2):LANES * (head // 2 + 1)]
        if head % 2 != hk % 2:
            qc = pltpu.roll(qc, HEAD, 1)
        blocks.append((hk, jnp.where(low if hk % 2 == 0 else jnp.logical_not(low), qc, 0.0).astype(BF16)))

    def scores(i, dst_ref, heads):
        r0 = pl.multiple_of(i * TM, TM)
        for head in heads:
            hk, qm = blocks[head]
            kc = k_ref[0, pl.ds(r0, TM), LANES * (hk // 2):LANES * (hk // 2 + 1)]
            dst_ref[head] = lax.dot_general(kc, qm, NT_DIMS, preferred_element_type=F32)

    def absorb(i, src_ref, stats, nxt=None):
        r0 = pl.multiple_of(i * TM, TM)
        out = []
        for head, ((hk, _), (m, acc)) in enumerate(zip(blocks, stats)):
            if nxt is not None:
                scores(nxt[0], nxt[1], [head])
            st = src_ref[head]
            m_new = jnp.maximum(m, jnp.max(st, axis=0, keepdims=True))
            out.append((m_new, _dot(vt_ref[0, hk, :, pl.ds(r0, TM)], jnp.exp2(st - m_new).astype(BF16))))
        return tuple((m_new, jnp.exp2(m - m_new) * acc + pv) for (m_new, pv), (m, acc) in zip(out, stats))

    def pair(j, stats):
        stats = absorb(2 * j, sa_ref, stats, (2 * j + 1, sb_ref))
        return absorb(2 * j + 1, sb_ref, stats, (2 * j + 2, sa_ref))

    scores(0, sa_ref, range(len(blocks)))
    init = tuple((jnp.full((1, TM), -jnp.inf, F32), jnp.zeros((vt_ref.shape[2], TM), F32)) for _ in blocks)
    res = absorb(nkv - 1, sa_ref, lax.fori_loop(0, (nkv - 1) // 2, pair, init))
    for head, (_, acc) in enumerate(res):
        ot_ref[HEAD * head:HEAD * (head + 1), :] = acc[:HEAD] / acc[HEAD:HEAD + 1]
    o_ref[0] = ot_ref[...].T.astype(BF16)


def _scan8(a, u, reverse):
    row = lax.broadcasted_iota(jnp.int32, a.shape, 0)
    for d in (1, 2, 4):
        if reverse:
            a_s, u_s, ok = pltpu.roll(a, SUBLANES - d, 0), pltpu.roll(u, SUBLANES - d, 0), row < SUBLANES - d
        else:
            a_s, u_s, ok = pltpu.roll(a, d, 0), pltpu.roll(u, d, 0), row >= d
        u = a * jnp.where(ok, u_s, 0.0) + u
        a = a * jnp.where(ok, a_s, 1.0)
    return a, u


def _rglru_kernel(xr_ref, gl_ref, cw_ref, cb_ref, gw_ref, gb_ref, lam_ref, o_ref,
                  xs_ref, a_ref, u_ref, *, cl, n):
    t_all = cl + n
    pad = SUBLANES
    width = xs_ref.shape[1]
    xs_ref[0:pad, :] = jnp.zeros((pad, width), F32)
    xs_ref[pad + t_all:, :] = jnp.zeros((pad, width), F32)
    xs_ref[pad:pad + t_all, :] = xr_ref[0]
    lam = lam_ref[...]
    z = -lam
    softplus = jnp.maximum(z, 0.0) + jnp.log(1.0 + jnp.exp(-jnp.abs(z)))
    cw = cw_ref[...]
    cb = cb_ref[...]
    gb = gb_ref[...]

    def coeffs(i, _):
        r0 = pl.multiple_of(i * TM, TM)
        blk = xs_ref[pl.ds(r0, TM + 2 * pad), :]
        rows = r0 + lax.broadcasted_iota(jnp.int32, (TM, 1), 0)
        in_lat = rows >= cl
        pos = jnp.where(in_lat, rows - cl, rows)
        seqlen = jnp.where(in_lat, n, cl)
        xc = jnp.zeros((TM, width), F32) + cb
        for j in range(CONV_K):
            off = j - CONV_LEFT
            tap = blk[pad + off:pad + off + TM, :]
            ok = jnp.logical_and(pos + off >= 0, pos + off < seqlen)
            xc = xc + jnp.where(ok, tap, 0.0) * cw[j:j + 1, :]
        xcb = xc.astype(BF16)
        for d in range(2):
            r = _sigmoid(_dot(xcb, gw_ref[0, 2 * d]) + gb[2 * d:2 * d + 1, :])
            ig = _sigmoid(_dot(xcb, gw_ref[0, 2 * d + 1]) + gb[2 * d + 1:2 * d + 2, :])
            a = jnp.exp(-RG_C * r * softplus[d:d + 1, :])
            a_ref[d, pl.ds(r0, TM), :] = a
            u_ref[d, pl.ds(r0, TM), :] = jnp.sqrt(1.0 - a * a) * (ig * xc)
        return 0

    lax.fori_loop(0, t_all // TM, coeffs, 0)

    g_all, g_ctx = t_all // SUBLANES, cl // SUBLANES

    def step(i, carry):
        h_fwd, h_rev = carry
        g_rev = jnp.where(i < g_ctx, g_ctx - 1 - i, g_all - 1 - (i - g_ctx))
        sf = pl.ds(pl.multiple_of(i * SUBLANES, SUBLANES), SUBLANES)
        sr = pl.ds(pl.multiple_of(g_rev * SUBLANES, SUBLANES), SUBLANES)
        af, uf = _scan8(a_ref[0, sf, :], u_ref[0, sf, :], False)
        ar, ur = _scan8(a_ref[1, sr, :], u_ref[1, sr, :], True)
        hf = af * h_fwd + uf
        hr = ar * h_rev + ur
        u_ref[0, sf, :] = hf
        u_ref[1, sr, :] = hr
        return hf[SUBLANES - 1:SUBLANES, :], hr[0:1, :]

    zero = jnp.zeros((1, width), F32)
    lax.fori_loop(0, g_all, step, (zero, zero), unroll=4)

    def combine(i, _):
        r0 = pl.multiple_of(i * TM, TM)
        hsum = u_ref[0, pl.ds(r0, TM), :] + u_ref[1, pl.ds(r0, TM), :]
        o_ref[0, pl.ds(r0, TM), :] = (jax.nn.gelu(gl_ref[0, pl.ds(r0, TM), :]) * hsum).astype(BF16)
        return 0

    lax.fori_loop(0, t_all // TM, combine, 0)


def _rw_in_kernel(x_ref, xp_ref, xn_ref, mod_ref, g_ref, mu_ref, wrkv_ref, wdw_ref, wda_ref, wuw_ref, wua_ref,
                  lb_ref, gd_ref, gu_ref, kk_ref, ka_ref, rk_ref, gnb_ref, ones_ref,
                  r_out, kk_out, v_out, lw_out, bb_out, kd_out, g_out, z_out, *, ct, nt):
    t = pl.program_id(1)
    g = g_ref[...]
    shift, scale = mod_ref[0, 0:1, :], mod_ref[0, 1:2, :]
    h = _norm_mod(x_ref[0], g, shift, scale)
    first = jnp.logical_or(t == 0, t == ct)
    last = jnp.logical_or(t == ct - 1, t == nt - 1)
    hp = _norm_mod(xp_ref[0], g, shift, scale)[SUBLANES - 1:SUBLANES, :]
    hn = _norm_mod(xn_ref[0], g, shift, scale)[0:1, :]
    hp = jnp.where(first, 0.0, hp)
    hn = jnp.where(last, 0.0, hn)
    row = lax.broadcasted_iota(jnp.int32, (TM, 1), 0)
    h_prev = jnp.where(row == 0, hp, pltpu.roll(h, 1, 0))
    h_next = jnp.where(row == TM - 1, hn, pltpu.roll(h, TM - 1, 0))
    xx = 0.5 * (h_prev + h_next) - h
    lerp = lambda j: (h + xx * mu_ref[j:j + 1, :]).astype(BF16)

    r = _dot(lerp(0), wrkv_ref[0])
    k = _dot(lerp(2), wrkv_ref[1])
    v = _dot(lerp(3), wrkv_ref[2])
    gate = _dot(_sigmoid(_dot(lerp(5), gd_ref[...])).astype(BF16), gu_ref[...])
    tw = jnp.tanh(_dot(lerp(1), wdw_ref[...])).astype(BF16)
    ta = _dot(lerp(4), wda_ref[...]).astype(BF16)

    kk = k * kk_ref[...]
    nrm = jnp.sqrt(_group_reduce(kk * kk, ones_ref))
    kk = kk / jnp.maximum(nrm, 1e-12)
    r_out[0] = r
    kk_out[0] = kk
    v_out[0] = v
    kd_sum = jnp.zeros_like(k)
    for d in range(2):
        dec = lb_ref[2 * d:2 * d + 1, :] + _dot(tw, wuw_ref[d])
        a = _sigmoid(lb_ref[2 * d + 1:2 * d + 2, :] + _dot(ta, wua_ref[d]))
        kd = k * (1.0 + (a - 1.0) * ka_ref[...])
        lw_out[d, 0] = -DECAY_SCALE * _sigmoid(dec)
        bb_out[d, 0] = kk * a
        kd_out[d, 0] = kd
        kd_sum = kd_sum + kd
    bonus = _group_reduce(r * kd_sum * rk_ref[...], ones_ref)
    g_out[0] = gate
    z_out[0] = (gnb_ref[...] + bonus * v) * gate


def _wkv_kernel(r_ref, kk_ref, v_ref, lw_ref, bb_ref, kd_ref, y_ref, h_ref):
    c, lw_n = WKV_CHUNK, WKV_LANES
    n_heads = lw_n // HEAD
    s_rows = n_heads * c
    d = pl.program_id(0)
    i = pl.program_id(3)

    @pl.when(i == 0)
    def _():
        h_ref[...] = jnp.zeros_like(h_ref)

    rev = d == 1
    crow = lax.broadcasted_iota(jnp.int32, (c, lw_n), 0)
    trow = lax.broadcasted_iota(jnp.int32, (c, s_rows), 0)
    tcol = lax.broadcasted_iota(jnp.int32, (c, s_rows), 1) % c
    ahead = jnp.where(rev, tcol - trow, trow - tcol)
    strict = ahead > 0
    incl = ahead >= 0
    eye = (trow == tcol).astype(F32)
    same_block = lambda size: (trow // size) == (tcol // size)
    lrow = lax.broadcasted_iota(jnp.int32, (lw_n, lw_n), 0)
    lcol = lax.broadcasted_iota(jnp.int32, (lw_n, lw_n), 1)
    same_head = (lrow // HEAD) == (lcol // HEAD)
    diag = lrow == lcol
    own_dims = (lax.broadcasted_iota(jnp.int32, (s_rows, lw_n), 1) // HEAD
                == lax.broadcasted_iota(jnp.int32, (s_rows, lw_n), 0) // c)
    own_steps = (lax.broadcasted_iota(jnp.int32, (s_rows, s_rows), 1) // c
                 == lax.broadcasted_iota(jnp.int32, (s_rows, s_rows), 0) // c)
    n_chunks = TM // c

    def stack(x, own):
        return jnp.where(own, jnp.concatenate([x] * n_heads, axis=0), 0.0).astype(BF16)

    def local(grp, ci):
        cidx = jnp.where(rev, n_chunks - 1 - ci, ci)
        sl = pl.ds(pl.multiple_of(cidx * c, c), c)
        ls = slice(grp * lw_n, (grp + 1) * lw_n)
        r, kk, v = r_ref[0, sl, ls], kk_ref[0, sl, ls], v_ref[0, sl, ls]
        lw, bb, kd = lw_ref[0, 0, sl, ls], bb_ref[0, 0, sl, ls], kd_ref[0, 0, sl, ls]
        pre = lw
        shift = 1
        while shift < c:
            pre = pre + jnp.where(crow >= shift, pltpu.roll(pre, shift, 0), 0.0)
            shift *= 2
        tot = pre[c - 1:c, :]
        cum = jnp.where(rev, tot - pre + lw, pre)
        yield
        e_neg = jnp.exp(-cum)
        e_end = jnp.exp(tot - cum)
        kd_g, b_g = kd * e_end, bb * e_end
        kk_hat = kk * jnp.exp(cum - lw)
        r_hat = r * jnp.exp(cum)
        kk_s = stack(kk_hat, own_dims)
        v_s = stack(v, own_dims)
        lhs = jnp.concatenate([kk_hat, r_hat], axis=0)
        yield
        p1 = _bdot(lhs, stack(kd * e_neg, own_dims), NT_DIMS)
        yield
        p2 = _bdot(lhs, stack(bb * e_neg, own_dims), NT_DIMS)
        yield
        a_kd = jnp.where(strict, p1[:c], 0.0)
        b_kd = jnp.where(incl, p1[c:], 0.0)
        nmat = jnp.where(strict, p2[:c], 0.0)
        b_b = jnp.where(incl, p2[c:], 0.0)
        tinv = eye - jnp.where(same_block(2), nmat, 0.0)
        size = 2
        while size < c:
            n_off = jnp.where(jnp.logical_and(same_block(2 * size), jnp.logical_not(same_block(size))), nmat, 0.0)
            nt = _bdot(n_off, stack(tinv, own_steps))
            yield
            tinv = tinv - _bdot(tinv, stack(nt, own_steps))
            yield
            size *= 2
        av = _bdot(a_kd, v_s)
        yield
        tw = _bdot(tinv, jnp.concatenate([kk_s, stack(av, own_dims)], axis=1))
        yield
        kkp, u0 = tw[:, :lw_n], tw[:, lw_n:]
        bw = _bdot(b_b, jnp.concatenate([stack(kkp, own_dims), stack(u0, own_dims)], axis=1))
        yield
        rp = r_hat - bw[:, :lw_n]
        y0 = _bdot(b_kd, v_s) - bw[:, lw_n:]
        m_mat = jnp.where(diag, jnp.exp(tot), 0.0) - jnp.where(same_head, _bdot(b_g, kkp, TN_DIMS), 0.0)
        g_mat = jnp.where(same_head, _bdot(jnp.concatenate([kd_g, b_g], axis=0),
                                           jnp.concatenate([v, -u0], axis=0), TN_DIMS), 0.0)
        return grp, sl, ls, rp, y0, m_mat, g_mat

    chains = [local(grp, ci) for ci in range(n_chunks) for grp in range(h_ref.shape[0])]
    parts = [None] * len(chains)
    while any(p is None for p in parts):
        for idx, chain in enumerate(chains):
            if parts[idx] is None:
                try:
                    next(chain)
                except StopIteration as done:
                    parts[idx] = done.value
    hs = [h_ref[grp] for grp in range(h_ref.shape[0])]
    for grp, sl, ls, rp, y0, m_mat, g_mat in parts:
        y_ref[0, 0, sl, ls] = _bdot(rp, hs[grp]) + y0
        hs[grp] = _bdot(m_mat, hs[grp]) + g_mat
    for grp, h_new in enumerate(hs):
        h_ref[grp] = h_new


def _bdot(a, b, dims=(((1,), (0,)), ((), ()))):
    return lax.dot_general(a.astype(BF16), b.astype(BF16), dims, preferred_element_type=F32)


def _post_kernel(*refs, odd, ff_chunk):
    if odd:
        (x_ref, y0_ref, y1_ref, gate_ref, z_ref, gng_ref, ones_ref,
         wo_ref, mod_ref, g2_ref, w1_ref, w2_ref, o_ref) = refs
        y = y0_ref[0, 0] + y1_ref[0, 0]
        dlt = y - _group_reduce(y, ones_ref)
        var = _group_reduce(dlt * dlt, ones_ref)
        mix = (dlt * lax.rsqrt(var + GN_EPS) * gng_ref[...] * gate_ref[0] + z_ref[0]).astype(BF16)
    else:
        x_ref, att_ref, rec_ref, wo_ref, mod_ref, g2_ref, w1_ref, w2_ref, o_ref = refs
        mix = jnp.concatenate([att_ref[0], rec_ref[0]], axis=1)
    x1 = x_ref[0] + mod_ref[0, 2:3, :] * _dot(mix, wo_ref[...])
    h2 = _norm_mod(x1, g2_ref[...], mod_ref[0, 3:4, :], mod_ref[0, 4:5, :]).astype(BF16)
    acc = jnp.zeros_like(x1)
    for c in range(w1_ref.shape[1] // ff_chunk):
        a = _dot(h2, w1_ref[:, c * ff_chunk:(c + 1) * ff_chunk])
        a = jnp.square(jnp.maximum(a, 0.0)).astype(BF16)
        acc = acc + _dot(a, w2_ref[c * ff_chunk:(c + 1) * ff_chunk, :])
    o_ref[0] = x1 + mod_ref[0, 5:6, :] * acc


def _blockdiag_ones(width, value):
    idx = jnp.arange(width) // HEAD
    return jnp.where(idx[:, None] == idx[None, :], value, 0.0).astype(BF16)


def _rope_tables(cl, n, grid_w):
    rows = n // grid_w
    row = jnp.repeat(jnp.arange(rows, dtype=F32), grid_w)
    col = jnp.tile(jnp.arange(grid_w, dtype=F32), rows)
    half = HEAD // 2
    inv = ROPE_THETA ** (-jnp.arange(0, half, 2, dtype=F32) / half)
    ang = jnp.concatenate([row[:, None] * inv, col[:, None] * inv], axis=-1)
    cos = jnp.repeat(jnp.cos(ang), 2, axis=-1)
    sin = jnp.repeat(jnp.sin(ang), 2, axis=-1)
    even = (jnp.arange(HEAD) % 2 == 0)[None, :]
    sa = jnp.where(even, -sin, 0.0)
    sb = jnp.where(even, 0.0, sin)
    ctx = lambda fill: jnp.full((cl, HEAD), fill, F32)
    full = lambda lat, fill: jnp.tile(jnp.concatenate([ctx(fill), lat], axis=0), (1, LANES // HEAD))
    return full(cos, 1.0), full(sa, 0.0), full(sb, 0.0)


def _tile_specs(b_all, ct, d_model):
    mod_spec = pl.BlockSpec((1, 6, d_model), lambda b, t: (jnp.where(t < ct, b_all, b), 0, 0))
    row_spec = lambda w: pl.BlockSpec((1, TM, w), lambda b, t: (b, t, 0))
    return mod_spec, row_spec


def _hybrid_layer(xs, mods, g1, w_in, w_out, qn, kn, conv_w, conv_b, gate_w, gate_b, lam, ropes, dims):
    b_all, cl, n, d_model = dims
    t_all = cl + n
    ct, nt = cl // TM, t_all // TM
    in_w = w_in.shape[1]
    rnn_w = conv_w.shape[1]
    kv_w = (in_w - 2 * rnn_w - d_model // 2) // 2
    q_w = in_w - 2 * kv_w - 2 * rnn_w
    n_kv = kv_w // HEAD
    assert ct % 2 == 1 and nt % 2 == 1, "the attention kernel walks key tiles in pairs plus one"
    groups = q_w // kv_w
    mod_spec, row_spec = _tile_specs(b_all, ct, d_model)
    rope_spec = pl.BlockSpec((TM, LANES), lambda b, t: (t, 0))
    cos, sa, sb = ropes

    q, k, vt, xr, gl = pl.pallas_call(
        functools.partial(_hy_in_kernel, q_w=q_w, kv_w=kv_w, rnn_w=rnn_w),
        grid=(b_all, nt),
        in_specs=[row_spec(d_model), mod_spec, _const_spec((1, d_model)), _const_spec((d_model, in_w)),
                  rope_spec, rope_spec, rope_spec, _const_spec((1, q_w)), _const_spec((1, kv_w)),
                  _const_spec((MXU, MXU)), _const_spec((kv_w, kv_w))],
        out_specs=[row_spec(q_w), row_spec(kv_w),
                   pl.BlockSpec((1, n_kv, HEAD + V_ONES, TM), lambda b, t: (b, 0, 0, t)),
                   row_spec(rnn_w), row_spec(rnn_w)],
        out_shape=[jax.ShapeDtypeStruct((b_all, t_all, q_w), BF16),
                   jax.ShapeDtypeStruct((b_all, t_all, kv_w), BF16),
                   jax.ShapeDtypeStruct((b_all, n_kv, HEAD + V_ONES, t_all), BF16),
                   jax.ShapeDtypeStruct((b_all, t_all, rnn_w), F32),
                   jax.ShapeDtypeStruct((b_all, t_all, rnn_w), F32)],
        compiler_params=_params(("parallel", "parallel")),
    )(xs, mods, g1.reshape(1, d_model), w_in.astype(BF16), cos, sa, sb,
      jnp.tile(qn, q_w // HEAD).reshape(1, q_w), jnp.tile(kn, kv_w // HEAD).reshape(1, kv_w),
      _blockdiag_ones(MXU, 1.0 / HEAD), _blockdiag_ones(kv_w, 1.0 / HEAD))

    att = pl.pallas_call(
        functools.partial(_attn_kernel, ct=ct, nt=nt, groups=groups, n_kv=n_kv),
        grid=(b_all, nt),
        in_specs=[row_spec(q_w),
                  pl.BlockSpec((1, t_all, kv_w), lambda b, t: (b, 0, 0)),
                  pl.BlockSpec((1, n_kv, HEAD + V_ONES, t_all), lambda b, t: (b, 0, 0, 0))],
        out_specs=row_spec(q_w),
        out_shape=jax.ShapeDtypeStruct((b_all, t_all, q_w), BF16),
        scratch_shapes=[pltpu.VMEM((q_w, TM), F32)] + [pltpu.VMEM((q_w // HEAD, TM, TM), F32)] * 2,
        compiler_params=_params(("parallel", "parallel")),
    )(q, k, vt)

    n_lc = rnn_w // RG_LANES
    per = RG_LANES // HEAD
    gw = gate_w.reshape(4, n_lc, per, HEAD, HEAD)
    eye = jnp.eye(per, dtype=F32)
    gw = jnp.einsum('gcpde,pq->cgpdqe', gw, eye).reshape(n_lc, 4, RG_LANES, RG_LANES).astype(BF16)
    lane_spec = lambda rows: pl.BlockSpec((rows, RG_LANES), lambda b, c: (0, c))
    seq_spec = pl.BlockSpec((1, t_all, RG_LANES), lambda b, c: (b, 0, c))
    rec = pl.pallas_call(
        functools.partial(_rglru_kernel, cl=cl, n=n),
        grid=(b_all, n_lc),
        in_specs=[seq_spec, seq_spec, lane_spec(CONV_K), lane_spec(1),
                  pl.BlockSpec((1, 4, RG_LANES, RG_LANES), lambda b, c: (c, 0, 0, 0)),
                  lane_spec(4), lane_spec(2)],
        out_specs=seq_spec,
        out_shape=jax.ShapeDtypeStruct((b_all, t_all, rnn_w), BF16),
        scratch_shapes=[pltpu.VMEM((t_all + 2 * SUBLANES, RG_LANES), F32),
                        pltpu.VMEM((2, t_all, RG_LANES), F32),
                        pltpu.VMEM((2, t_all, RG_LANES), F32)],
        compiler_params=_params(("parallel", "parallel")),
    )(xr, gl, conv_w, conv_b.reshape(1, rnn_w), gw, gate_b.reshape(4, rnn_w), lam)
    return (att, rec), w_out.astype(BF16)


def _rwkv_layer(xs, mods, g1, mu, w_rkv, lora_down, lora_up, lora_bias, gate_down, gate_up,
                k_k, k_a, r_k, gn_b, dims):
    b_all, cl, n, d_model = dims
    t_all = cl + n
    ct, nt = cl // TM, t_all // TM
    n8 = t_all // SUBLANES
    per8 = TM // SUBLANES
    lora = lora_down.shape[-1]
    glora = gate_down.shape[-1]
    mod_spec, row_spec = _tile_specs(b_all, ct, d_model)
    dir_spec = pl.BlockSpec((2, 1, TM, d_model), lambda b, t: (0, b, t, 0))
    wdw = jnp.concatenate([lora_down[0, 0], lora_down[1, 0]], axis=1).astype(BF16)
    wda = jnp.concatenate([lora_down[0, 1], lora_down[1, 1]], axis=1).astype(BF16)
    zeros = jnp.zeros((lora, d_model), F32)
    pad_up = lambda j: jnp.stack([jnp.concatenate([lora_up[0, j], zeros], axis=0),
                                  jnp.concatenate([zeros, lora_up[1, j]], axis=0)]).astype(BF16)
    vec = lambda a: a.reshape(1, d_model)
    outs = pl.pallas_call(
        functools.partial(_rw_in_kernel, ct=ct, nt=nt),
        grid=(b_all, nt),
        in_specs=[row_spec(d_model),
                  pl.BlockSpec((1, SUBLANES, d_model), lambda b, t: (b, jnp.maximum(t * per8 - 1, 0), 0)),
                  pl.BlockSpec((1, SUBLANES, d_model), lambda b, t: (b, jnp.minimum((t + 1) * per8, n8 - 1), 0)),
                  mod_spec, _const_spec((1, d_model)), _const_spec((6, d_model)),
                  _const_spec((3, d_model, d_model)), _const_spec((d_model, 2 * lora)),
                  _const_spec((d_model, 2 * lora)), _const_spec((2, 2 * lora, d_model)),
                  _const_spec((2, 2 * lora, d_model)), _const_spec((4, d_model)),
                  _const_spec((d_model, glora)), _const_spec((glora, d_model)),
                  _const_spec((1, d_model)), _const_spec((1, d_model)), _const_spec((1, d_model)),
                  _const_spec((1, d_model)), _const_spec((MXU, MXU))],
        out_specs=[row_spec(d_model), row_spec(d_model), row_spec(d_model),
                   dir_spec, dir_spec, dir_spec, row_spec(d_model), row_spec(d_model)],
        out_shape=[jax.ShapeDtypeStruct((b_all, t_all, d_model), F32)] * 3
        + [jax.ShapeDtypeStruct((2, b_all, t_all, d_model), F32)] * 3
        + [jax.ShapeDtypeStruct((b_all, t_all, d_model), F32)] * 2,
        compiler_params=_params(("parallel", "parallel")),
    )(xs, xs, xs, mods, vec(g1), mu, w_rkv.astype(BF16), wdw, wda, pad_up(0), pad_up(1),
      lora_bias.reshape(4, d_model), gate_down.astype(BF16), gate_up.astype(BF16),
      vec(k_k), vec(k_a), vec(r_k), vec(gn_b), _blockdiag_ones(MXU, 1.0))
    r, kk, v, lw, bb, kd, gate, z = outs

    def tmap(dd, i):
        rev = jnp.where(i < ct, ct - 1 - i, nt - 1 - (i - ct))
        return jnp.where(dd == 0, i, rev)

    blk_lanes = WKV_LANES * WKV_GROUPS
    shared = pl.BlockSpec((1, TM, blk_lanes), lambda dd, b, hh, i: (b, tmap(dd, i), hh))
    per_dir = pl.BlockSpec((1, 1, TM, blk_lanes), lambda dd, b, hh, i: (dd, b, tmap(dd, i), hh))
    y = pl.pallas_call(
        _wkv_kernel,
        grid=(2, b_all, d_model // blk_lanes, nt),
        in_specs=[shared, shared, shared, per_dir, per_dir, per_dir],
        out_specs=per_dir,
        out_shape=jax.ShapeDtypeStruct((2, b_all, t_all, d_model), F32),
        scratch_shapes=[pltpu.VMEM((WKV_GROUPS, WKV_LANES, WKV_LANES), F32)],
        compiler_params=_params(("arbitrary", "arbitrary", "arbitrary", "arbitrary")),
    )(r, kk, v, lw, bb, kd)
    return y, gate, z


def _post(xs, mix_inputs, w_o, mods, g2, w1, w2, dims, odd, gn_g=None, latent_only=False):
    b_all, cl, n, d_model = dims
    t_all = cl + n
    ct, nt = cl // TM, t_all // TM
    t0 = ct if latent_only else 0
    d_ff = w1.shape[1]
    mod_spec = pl.BlockSpec((1, 6, d_model), lambda b, t: (jnp.where(t + t0 < ct, b_all, b), 0, 0))
    row_spec = lambda w: pl.BlockSpec((1, TM, w), lambda b, t: (b, t + t0, 0))
    tail_specs = [_const_spec((d_model, d_model)), mod_spec, _const_spec((1, d_model)),
                  _const_spec((d_model, d_ff)), _const_spec((d_ff, d_model))]
    tail_args = (w_o, mods, g2.reshape(1, d_model), w1.astype(BF16), w2.astype(BF16))
    if odd:
        y, gate, z = mix_inputs
        y_spec = lambda dd: pl.BlockSpec((1, 1, TM, d_model), lambda b, t: (dd, b, t + t0, 0))
        in_specs = [row_spec(d_model), y_spec(0), y_spec(1), row_spec(d_model), row_spec(d_model),
                    _const_spec((1, d_model)), _const_spec((MXU, MXU))] + tail_specs
        args = (xs, y, y, gate, z, gn_g.reshape(1, d_model), _blockdiag_ones(MXU, 1.0 / HEAD)) + tail_args
    else:
        att, rec = mix_inputs
        in_specs = [row_spec(d_model), row_spec(att.shape[-1]), row_spec(rec.shape[-1])] + tail_specs
        args = (xs, att, rec) + tail_args
    rows_out = n if latent_only else t_all
    return pl.pallas_call(
        functools.partial(_post_kernel, odd=odd, ff_chunk=min(d_ff, 4 * MXU)),
        grid=(b_all, nt - t0),
        in_specs=in_specs,
        out_specs=pl.BlockSpec((1, TM, d_model), lambda b, t: (b, t, 0)),
        out_shape=jax.ShapeDtypeStruct((b_all, rows_out, d_model), F32),
        compiler_params=_params(("parallel", "parallel")),
    )(*args)


def kernel(x, c, ctx, c_ctx, ada_w, ada_b, norm_g, mlp_w1, mlp_w2, hy_w_in, hy_w_out, hy_q_norm, hy_k_norm, hy_conv_w, hy_conv_b, hy_gate_w, hy_gate_b, hy_lam, rw_mu, rw_w_rkv, rw_w_o, rw_lora_down, rw_lora_up, rw_lora_bias, rw_gate_down, rw_gate_up, rw_k_k, rw_k_a, rw_r_k, rw_gn_g, rw_gn_b):
    b_all, n, d_model = x.shape
    cl = ctx.shape[1]
    depth = ada_w.shape[0]
    assert cl % TM == 0 and n % TM == 0 and d_model % MXU == 0
    dims = (b_all, cl, n, d_model)
    mods_all = _ada_mods(jnp.concatenate([c, c_ctx[None, :]], axis=0), ada_w, ada_b)
    ropes = _rope_tables(cl, n, GRID_W)
    xs = jnp.concatenate([ctx, x], axis=1)
    for l in range(depth):
        i = l // 2
        mods = mods_all[l]
        last = l == depth - 1
        if l % 2 == 0:
            mix, w_o = _hybrid_layer(xs, mods, norm_g[l, 0], hy_w_in[i], hy_w_out[i], hy_q_norm[i], hy_k_norm[i],
                                     hy_conv_w[i], hy_conv_b[i], hy_gate_w[i], hy_gate_b[i], hy_lam[i], ropes, dims)
            xs = _post(xs, mix, w_o, mods, norm_g[l, 1], mlp_w1[l], mlp_w2[l], dims, odd=False, latent_only=last)
        else:
            mix = _rwkv_layer(xs, mods, norm_g[l, 0], rw_mu[i], rw_w_rkv[i], rw_lora_down[i], rw_lora_up[i],
                              rw_lora_bias[i], rw_gate_down[i], rw_gate_up[i], rw_k_k[i], rw_k_a[i],
                              rw_r_k[i], rw_gn_b[i], dims)
            xs = _post(xs, mix, rw_w_o[i].astype(BF16), mods, norm_g[l, 1], mlp_w1[l], mlp_w2[l], dims,
                       odd=True, gn_g=rw_gn_g[i], latent_only=last)
    return xs if xs.shape[1] == n else xs[:, cl:]
```

```python
import functools
import math

import jax
import jax.numpy as jnp
from jax import lax
from jax.experimental import pallas as pl
from jax.experimental.pallas import tpu as pltpu

F32 = jnp.float32
BF16 = jnp.bfloat16

HEAD = 64
LANES = 128
SUBLANES = 8
MXU = 256
TM = 256
VMEM_LIMIT = 56 * 1024 * 1024

EPS = 1e-6
GN_EPS = 64e-5
RG_C = 8.0
ROPE_THETA = 10000.0
GRID_W = 64
DECAY_SCALE = math.exp(-0.5)
CONV_K = 4
CONV_LEFT = 2
Q_SCALE = HEAD ** -0.5 * math.log2(math.e)
V_ONES = 16
RG_LANES = 256
WKV_CHUNK = 64
WKV_LANES = 256
WKV_GROUPS = 4

NT_DIMS = (((1,), (1,)), ((), ()))
TN_DIMS = (((0,), (0,)), ((), ()))


def _params(sem):
    return pltpu.CompilerParams(dimension_semantics=sem, vmem_limit_bytes=VMEM_LIMIT)


def _const_spec(shape):
    nd = len(shape)
    return pl.BlockSpec(shape, lambda *_: (0,) * nd, pipeline_mode=pl.Buffered(1))


def _dot(a, b):
    return jnp.dot(a, b, preferred_element_type=F32)


def _dot_exact(a, b):
    return jnp.dot(a, b, preferred_element_type=F32, precision=lax.Precision.HIGHEST)


def _split(x):
    hi = x.astype(BF16)
    lo = (x - hi.astype(F32)).astype(BF16)
    return hi, lo


def _group_reduce(x, ones_ref):
    cw = ones_ref.shape[0]
    ones = ones_ref[...]
    outs = []
    for c in range(x.shape[1] // cw):
        hi, lo = _split(x[:, c * cw:(c + 1) * cw])
        outs.append(_dot(hi, ones) + _dot(lo, ones))
    return outs[0] if len(outs) == 1 else jnp.concatenate(outs, axis=1)


def _sigmoid(x):
    return 0.5 * jnp.tanh(0.5 * x) + 0.5


def _norm_mod(x, g, shift, scale):
    ms = jnp.mean(x * x, axis=-1, keepdims=True)
    return x * lax.rsqrt(ms + EPS) * (g * (1.0 + scale)) + shift


def _ada_kernel(c_ref, w_ref, b_ref, o_ref):
    c = c_ref[...]
    s = c * _sigmoid(c)
    o_ref[0, 0] = _dot_exact(s, w_ref[0]) + b_ref[0]


def _ada_mods(cc, ada_w, ada_b):
    depth, d, _ = ada_w.shape
    rows = cc.shape[0]
    out = pl.pallas_call(
        _ada_kernel,
        grid=(depth, 6),
        in_specs=[pl.BlockSpec((rows, d), lambda l, j: (0, 0)),
                  pl.BlockSpec((1, d, d), lambda l, j: (l, 0, j)),
                  pl.BlockSpec((1, 1, d), lambda l, j: (l * 6 + j, 0, 0))],
        out_specs=pl.BlockSpec((1, 1, rows, d), lambda l, j: (l, j, 0, 0)),
        out_shape=jax.ShapeDtypeStruct((depth, 6, rows, d), F32),
        compiler_params=_params(("arbitrary", "arbitrary")),
    )(cc, ada_w, ada_b.reshape(depth * 6, 1, d))
    return jnp.transpose(out, (0, 2, 1, 3))


def _hy_in_kernel(x_ref, mod_ref, g_ref, w_ref, cos_ref, sa_ref, sb_ref, qg_ref, kg_ref, oq_ref, ok_ref,
                  q_out, k_out, vt_out, xr_out, gl_out, *, q_w, kv_w, rnn_w):
    x = x_ref[0]
    h = _norm_mod(x, g_ref[...], mod_ref[0, 0:1, :], mod_ref[0, 1:2, :]).astype(BF16)
    z = _dot(h, w_ref[...])
    c0, c1, c2, c3 = q_w, q_w + kv_w, q_w + 2 * kv_w, q_w + 2 * kv_w + rnn_w

    def norm_rope(u, gain, ones_ref):
        width = u.shape[1]
        un = u * lax.rsqrt(_group_reduce(u * u, ones_ref) + EPS) * gain
        reps = width // LANES
        tile = lambda r: jnp.concatenate([r[...]] * reps, axis=1) if reps > 1 else r[...]
        nxt = pltpu.roll(un, width - 1, 1)
        prv = pltpu.roll(un, 1, 1)
        return un * tile(cos_ref) + nxt * tile(sa_ref) + prv * tile(sb_ref)

    q_out[0] = (norm_rope(z[:, :c0], qg_ref[...], oq_ref) * Q_SCALE).astype(BF16)
    k_out[0] = norm_rope(z[:, c0:c1], kg_ref[...], ok_ref).astype(BF16)
    vt = z[:, c1:c2].T
    ones = jnp.ones((V_ONES, TM), F32)
    for hk in range(kv_w // HEAD):
        vt_out[0, hk] = jnp.concatenate([vt[hk * HEAD:(hk + 1) * HEAD], ones], axis=0).astype(BF16)
    xr_out[0] = z[:, c2:c3]
    gl_out[0] = z[:, c3:]


def _attn_kernel(q_ref, k_ref, vt_ref, o_ref, ot_ref, sa_ref, sb_ref, *, ct, nt, groups, n_kv):
    t = pl.program_id(1)
    nkv = jnp.where(t < ct, ct, nt)
    q = q_ref[0].astype(F32)
    lane = lax.broadcasted_iota(jnp.int32, (TM, LANES), 1)
    low = lane < HEAD
    blocks = []
    for head in range(groups * n_kv):
        hk = head // groups
        qc = q[:, LANES * (head // 2):LANES * (head // 2 + 1)]
        if head % 2 != hk % 2:
            qc = pltpu.roll(qc, HEAD, 1)
        blocks.append((hk, jnp.where(low if hk % 2 == 0 else jnp.logical_not(low), qc, 0.0).astype(BF16)))

    def scores(i, dst_ref, heads):
        r0 = pl.multiple_of(i * TM, TM)
        for head in heads:
            hk, qm = blocks[head]
            kc = k_ref[0, pl.ds(r0, TM), LANES * (hk // 2):LANES * (hk // 2 + 1)]
            dst_ref[head] = lax.dot_general(kc, qm, NT_DIMS, preferred_element_type=F32)

    def absorb(i, src_ref, stats, nxt=None):
        r0 = pl.multiple_of(i * TM, TM)
        out = []
        for head, ((hk, _), (m, acc)) in enumerate(zip(blocks, stats)):
            if nxt is not None:
                scores(nxt[0], nxt[1], [head])
            st = src_ref[head]
            m_new = jnp.maximum(m, jnp.max(st, axis=0, keepdims=True))
            out.append((m_new, _dot(vt_ref[0, hk, :, pl.ds(r0, TM)], jnp.exp2(st - m_new).astype(BF16))))
        return tuple((m_new, jnp.exp2(m - m_new) * acc + pv) for (m_new, pv), (m, acc) in zip(out, stats))

    def pair(j, stats):
        stats = absorb(2 * j, sa_ref, stats, (2 * j + 1, sb_ref))
        return absorb(2 * j + 1, sb_ref, stats, (2 * j + 2, sa_ref))

    scores(0, sa_ref, range(len(blocks)))
    init = tuple((jnp.full((1, TM), -jnp.inf, F32), jnp.zeros((vt_ref.shape[2], TM), F32)) for _ in blocks)
    res = absorb(nkv - 1, sa_ref, lax.fori_loop(0, (nkv - 1) // 2, pair, init))
    for head, (_, acc) in enumerate(res):
        ot_ref[HEAD * head:HEAD * (head + 1), :] = acc[:HEAD] / acc[HEAD:HEAD + 1]
    o_ref[0] = ot_ref[...].T.astype(BF16)


def _scan8(a, u, reverse):
    row = lax.broadcasted_iota(jnp.int32, a.shape, 0)
    for d in (1, 2, 4):
        if reverse:
            a_s, u_s, ok = pltpu.roll(a, SUBLANES - d, 0), pltpu.roll(u, SUBLANES - d, 0), row < SUBLANES - d
        else:
            a_s, u_s, ok = pltpu.roll(a, d, 0), pltpu.roll(u, d, 0), row >= d
        u = a * jnp.where(ok, u_s, 0.0) + u
        a = a * jnp.where(ok, a_s, 1.0)
    return a, u


def _rglru_kernel(xr_ref, gl_ref, cw_ref, cb_ref, gw_ref, gb_ref, lam_ref, o_ref,
                  xs_ref, a_ref, u_ref, *, cl, n):
    t_all = cl + n
    pad = SUBLANES
    width = xs_ref.shape[1]
    xs_ref[0:pad, :] = jnp.zeros((pad, width), F32)
    xs_ref[pad + t_all:, :] = jnp.zeros((pad, width), F32)
    xs_ref[pad:pad + t_all, :] = xr_ref[0]
    lam = lam_ref[...]
    z = -lam
    softplus = jnp.maximum(z, 0.0) + jnp.log(1.0 + jnp.exp(-jnp.abs(z)))
    cw = cw_ref[...]
    cb = cb_ref[...]
    gb = gb_ref[...]

    def coeffs(i, _):
        r0 = pl.multiple_of(i * TM, TM)
        blk = xs_ref[pl.ds(r0, TM + 2 * pad), :]
        rows = r0 + lax.broadcasted_iota(jnp.int32, (TM, 1), 0)
        in_lat = rows >= cl
        pos = jnp.where(in_lat, rows - cl, rows)
        seqlen = jnp.where(in_lat, n, cl)
        xc = jnp.zeros((TM, width), F32) + cb
        for j in range(CONV_K):
            off = j - CONV_LEFT
            tap = blk[pad + off:pad + off + TM, :]
            ok = jnp.logical_and(pos + off >= 0, pos + off < seqlen)
            xc = xc + jnp.where(ok, tap, 0.0) * cw[j:j + 1, :]
        xcb = xc.astype(BF16)
        for d in range(2):
            r = _sigmoid(_dot(xcb, gw_ref[0, 2 * d]) + gb[2 * d:2 * d + 1, :])
            ig = _sigmoid(_dot(xcb, gw_ref[0, 2 * d + 1]) + gb[2 * d + 1:2 * d + 2, :])
            a = jnp.exp(-RG_C * r * softplus[d:d + 1, :])
            a_ref[d, pl.ds(r0, TM), :] = a
            u_ref[d, pl.ds(r0, TM), :] = jnp.sqrt(1.0 - a * a) * (ig * xc)
        return 0

    lax.fori_loop(0, t_all // TM, coeffs, 0)

    g_all, g_ctx = t_all // SUBLANES, cl // SUBLANES

    def step(i, carry):
        h_fwd, h_rev = carry
        g_rev = jnp.where(i < g_ctx, g_ctx - 1 - i, g_all - 1 - (i - g_ctx))
        sf = pl.ds(pl.multiple_of(i * SUBLANES, SUBLANES), SUBLANES)
        sr = pl.ds(pl.multiple_of(g_rev * SUBLANES, SUBLANES), SUBLANES)
        af, uf = _scan8(a_ref[0, sf, :], u_ref[0, sf, :], False)
        ar, ur = _scan8(a_ref[1, sr, :], u_ref[1, sr, :], True)
        hf = af * h_fwd + uf
        hr = ar * h_rev + ur
        u_ref[0, sf, :] = hf
        u_ref[1, sr, :] = hr
        return hf[SUBLANES - 1:SUBLANES, :], hr[0:1, :]

    zero = jnp.zeros((1, width), F32)
    lax.fori_loop(0, g_all, step, (zero, zero), unroll=4)

    def combine(i, _):
        r0 = pl.multiple_of(i * TM, TM)
        hsum = u_ref[0, pl.ds(r0, TM), :] + u_ref[1, pl.ds(r0, TM), :]
        o_ref[0, pl.ds(r0, TM), :] = (jax.nn.gelu(gl_ref[0, pl.ds(r0, TM), :]) * hsum).astype(BF16)
        return 0

    lax.fori_loop(0, t_all // TM, combine, 0)


def _rw_in_kernel(x_ref, xp_ref, xn_ref, mod_ref, g_ref, mu_ref, wrkv_ref, wdw_ref, wda_ref, wuw_ref, wua_ref,
                  lb_ref, gd_ref, gu_ref, kk_ref, ka_ref, rk_ref, gnb_ref, ones_ref,
                  r_out, kk_out, v_out, lw_out, bb_out, kd_out, g_out, z_out, *, ct, nt):
    t = pl.program_id(1)
    g = g_ref[...]
    shift, scale = mod_ref[0, 0:1, :], mod_ref[0, 1:2, :]
    h = _norm_mod(x_ref[0], g, shift, scale)
    first = jnp.logical_or(t == 0, t == ct)
    last = jnp.logical_or(t == ct - 1, t == nt - 1)
    hp = _norm_mod(xp_ref[0], g, shift, scale)[SUBLANES - 1:SUBLANES, :]
    hn = _norm_mod(xn_ref[0], g, shift, scale)[0:1, :]
    hp = jnp.where(first, 0.0, hp)
    hn = jnp.where(last, 0.0, hn)
    row = lax.broadcasted_iota(jnp.int32, (TM, 1), 0)
    h_prev = jnp.where(row == 0, hp, pltpu.roll(h, 1, 0))
    h_next = jnp.where(row == TM - 1, hn, pltpu.roll(h, TM - 1, 0))
    xx = 0.5 * (h_prev + h_next) - h
    lerp = lambda j: (h + xx * mu_ref[j:j + 1, :]).astype(BF16)

    r = _dot(lerp(0), wrkv_ref[0])
    k = _dot(lerp(2), wrkv_ref[1])
    v = _dot(lerp(3), wrkv_ref[2])
    gate = _dot(_sigmoid(_dot(lerp(5), gd_ref[...])).astype(BF16), gu_ref[...])
    tw = jnp.tanh(_dot(lerp(1), wdw_ref[...])).astype(BF16)
    ta = _dot(lerp(4), wda_ref[...]).astype(BF16)

    kk = k * kk_ref[...]
    nrm = jnp.sqrt(_group_reduce(kk * kk, ones_ref))
    kk = kk / jnp.maximum(nrm, 1e-12)
    r_out[0] = r
    kk_out[0] = kk
    v_out[0] = v
    kd_sum = jnp.zeros_like(k)
    for d in range(2):
        dec = lb_ref[2 * d:2 * d + 1, :] + _dot(tw, wuw_ref[d])
        a = _sigmoid(lb_ref[2 * d + 1:2 * d + 2, :] + _dot(ta, wua_ref[d]))
        kd = k * (1.0 + (a - 1.0) * ka_ref[...])
        lw_out[d, 0] = -DECAY_SCALE * _sigmoid(dec)
        bb_out[d, 0] = kk * a
        kd_out[d, 0] = kd
        kd_sum = kd_sum + kd
    bonus = _group_reduce(r * kd_sum * rk_ref[...], ones_ref)
    g_out[0] = gate
    z_out[0] = (gnb_ref[...] + bonus * v) * gate


def _wkv_kernel(r_ref, kk_ref, v_ref, lw_ref, bb_ref, kd_ref, y_ref, h_ref):
    c, lw_n = WKV_CHUNK, WKV_LANES
    n_heads = lw_n // HEAD
    s_rows = n_heads * c
    d = pl.program_id(0)
    i = pl.program_id(3)

    @pl.when(i == 0)
    def _():
        h_ref[...] = jnp.zeros_like(h_ref)

    rev = d == 1
    crow = lax.broadcasted_iota(jnp.int32, (c, lw_n), 0)
    trow = lax.broadcasted_iota(jnp.int32, (c, s_rows), 0)
    tcol = lax.broadcasted_iota(jnp.int32, (c, s_rows), 1) % c
    ahead = jnp.where(rev, tcol - trow, trow - tcol)
    strict = ahead > 0
    incl = ahead >= 0
    eye = (trow == tcol).astype(F32)
    same_block = lambda size: (trow // size) == (tcol // size)
    lrow = lax.broadcasted_iota(jnp.int32, (lw_n, lw_n), 0)
    lcol = lax.broadcasted_iota(jnp.int32, (lw_n, lw_n), 1)
    same_head = (lrow // HEAD) == (lcol // HEAD)
    diag = lrow == lcol
    own_dims = (lax.broadcasted_iota(jnp.int32, (s_rows, lw_n), 1) // HEAD
                == lax.broadcasted_iota(jnp.int32, (s_rows, lw_n), 0) // c)
    own_steps = (lax.broadcasted_iota(jnp.int32, (s_rows, s_rows), 1) // c
                 == lax.broadcasted_iota(jnp.int32, (s_rows, s_rows), 0) // c)
    n_chunks = TM // c

    def stack(x, own):
        return jnp.where(own, jnp.concatenate([x] * n_heads, axis=0), 0.0).astype(BF16)

    def rows_of(m):
        out = m[0:c]
        for hh in range(1, n_heads):
            out = out + m[hh * c:(hh + 1) * c]
        return out

    def local(grp, ci):
        cidx = jnp.where(rev, n_chunks - 1 - ci, ci)
        sl = pl.ds(pl.multiple_of(cidx * c, c), c)
        ls = slice(grp * lw_n, (grp + 1) * lw_n)
        r, kk, v = r_ref[0, sl, ls], kk_ref[0, sl, ls], v_ref[0, sl, ls]
        lw, bb, kd = lw_ref[0, 0, sl, ls], bb_ref[0, 0, sl, ls], kd_ref[0, 0, sl, ls]
        pre = lw
        shift = 1
        while shift < c:
            pre = pre + jnp.where(crow >= shift, pltpu.roll(pre, shift, 0), 0.0)
            shift *= 2
        tot = pre[c - 1:c, :]
        cum = jnp.where(rev, tot - pre + lw, pre)
        yield
        e_neg = jnp.exp(-cum)
        e_end = jnp.exp(tot - cum)
        kd_g, b_g = kd * e_end, bb * e_end
        kk_hat = kk * jnp.exp(cum - lw)
        r_hat = r * jnp.exp(cum)
        kk_s = stack(kk_hat, own_dims)
        v_s = stack(v, own_dims)
        lhs = jnp.concatenate([kk_hat, r_hat], axis=0)
        yield
        p1 = _bdot(lhs, stack(kd * e_neg, own_dims), NT_DIMS)
        yield
        p2 = _bdot(lhs, stack(bb * e_neg, own_dims), NT_DIMS)
        yield
        a_kd = jnp.where(strict, p1[:c], 0.0)
        b_kd = jnp.where(incl, p1[c:], 0.0)
        nmat = jnp.where(strict, p2[:c], 0.0)
        b_b = jnp.where(incl, p2[c:], 0.0)
        tinv = eye - jnp.where(same_block(2), nmat, 0.0)
        size = 2
        while size < c:
            n_off = jnp.where(jnp.logical_and(same_block(2 * size), jnp.logical_not(same_block(size))), nmat, 0.0)
            nt = _bdot(n_off, stack(tinv, own_steps))
            yield
            tinv = tinv - _bdot(tinv, stack(nt, own_steps))
            yield
            size *= 2
        av = _bdot(a_kd, v_s)
        yield
        tw = _bdot(tinv, jnp.concatenate([kk_s, stack(av, own_dims)], axis=1))
        yield
        kkp, u0 = tw[:, :lw_n], tw[:, lw_n:]
        bw = _bdot(b_b, jnp.concatenate([stack(kkp, own_dims), stack(u0, own_dims)], axis=1))
        yield
        rp = r_hat - bw[:, :lw_n]
        y0 = _bdot(b_kd, v_s) - bw[:, lw_n:]
        m_mat = jnp.where(diag, jnp.exp(tot), 0.0) - jnp.where(same_head, _bdot(b_g, kkp, TN_DIMS), 0.0)
        g_mat = jnp.where(same_head, _bdot(jnp.concatenate([kd_g, b_g], axis=0),
                                           jnp.concatenate([v, -u0], axis=0), TN_DIMS), 0.0)
        return grp, sl, ls, rp, y0, rows_of(m_mat), rows_of(g_mat)

    chains = [local(grp, ci) for ci in range(n_chunks) for grp in range(h_ref.shape[0])]
    parts = [None] * len(chains)
    while any(p is None for p in parts):
        for idx, chain in enumerate(chains):
            if parts[idx] is None:
                try:
                    next(chain)
                except StopIteration as done:
                    parts[idx] = done.value
    hs = [h_ref[grp] for grp in range(h_ref.shape[0])]
    for grp, sl, ls, rp, y0, m_mat, g_mat in parts:
        y_ref[0, 0, sl, ls] = _bdot(rp, hs[grp]) + y0
        h_rows = _bdot(m_mat, hs[grp]) + g_mat
        hs[grp] = jnp.where(same_head, jnp.concatenate([h_rows] * n_heads, axis=0), 0.0)
    for grp, h_new in enumerate(hs):
        h_ref[grp] = h_new


def _bdot(a, b, dims=(((1,), (0,)), ((), ()))):
    return lax.dot_general(a.astype(BF16), b.astype(BF16), dims, preferred_element_type=F32)


def _post_kernel(*refs, odd, ff_chunk):
    if odd:
        (x_ref, y0_ref, y1_ref, gate_ref, z_ref, gng_ref, ones_ref,
         wo_ref, mod_ref, g2_ref, w1_ref, w2_ref, o_ref) = refs
        y = y0_ref[0, 0] + y1_ref[0, 0]
        dlt = y - _group_reduce(y, ones_ref)
        var = _group_reduce(dlt * dlt, ones_ref)
        mix = (dlt * lax.rsqrt(var + GN_EPS) * gng_ref[...] * gate_ref[0] + z_ref[0]).astype(BF16)
    else:
        x_ref, att_ref, rec_ref, wo_ref, mod_ref, g2_ref, w1_ref, w2_ref, o_ref = refs
        mix = jnp.concatenate([att_ref[0], rec_ref[0]], axis=1)
    x1 = x_ref[0] + mod_ref[0, 2:3, :] * _dot(mix, wo_ref[...])
    h2 = _norm_mod(x1, g2_ref[...], mod_ref[0, 3:4, :], mod_ref[0, 4:5, :]).astype(BF16)
    acc = jnp.zeros_like(x1)
    for c in range(w1_ref.shape[1] // ff_chunk):
        a = _dot(h2, w1_ref[:, c * ff_chunk:(c + 1) * ff_chunk])
        a = jnp.square(jnp.maximum(a, 0.0)).astype(BF16)
        acc = acc + _dot(a, w2_ref[c * ff_chunk:(c + 1) * ff_chunk, :])
    o_ref[0] = x1 + mod_ref[0, 5:6, :] * acc


def _blockdiag_ones(width, value):
    idx = jnp.arange(width) // HEAD
    return jnp.where(idx[:, None] == idx[None, :], value, 0.0).astype(BF16)


def _rope_tables(cl, n, grid_w):
    rows = n // grid_w
    row = jnp.repeat(jnp.arange(rows, dtype=F32), grid_w)
    col = jnp.tile(jnp.arange(grid_w, dtype=F32), rows)
    half = HEAD // 2
    inv = ROPE_THETA ** (-jnp.arange(0, half, 2, dtype=F32) / half)
    ang = jnp.concatenate([row[:, None] * inv, col[:, None] * inv], axis=-1)
    cos = jnp.repeat(jnp.cos(ang), 2, axis=-1)
    sin = jnp.repeat(jnp.sin(ang), 2, axis=-1)
    even = (jnp.arange(HEAD) % 2 == 0)[None, :]
    sa = jnp.where(even, -sin, 0.0)
    sb = jnp.where(even, 0.0, sin)
    ctx = lambda fill: jnp.full((cl, HEAD), fill, F32)
    full = lambda lat, fill: jnp.tile(jnp.concatenate([ctx(fill), lat], axis=0), (1, LANES // HEAD))
    return full(cos, 1.0), full(sa, 0.0), full(sb, 0.0)


def _tile_specs(b_all, ct, d_model):
    mod_spec = pl.BlockSpec((1, 6, d_model), lambda b, t: (jnp.where(t < ct, b_all, b), 0, 0))
    row_spec = lambda w: pl.BlockSpec((1, TM, w), lambda b, t: (b, t, 0))
    return mod_spec, row_spec


def _hybrid_layer(xs, mods, g1, w_in, w_out, qn, kn, conv_w, conv_b, gate_w, gate_b, lam, ropes, dims):
    b_all, cl, n, d_model = dims
    t_all = cl + n
    ct, nt = cl // TM, t_all // TM
    in_w = w_in.shape[1]
    rnn_w = conv_w.shape[1]
    kv_w = (in_w - 2 * rnn_w - d_model // 2) // 2
    q_w = in_w - 2 * kv_w - 2 * rnn_w
    n_kv = kv_w // HEAD
    assert ct % 2 == 1 and nt % 2 == 1, "the attention kernel walks key tiles in pairs plus one"
    groups = q_w // kv_w
    mod_spec, row_spec = _tile_specs(b_all, ct, d_model)
    rope_spec = pl.BlockSpec((TM, LANES), lambda b, t: (t, 0))
    cos, sa, sb = ropes

    q, k, vt, xr, gl = pl.pallas_call(
        functools.partial(_hy_in_kernel, q_w=q_w, kv_w=kv_w, rnn_w=rnn_w),
        grid=(b_all, nt),
        in_specs=[row_spec(d_model), mod_spec, _const_spec((1, d_model)), _const_spec((d_model, in_w)),
                  rope_spec, rope_spec, rope_spec, _const_spec((1, q_w)), _const_spec((1, kv_w)),
                  _const_spec((MXU, MXU)), _const_spec((kv_w, kv_w))],
        out_specs=[row_spec(q_w), row_spec(kv_w),
                   pl.BlockSpec((1, n_kv, HEAD + V_ONES, TM), lambda b, t: (b, 0, 0, t)),
                   row_spec(rnn_w), row_spec(rnn_w)],
        out_shape=[jax.ShapeDtypeStruct((b_all, t_all, q_w), BF16),
                   jax.ShapeDtypeStruct((b_all, t_all, kv_w), BF16),
                   jax.ShapeDtypeStruct((b_all, n_kv, HEAD + V_ONES, t_all), BF16),
                   jax.ShapeDtypeStruct((b_all, t_all, rnn_w), F32),
                   jax.ShapeDtypeStruct((b_all, t_all, rnn_w), F32)],
        compiler_params=_params(("parallel", "parallel")),
    )(xs, mods, g1.reshape(1, d_model), w_in.astype(BF16), cos, sa, sb,
      jnp.tile(qn, q_w // HEAD).reshape(1, q_w), jnp.tile(kn, kv_w // HEAD).reshape(1, kv_w),
      _blockdiag_ones(MXU, 1.0 / HEAD), _blockdiag_ones(kv_w, 1.0 / HEAD))

    att = pl.pallas_call(
        functools.partial(_attn_kernel, ct=ct, nt=nt, groups=groups, n_kv=n_kv),
        grid=(b_all, nt),
        in_specs=[row_spec(q_w),
                  pl.BlockSpec((1, t_all, kv_w), lambda b, t: (b, 0, 0)),
                  pl.BlockSpec((1, n_kv, HEAD + V_ONES, t_all), lambda b, t: (b, 0, 0, 0))],
        out_specs=row_spec(q_w),
        out_shape=jax.ShapeDtypeStruct((b_all, t_all, q_w), BF16),
        scratch_shapes=[pltpu.VMEM((q_w, TM), F32)] + [pltpu.VMEM((q_w // HEAD, TM, TM), F32)] * 2,
        compiler_params=_params(("parallel", "parallel")),
    )(q, k, vt)

    n_lc = rnn_w // RG_LANES
    per = RG_LANES // HEAD
    gw = gate_w.reshape(4, n_lc, per, HEAD, HEAD)
    eye = jnp.eye(per, dtype=F32)
    gw = jnp.einsum('gcpde,pq->cgpdqe', gw, eye).reshape(n_lc, 4, RG_LANES, RG_LANES).astype(BF16)
    lane_spec = lambda rows: pl.BlockSpec((rows, RG_LANES), lambda b, c: (0, c))
    seq_spec = pl.BlockSpec((1, t_all, RG_LANES), lambda b, c: (b, 0, c))
    rec = pl.pallas_call(
        functools.partial(_rglru_kernel, cl=cl, n=n),
        grid=(b_all, n_lc),
        in_specs=[seq_spec, seq_spec, lane_spec(CONV_K), lane_spec(1),
                  pl.BlockSpec((1, 4, RG_LANES, RG_LANES), lambda b, c: (c, 0, 0, 0)),
                  lane_spec(4), lane_spec(2)],
        out_specs=seq_spec,
        out_shape=jax.ShapeDtypeStruct((b_all, t_all, rnn_w), BF16),
        scratch_shapes=[pltpu.VMEM((t_all + 2 * SUBLANES, RG_LANES), F32),
                        pltpu.VMEM((2, t_all, RG_LANES), F32),
                        pltpu.VMEM((2, t_all, RG_LANES), F32)],
        compiler_params=_params(("parallel", "parallel")),
    )(xr, gl, conv_w, conv_b.reshape(1, rnn_w), gw, gate_b.reshape(4, rnn_w), lam)
    return (att, rec), w_out.astype(BF16)


def _rwkv_layer(xs, mods, g1, mu, w_rkv, lora_down, lora_up, lora_bias, gate_down, gate_up,
                k_k, k_a, r_k, gn_b, dims):
    b_all, cl, n, d_model = dims
    t_all = cl + n
    ct, nt = cl // TM, t_all // TM
    n8 = t_all // SUBLANES
    per8 = TM // SUBLANES
    lora = lora_down.shape[-1]
    glora = gate_down.shape[-1]
    mod_spec, row_spec = _tile_specs(b_all, ct, d_model)
    dir_spec = pl.BlockSpec((2, 1, TM, d_model), lambda b, t: (0, b, t, 0))
    wdw = jnp.concatenate([lora_down[0, 0], lora_down[1, 0]], axis=1).astype(BF16)
    wda = jnp.concatenate([lora_down[0, 1], lora_down[1, 1]], axis=1).astype(BF16)
    zeros = jnp.zeros((lora, d_model), F32)
    pad_up = lambda j: jnp.stack([jnp.concatenate([lora_up[0, j], zeros], axis=0),
                                  jnp.concatenate([zeros, lora_up[1, j]], axis=0)]).astype(BF16)
    vec = lambda a: a.reshape(1, d_model)
    outs = pl.pallas_call(
        functools.partial(_rw_in_kernel, ct=ct, nt=nt),
        grid=(b_all, nt),
        in_specs=[row_spec(d_model),
                  pl.BlockSpec((1, SUBLANES, d_model), lambda b, t: (b, jnp.maximum(t * per8 - 1, 0), 0)),
                  pl.BlockSpec((1, SUBLANES, d_model), lambda b, t: (b, jnp.minimum((t + 1) * per8, n8 - 1), 0)),
                  mod_spec, _const_spec((1, d_model)), _const_spec((6, d_model)),
                  _const_spec((3, d_model, d_model)), _const_spec((d_model, 2 * lora)),
                  _const_spec((d_model, 2 * lora)), _const_spec((2, 2 * lora, d_model)),
                  _const_spec((2, 2 * lora, d_model)), _const_spec((4, d_model)),
                  _const_spec((d_model, glora)), _const_spec((glora, d_model)),
                  _const_spec((1, d_model)), _const_spec((1, d_model)), _const_spec((1, d_model)),
                  _const_spec((1, d_model)), _const_spec((MXU, MXU))],
        out_specs=[row_spec(d_model), row_spec(d_model), row_spec(d_model),
                   dir_spec, dir_spec, dir_spec, row_spec(d_model), row_spec(d_model)],
        out_shape=[jax.ShapeDtypeStruct((b_all, t_all, d_model), F32)] * 3
        + [jax.ShapeDtypeStruct((2, b_all, t_all, d_model), F32)] * 3
        + [jax.ShapeDtypeStruct((b_all, t_all, d_model), F32)] * 2,
        compiler_params=_params(("parallel", "parallel")),
    )(xs, xs, xs, mods, vec(g1), mu, w_rkv.astype(BF16), wdw, wda, pad_up(0), pad_up(1),
      lora_bias.reshape(4, d_model), gate_down.astype(BF16), gate_up.astype(BF16),
      vec(k_k), vec(k_a), vec(r_k), vec(gn_b), _blockdiag_ones(MXU, 1.0))
    r, kk, v, lw, bb, kd, gate, z = outs

    def tmap(dd, i):
        rev = jnp.where(i < ct, ct - 1 - i, nt - 1 - (i - ct))
        return jnp.where(dd == 0, i, rev)

    blk_lanes = WKV_LANES * WKV_GROUPS
    shared = pl.BlockSpec((1, TM, blk_lanes), lambda dd, b, hh, i: (b, tmap(dd, i), hh))
    per_dir = pl.BlockSpec((1, 1, TM, blk_lanes), lambda dd, b, hh, i: (dd, b, tmap(dd, i), hh))
    y = pl.pallas_call(
        _wkv_kernel,
        grid=(2, b_all, d_model // blk_lanes, nt),
        in_specs=[shared, shared, shared, per_dir, per_dir, per_dir],
        out_specs=per_dir,
        out_shape=jax.ShapeDtypeStruct((2, b_all, t_all, d_model), F32),
        scratch_shapes=[pltpu.VMEM((WKV_GROUPS, WKV_LANES, WKV_LANES), F32)],
        compiler_params=_params(("arbitrary", "arbitrary", "arbitrary", "arbitrary")),
    )(r, kk, v, lw, bb, kd)
    return y, gate, z


def _post(xs, mix_inputs, w_o, mods, g2, w1, w2, dims, odd, gn_g=None, latent_only=False):
    b_all, cl, n, d_model = dims
    t_all = cl + n
    ct, nt = cl // TM, t_all // TM
    t0 = ct if latent_only else 0
    d_ff = w1.shape[1]
    mod_spec = pl.BlockSpec((1, 6, d_model), lambda b, t: (jnp.where(t + t0 < ct, b_all, b), 0, 0))
    row_spec = lambda w: pl.BlockSpec((1, TM, w), lambda b, t: (b, t + t0, 0))
    tail_specs = [_const_spec((d_model, d_model)), mod_spec, _const_spec((1, d_model)),
                  _const_spec((d_model, d_ff)), _const_spec((d_ff, d_model))]
    tail_args = (w_o, mods, g2.reshape(1, d_model), w1.astype(BF16), w2.astype(BF16))
    if odd:
        y, gate, z = mix_inputs
        y_spec = lambda dd: pl.BlockSpec((1, 1, TM, d_model), lambda b, t: (dd, b, t + t0, 0))
        in_specs = [row_spec(d_model), y_spec(0), y_spec(1), row_spec(d_model), row_spec(d_model),
                    _const_spec((1, d_model)), _const_spec((MXU, MXU))] + tail_specs
        args = (xs, y, y, gate, z, gn_g.reshape(1, d_model), _blockdiag_ones(MXU, 1.0 / HEAD)) + tail_args
    else:
        att, rec = mix_inputs
        in_specs = [row_spec(d_model), row_spec(att.shape[-1]), row_spec(rec.shape[-1])] + tail_specs
        args = (xs, att, rec) + tail_args
    rows_out = n if latent_only else t_all
    return pl.pallas_call(
        functools.partial(_post_kernel, odd=odd, ff_chunk=min(d_ff, 4 * MXU)),
        grid=(b_all, nt - t0),
        in_specs=in_specs,
        out_specs=pl.BlockSpec((1, TM, d_model), lambda b, t: (b, t, 0)),
        out_shape=jax.ShapeDtypeStruct((b_all, rows_out, d_model), F32),
        compiler_params=_params(("parallel", "parallel")),
    )(*args)


def kernel(x, c, ctx, c_ctx, ada_w, ada_b, norm_g, mlp_w1, mlp_w2, hy_w_in, hy_w_out, hy_q_norm, hy_k_norm, hy_conv_w, hy_conv_b, hy_gate_w, hy_gate_b, hy_lam, rw_mu, rw_w_rkv, rw_w_o, rw_lora_down, rw_lora_up, rw_lora_bias, rw_gate_down, rw_gate_up, rw_k_k, rw_k_a, rw_r_k, rw_gn_g, rw_gn_b):
    b_all, n, d_model = x.shape
    cl = ctx.shape[1]
    depth = ada_w.shape[0]
    assert cl % TM == 0 and n % TM == 0 and d_model % MXU == 0
    dims = (b_all, cl, n, d_model)
    mods_all = _ada_mods(jnp.concatenate([c, c_ctx[None, :]], axis=0), ada_w, ada_b)
    ropes = _rope_tables(cl, n, GRID_W)
    xs = jnp.concatenate([ctx, x], axis=1)
    for l in range(depth):
        i = l // 2
        mods = mods_all[l]
        last = l == depth - 1
        if l % 2 == 0:
            mix, w_o = _hybrid_layer(xs, mods, norm_g[l, 0], hy_w_in[i], hy_w_out[i], hy_q_norm[i], hy_k_norm[i],
                                     hy_conv_w[i], hy_conv_b[i], hy_gate_w[i], hy_gate_b[i], hy_lam[i], ropes, dims)
            xs = _post(xs, mix, w_o, mods, norm_g[l, 1], mlp_w1[l], mlp_w2[l], dims, odd=False, latent_only=last)
        else:
            mix = _rwkv_layer(xs, mods, norm_g[l, 0], rw_mu[i], rw_w_rkv[i], rw_lora_down[i], rw_lora_up[i],
                              rw_lora_bias[i], rw_gate_down[i], rw_gate_up[i], rw_k_k[i], rw_k_a[i],
                              rw_r_k[i], rw_gn_b[i], dims)
            xs = _post(xs, mix, rw_w_o[i].astype(BF16), mods, norm_g[l, 1], mlp_w1[l], mlp_w2[l], dims,
                       odd=True, gn_g=rw_gn_g[i], latent_only=last)
    return xs if xs.shape[1] == n else xs[:, cl:]
```
